```python
import math
import jax, jax.numpy as jnp
from jax import lax
import numpy as np


D_MODEL = 1024
BATCH = 8
SEQ = 8192
DEPTH = 4
DEC_BATCH = 32
DEC_SEQ = 16
PAST_LEN = 2048

CHUNK = 64
HEAD_DIM = 64
N_HEADS_A = 8
BAND_CHUNKS = 8
BAND_PAST = BAND_CHUNKS * CHUNK
MAX_REL = 128
N_HEADS_B = 4
WIDTH_A = N_HEADS_A * HEAD_DIM
WIDTH_B = N_HEADS_B * 2 * HEAD_DIM
ATTN_IN = 3 * WIDTH_A + 3 * WIDTH_B
ATTN_OUT = WIDTH_A + WIDTH_B
CONV_WIDTH = 3
D_FF = -(-8 * D_MODEL // (3 * 256)) * 256
ROPE_THETA = 10000.0
EPS = 1e-6
Q_BLOCK = 128
NEG = -1e30
N_ATTN_LAYERS = (DEPTH + 1) // 2
N_CONV_LAYERS = DEPTH // 2

kernel_name = "chunk_stream_hybrid_band_diff_conv"


def rms_norm(x, g):
    xf = x.astype(jnp.float32)
    y = xf * lax.rsqrt(jnp.mean(xf * xf, axis=-1, keepdims=True) + EPS)
    return (y * g.astype(jnp.float32)).astype(x.dtype)


def rotary(x, pos):
    half = x.shape[-1] // 2
    inv = ROPE_THETA ** (-jnp.arange(half, dtype=jnp.float32) / half)
    ang = pos.astype(jnp.float32)[:, None] * inv[None, :]
    shape = (ang.shape[0],) + (1,) * (x.ndim - 3) + (half,)
    c = jnp.cos(ang).reshape(shape)
    s = jnp.sin(ang).reshape(shape)
    xf = x.astype(jnp.float32)
    x1, x2 = xf[..., :half], xf[..., half:]
    return jnp.concatenate([x1 * c - x2 * s, x2 * c + x1 * s], axis=-1).astype(x.dtype)


def attn_project(h, w_in):
    b, s, _ = h.shape
    z = h @ w_in
    cuts = [WIDTH_A, 2 * WIDTH_A, 3 * WIDTH_A, 3 * WIDTH_A + WIDTH_B, 3 * WIDTH_A + 2 * WIDTH_B]
    qa, ka, va, qb, kb, vb = jnp.split(z, cuts, axis=-1)
    qa = qa.reshape(b, s, N_HEADS_A, HEAD_DIM)
    ka = ka.reshape(b, s, N_HEADS_A, HEAD_DIM)
    va = va.reshape(b, s, N_HEADS_A, HEAD_DIM)
    qb = qb.reshape(b, s, N_HEADS_B, 2, HEAD_DIM)
    kb = kb.reshape(b, s, N_HEADS_B, 2, HEAD_DIM)
    vb = vb.reshape(b, s, N_HEADS_B, 2 * HEAD_DIM)
    return qa, ka, va, qb, kb, vb


def rel_bias_block(table, q_offset, n_q, n_k):
    rel = q_offset + np.arange(n_q)[:, None] - np.arange(n_k)[None, :]
    idx = np.clip(rel, -MAX_REL, MAX_REL) + MAX_REL
    return table[:, idx].astype(jnp.float32)


def band_attn(q, k, v, bias, valid):
    s = jnp.einsum('bqhd,bkhd->bhqk', q, k).astype(jnp.float32) / math.sqrt(HEAD_DIM) + bias
    if valid is not None:
        s = jnp.where(valid, s, NEG)
    p = jax.nn.softmax(s, axis=-1).astype(v.dtype)
    return jnp.einsum('bhqk,bkhd->bqhd', p, v)


def diff_lambda(lq1, lk1, lq2, lk2, lam_init):
    f = jnp.float32
    return (jnp.exp(jnp.sum(lq1.astype(f) * lk1.astype(f)))
            - jnp.exp(jnp.sum(lq2.astype(f) * lk2.astype(f))) + lam_init)


def diff_attn(q, k, v, valid, lam, subln_g, lam_init):
    s = jnp.einsum('bqhmd,bkhmd->bhmqk', q, k).astype(jnp.float32) / math.sqrt(HEAD_DIM)
    if valid is not None:
        s = jnp.where(valid, s, NEG)
    p = jax.nn.softmax(s, axis=-1)
    a = (p[:, :, 0] - lam * p[:, :, 1]).astype(v.dtype)
    o = jnp.einsum('bhqk,bkhe->bqhe', a, v)
    return rms_norm(o, subln_g) * (1.0 - lam_init)


def attn_mixer_prompt(h, w_in, w_o, table, lam, subln_g, lam_init):
    b, s, _ = h.shape
    qa, ka, va, qb, kb, vb = attn_project(h, w_in)
    pos = jnp.arange(s)
    qb = rotary(qb, pos)
    kb = rotary(kb, pos)
    band = BAND_PAST + CHUNK
    kpad = jnp.pad(ka, ((0, 0), (BAND_PAST, 0), (0, 0), (0, 0)))
    vpad = jnp.pad(va, ((0, 0), (BAND_PAST, 0), (0, 0), (0, 0)))
    bias = rel_bias_block(table, BAND_PAST, CHUNK, band)

    def a_chunk(c):
        start = c * CHUNK
        qc = lax.dynamic_slice_in_dim(qa, start, CHUNK, axis=1)
        kc = lax.dynamic_slice_in_dim(kpad, start, band, axis=1)
        vc = lax.dynamic_slice_in_dim(vpad, start, band, axis=1)
        valid = (start - BAND_PAST + jnp.arange(band)) >= 0
        return band_attn(qc, kc, vc, bias, valid)

    oa = lax.map(a_chunk, jnp.arange(s // CHUNK))
    oa = jnp.moveaxis(oa, 0, 1).reshape(b, s, WIDTH_A)
    key_chunk = jnp.arange(s) // CHUNK

    def b_block(i):
        start = i * Q_BLOCK
        qblk = lax.dynamic_slice_in_dim(qb, start, Q_BLOCK, axis=1)
        q_chunk = (start + jnp.arange(Q_BLOCK)) // CHUNK
        valid = key_chunk[None, :] <= q_chunk[:, None]
        return diff_attn(qblk, kb, vb, valid, lam, subln_g, lam_init)

    ob = lax.map(b_block, jnp.arange(s // Q_BLOCK))
    ob = jnp.moveaxis(ob, 0, 1).reshape(b, s, WIDTH_B)
    y = jnp.concatenate([oa, ob], axis=-1) @ w_o
    keep = min(BAND_PAST, s)
    return y, ka[:, s - keep:], va[:, s - keep:], kb, vb


def attn_mixer_sample(h, ka_c, va_c, kb_c, vb_c, w_in, w_o, table, lam, subln_g, lam_init):
    b, t, _ = h.shape
    past = kb_c.shape[1]
    a_past = ka_c.shape[1]
    qa, ka, va, qb, kb, vb = attn_project(h, w_in)
    pos = past + jnp.arange(t)
    qb = rotary(qb, pos)
    kb = rotary(kb, pos)
    k_all = jnp.concatenate([ka_c, ka], axis=1)
    v_all = jnp.concatenate([va_c, va], axis=1)
    bias = rel_bias_block(table, a_past, t, a_past + t)
    oa = band_attn(qa, k_all, v_all, bias, None).reshape(b, t, WIDTH_A)
    kb_all = jnp.concatenate([kb_c, kb], axis=1)
    vb_all = jnp.concatenate([vb_c, vb], axis=1)
    ob = diff_attn(qb, kb_all, vb_all, None, lam, subln_g, lam_init).reshape(b, t, WIDTH_B)
    y = jnp.concatenate([oa, ob], axis=-1) @ w_o
    return y, ka, va, kb, vb


def conv_mixer(h, w_in, conv_w, w_out, conv_state):
    b, s, d = h.shape
    gate_b, gate_c, u = jnp.split(h @ w_in, 3, axis=-1)
    xin = gate_c * u
    if conv_state is None:
        conv_state = jnp.zeros((b, CONV_WIDTH - 1, d), xin.dtype)
    xpad = jnp.concatenate([conv_state.astype(xin.dtype), xin], axis=1)
    conv = lax.conv_general_dilated(xpad, conv_w[:, None, :].astype(xin.dtype), window_strides=(1,),
                                    padding='VALID', dimension_numbers=('NWC', 'WIO', 'NWC'),
                                    feature_group_count=d)
    y = (gate_b * conv) @ w_out
    return y, xpad[:, -(CONV_WIDTH - 1):]


def swiglu(h, wg, wu, wd):
    return (jax.nn.silu(h @ wg) * (h @ wu)) @ wd


def setup_inputs(seed: int = 0) -> dict:
    key = jax.random.key(seed)
    ks = jax.random.split(key, 32)
    f = jnp.float32
    a_past = min(BAND_PAST, PAST_LEN)
    nrm = lambda k, shape, scale: jax.random.normal(k, shape, f) * scale
    return {
        'x_prompt': nrm(ks[0], (BATCH, SEQ, D_MODEL), 1.0),
        'x_sample': nrm(ks[1], (DEC_BATCH, DEC_SEQ, D_MODEL), 1.0),
        'cache_a_k': nrm(ks[2], (N_ATTN_LAYERS, DEC_BATCH, a_past, N_HEADS_A, HEAD_DIM), 1.0),
        'cache_a_v': nrm(ks[3], (N_ATTN_LAYERS, DEC_BATCH, a_past, N_HEADS_A, HEAD_DIM), 1.0),
        'cache_b_k': nrm(ks[4], (N_ATTN_LAYERS, DEC_BATCH, PAST_LEN, N_HEADS_B, 2, HEAD_DIM), 1.0),
        'cache_b_v': nrm(ks[5], (N_ATTN_LAYERS, DEC_BATCH, PAST_LEN, N_HEADS_B, 2 * HEAD_DIM), 1.0),
        'state_conv': nrm(ks[6], (N_CONV_LAYERS, DEC_BATCH, CONV_WIDTH - 1, D_MODEL), 1.0),
        'norm_mix': 1.0 + nrm(ks[7], (DEPTH, D_MODEL), 0.02),
        'norm_ffn': 1.0 + nrm(ks[8], (DEPTH, D_MODEL), 0.02),
        'norm_final': 1.0 + nrm(ks[9], (D_MODEL,), 0.02),
        'w_attn_in': nrm(ks[10], (N_ATTN_LAYERS, D_MODEL, ATTN_IN), D_MODEL ** -0.5),
        'w_attn_out': nrm(ks[11], (N_ATTN_LAYERS, ATTN_OUT, D_MODEL), ATTN_OUT ** -0.5),
        'rel_bias': nrm(ks[12], (N_ATTN_LAYERS, N_HEADS_A, 2 * MAX_REL + 1), 0.1),
        'lambda_q1': nrm(ks[13], (N_ATTN_LAYERS, HEAD_DIM), 0.1),
        'lambda_k1': nrm(ks[14], (N_ATTN_LAYERS, HEAD_DIM), 0.1),
        'lambda_q2': nrm(ks[15], (N_ATTN_LAYERS, HEAD_DIM), 0.1),
        'lambda_k2': nrm(ks[16], (N_ATTN_LAYERS, HEAD_DIM), 0.1),
        'subln_g': 1.0 + nrm(ks[17], (N_ATTN_LAYERS, 2 * HEAD_DIM), 0.02),
        'w_conv_in': nrm(ks[18], (N_CONV_LAYERS, D_MODEL, 3 * D_MODEL), D_MODEL ** -0.5),
        'conv_w': nrm(ks[19], (N_CONV_LAYERS, CONV_WIDTH, D_MODEL), CONV_WIDTH ** -0.5),
        'w_conv_out': nrm(ks[20], (N_CONV_LAYERS, D_MODEL, D_MODEL), D_MODEL ** -0.5),
        'w_ffn_gate': nrm(ks[21], (DEPTH, D_MODEL, D_FF), D_MODEL ** -0.5),
        'w_ffn_up': nrm(ks[22], (DEPTH, D_MODEL, D_FF), D_MODEL ** -0.5),
        'w_ffn_down': nrm(ks[23], (DEPTH, D_FF, D_MODEL), D_FF ** -0.5),
    }


def reference(x_prompt, x_sample, cache_a_k, cache_a_v, cache_b_k, cache_b_v, state_conv,
              norm_mix, norm_ffn, norm_final, w_attn_in, w_attn_out, rel_bias,
              lambda_q1, lambda_k1, lambda_q2, lambda_k2, subln_g,
              w_conv_in, conv_w, w_conv_out, w_ffn_gate, w_ffn_up, w_ffn_down):
    xp, xs = x_prompt, x_sample
    pak, pav, pbk, pbv, pcs = [], [], [], [], []
    sak, sav, sbk, sbv, scs = [], [], [], [], []
    for i in range(DEPTH):
        j = i // 2
        hp = rms_norm(xp, norm_mix[i])
        hs = rms_norm(xs, norm_mix[i])
        if i % 2 == 0:
            lam_init = 0.8 - 0.6 * math.exp(-0.3 * i)
            lam = diff_lambda(lambda_q1[j], lambda_k1[j], lambda_q2[j], lambda_k2[j], lam_init)
            mp, ka, va, kb, vb = attn_mixer_prompt(hp, w_attn_in[j], w_attn_out[j], rel_bias[j],
                                                   lam, subln_g[j], lam_init)
            pak.append(ka); pav.append(va); pbk.append(kb); pbv.append(vb)
            ms, ka, va, kb, vb = attn_mixer_sample(hs, cache_a_k[j], cache_a_v[j], cache_b_k[j], cache_b_v[j],
                                                   w_attn_in[j], w_attn_out[j], rel_bias[j],
                                                   lam, subln_g[j], lam_init)
            sak.append(ka); sav.append(va); sbk.append(kb); sbv.append(vb)
        else:
            mp, cp = conv_mixer(hp, w_conv_in[j], conv_w[j], w_conv_out[j], None)
            ms, cs = conv_mixer(hs, w_conv_in[j], conv_w[j], w_conv_out[j], state_conv[j])
            pcs.append(cp); scs.append(cs)
        xp = xp + mp
        xs = xs + ms
        xp = xp + swiglu(rms_norm(xp, norm_ffn[i]), w_ffn_gate[i], w_ffn_up[i], w_ffn_down[i])
        xs = xs + swiglu(rms_norm(xs, norm_ffn[i]), w_ffn_gate[i], w_ffn_up[i], w_ffn_down[i])
    y_prompt = rms_norm(xp, norm_final)
    y_sample = rms_norm(xs, norm_final)
    return (y_prompt, y_sample,
            jnp.stack(pak), jnp.stack(pav), jnp.stack(pbk), jnp.stack(pbv), jnp.stack(pcs),
            jnp.stack(sak), jnp.stack(sav), jnp.stack(sbk), jnp.stack(sbv), jnp.stack(scs))
```

```python
import functools
import math

import jax
import jax.numpy as jnp
from jax import lax
from jax.experimental import pallas as pl
from jax.experimental.pallas import tpu as pltpu

F32 = jnp.float32
BF16 = jnp.bfloat16

EPS = 1e-6
NEG = -1e30
ROPE_THETA = 10000.0
CHUNK = 64
HEAD_DIM = 64
N_HEADS_A = 8
N_HEADS_B = 4
BAND_PAST = 512
MAX_REL = 128
BAND_KEYS = BAND_PAST + CHUNK
BIAS_LANES = 640
WIDTH = N_HEADS_A * HEAD_DIM
LANES = 128
ROW_TILE = 512
FF_CHUNK = 256
VMEM_LIMIT = 56 * 1024 * 1024

_NT = (((1,), (1,)), ((), ()))


def _resident(shape):
    return pl.BlockSpec(shape, lambda *_: (0,) * len(shape), pipeline_mode=pl.Buffered(1))


def _rms(x, g):
    return x * lax.rsqrt(jnp.mean(x * x, axis=-1, keepdims=True) + EPS) * g


def _params(*sem):
    return pltpu.CompilerParams(dimension_semantics=sem, vmem_limit_bytes=VMEM_LIMIT)


def _attn_in_kernel(x_ref, g_ref, w_ref, cos_ref, sin_ref,
                    qa_ref, ka_ref, va_ref, qb_ref, kb_ref, vb_ref,
                    kaf_ref, vaf_ref, kbf_ref, vbf_ref):
    h = _rms(x_ref[...], g_ref[...]).astype(BF16)
    scale = 1.0 / math.sqrt(HEAD_DIM)

    def proj(c):
        return jnp.dot(h, w_ref[:, c * WIDTH:(c + 1) * WIDTH], preferred_element_type=F32)

    cos = cos_ref[...]
    sin = sin_ref[...]
    low_half = (lax.broadcasted_iota(jnp.int32, cos.shape, 1) % HEAD_DIM) < (HEAD_DIM // 2)

    def rotary(z):
        parts = []
        for j in range(WIDTH // LANES):
            zj = z[:, j * LANES:(j + 1) * LANES]
            swapped = jnp.where(low_half, pltpu.roll(zj, LANES - HEAD_DIM // 2, 1),
                                pltpu.roll(zj, HEAD_DIM // 2, 1))
            parts.append(zj * cos + swapped * sin)
        return jnp.concatenate(parts, axis=1)

    qa_ref[...] = (proj(0) * scale).astype(BF16)
    ka = proj(1)
    ka_ref[...] = ka.astype(BF16)
    kaf_ref[...] = ka
    va = proj(2)
    va_ref[...] = va.astype(BF16)
    vaf_ref[...] = va
    qb_ref[...] = (rotary(proj(3)) * scale).astype(BF16)
    kb = rotary(proj(4))
    kb_ref[...] = kb.astype(BF16)
    kbf_ref[...] = kb
    vb = proj(5)
    vb_ref[...] = vb.astype(BF16)
    vbf_ref[...] = vb


def _attn_in_proj(x, g, w, cos_t, sin_t, *, seg_len, keep):
    m, d = x.shape
    tm = min(ROW_TILE, m)
    n_tiles = m // tm
    n_tab = cos_t.shape[0] // tm
    n_seg = m // seg_len
    if seg_len >= tm:
        tps, kpt = seg_len // tm, keep // tm

        def keep_idx(i):
            return ((i // tps) * kpt + jnp.clip(i % tps - (tps - kpt), 0, kpt - 1), 0)
    else:
        assert keep == seg_len

        def keep_idx(i):
            return (i, 0)

    row = lambda i: (i, 0)
    tab = lambda i: (i % n_tab, 0)
    bf_out = jax.ShapeDtypeStruct((m, WIDTH), BF16)
    f_out = jax.ShapeDtypeStruct((m, WIDTH), F32)
    keep_out = jax.ShapeDtypeStruct((n_seg * keep, WIDTH), F32)
    blk = pl.BlockSpec((tm, WIDTH), row)
    return pl.pallas_call(
        _attn_in_kernel,
        grid=(n_tiles,),
        in_specs=[pl.BlockSpec((tm, d), row), _resident((1, d)), _resident(w.shape),
                  pl.BlockSpec((tm, LANES), tab), pl.BlockSpec((tm, LANES), tab)],
        out_specs=[blk] * 6 + [pl.BlockSpec((tm, WIDTH), keep_idx)] * 2 + [blk] * 2,
        out_shape=[bf_out] * 6 + [keep_out] * 2 + [f_out] * 2,
        compiler_params=_params("arbitrary"),
        name="attn_in_proj",
    )(x, g, w, cos_t, sin_t)


def _rope_tables(pos):
    half = HEAD_DIM // 2
    inv = ROPE_THETA ** (-jnp.arange(half, dtype=F32) / half)
    ang = pos.astype(F32)[:, None] * inv[None, :]
    c, s = jnp.cos(ang), jnp.sin(ang)
    reps = LANES // HEAD_DIM
    return jnp.tile(c, (1, 2 * reps)), jnp.tile(jnp.concatenate([-s, s], axis=1), (1, reps))


def _bias_kernel(tab_ref, o_ref):
    r = pl.program_id(0)
    qi = lax.broadcasted_iota(jnp.int32, (CHUNK, BIAS_LANES), 0)
    kj = lax.broadcasted_iota(jnp.int32, (CHUNK, BIAS_LANES), 1)
    idx = jnp.clip(BAND_PAST + qi - kj, -MAX_REL, MAX_REL) + MAX_REL

    def body(d, acc):
        return jnp.where(idx == d, tab_ref[r, d], acc)

    o_ref[0] = lax.fori_loop(0, 2 * MAX_REL + 1, body, jnp.zeros((CHUNK, BIAS_LANES), F32))


def _build_bias(table):
    n = table.shape[0]
    return pl.pallas_call(
        _bias_kernel,
        grid=(n,),
        in_specs=[pl.BlockSpec(memory_space=pltpu.SMEM)],
        out_specs=pl.BlockSpec((1, CHUNK, BIAS_LANES), lambda r: (r, 0, 0)),
        out_shape=jax.ShapeDtypeStruct((n, CHUNK, BIAS_LANES), F32),
        compiler_params=_params("arbitrary"),
        name="build_bias",
    )(table)


def _band_kernel(q_ref, kp_ref, kc_ref, vp_ref, vc_ref, bias_ref, o_ref):
    qblk = pl.program_id(2)
    q = q_ref[0]
    low = lax.broadcasted_iota(jnp.int32, q.shape, 1) < HEAD_DIM

    def attend(first):
        if first:
            k, v = kc_ref[0], vc_ref[0]
        else:
            k = jnp.concatenate([kp_ref[0], kc_ref[0]], axis=0)
            v = jnp.concatenate([vp_ref[0], vc_ref[0]], axis=0)
        outs = []
        for hh in range(2):
            qm = jnp.where(low if hh == 0 else jnp.logical_not(low), q, jnp.zeros_like(q))
            s = lax.dot_general(qm, k, _NT, preferred_element_type=F32)
            s = s + (bias_ref[hh, :, BAND_PAST:] if first else bias_ref[hh])
            p = jnp.exp(s - jnp.max(s, axis=-1, keepdims=True))
            l = jnp.sum(p, axis=-1, keepdims=True)
            outs.append(jnp.dot(p.astype(BF16), v, preferred_element_type=F32) / l)
        o_ref[0] = jnp.where(low, outs[0], outs[1]).astype(BF16)

    pl.when(qblk == 0)(functools.partial(attend, True))
    pl.when(qblk > 0)(functools.partial(attend, False))


def _band_attn(q, k, v, bias_blocks):
    b, s, _ = q.shape
    bq = BAND_PAST
    cur = lambda hp, bi, qi: (bi, qi, hp)
    prev = lambda hp, bi, qi: (bi, jnp.maximum(qi - 1, 0), hp)
    blk = (1, bq, LANES)
    return pl.pallas_call(
        _band_kernel,
        grid=(N_HEADS_A // 2, b, s // bq),
        in_specs=[pl.BlockSpec(blk, cur), pl.BlockSpec(blk, prev), pl.BlockSpec(blk, cur),
                  pl.BlockSpec(blk, prev), pl.BlockSpec(blk, cur),
                  pl.BlockSpec((2, bq, 2 * bq), lambda hp, bi, qi: (hp, 0, 0))],
        out_specs=pl.BlockSpec(blk, cur),
        out_shape=jax.ShapeDtypeStruct((b, s, WIDTH), BF16),
        compiler_params=_params("arbitrary", "arbitrary", "arbitrary"),
        name="band_attn",
    )(q, k, k, v, v, bias_blocks)


def _band_bias_blocks(bias):
    n_chunks = BAND_PAST // CHUNK
    rows = [jnp.pad(bias, ((0, 0), (0, 0), (CHUNK * c, 2 * BAND_PAST - BAND_KEYS - CHUNK * c)),
                    constant_values=NEG) for c in range(n_chunks)]
    return jnp.stack(rows, axis=1).reshape(bias.shape[0], BAND_PAST, 2 * BAND_PAST)


def _lambda(lam_ref, lam_init):
    lp = lam_ref[...]
    a1 = jnp.sum(lp[0:1] * lp[1:2], axis=-1, keepdims=True)
    a2 = jnp.sum(lp[2:3] * lp[3:4], axis=-1, keepdims=True)
    return jnp.exp(a1) - jnp.exp(a2) + lam_init


def _diff_kernel(lam_ref, g_ref, q_ref, k_ref, v_ref, o_ref, m_ref, l_ref, acc_ref, *, lam_init):
    qi = pl.program_id(2)
    q = q_ref[0]
    bq = q.shape[0]
    low = lax.broadcasted_iota(jnp.int32, q.shape, 1) < HEAD_DIM
    zero = jnp.zeros_like(q)
    q2 = jnp.concatenate([jnp.where(low, q, zero), jnp.where(low, zero, q)], axis=0)

    m_ref[...] = jnp.full(m_ref.shape, NEG, F32)
    l_ref[...] = jnp.zeros(l_ref.shape, F32)
    acc_ref[...] = jnp.zeros(acc_ref.shape, F32)

    def step(kj, diagonal):
        start = pl.multiple_of(kj * bq, bq)
        k = k_ref[0, pl.ds(start, bq), :]
        v = v_ref[0, pl.ds(start, bq), :]
        s = lax.dot_general(q2, k, _NT, preferred_element_type=F32)
        if diagonal:
            q_chunk = (lax.broadcasted_iota(jnp.int32, s.shape, 0) % bq) // CHUNK
            k_chunk = lax.broadcasted_iota(jnp.int32, s.shape, 1) // CHUNK
            s = jnp.where(k_chunk <= q_chunk, s, NEG)
        m_old = m_ref[...]
        m_new = jnp.maximum(m_old, jnp.max(s, axis=-1, keepdims=True))
        alpha = jnp.exp(m_old - m_new)
        p = jnp.exp(s - m_new)
        l_ref[...] = alpha * l_ref[...] + jnp.sum(p, axis=-1, keepdims=True)
        acc_ref[...] = alpha * acc_ref[...] + jnp.dot(p.astype(BF16), v, preferred_element_type=F32)
        m_ref[...] = m_new

    def full_step(kj, carry):
        step(kj, False)
        return carry

    lax.fori_loop(0, qi, full_step, 0)
    step(qi, True)

    o = acc_ref[...] / l_ref[...]
    lam = _lambda(lam_ref, lam_init)
    d = o[:bq] - lam * o[bq:]
    o_ref[0] = (_rms(d, g_ref[...]) * (1.0 - lam_init)).astype(BF16)


def _diff_attn(q, k, v, lam_p, g, lam_init):
    b, s, _ = q.shape
    bq = min(ROW_TILE, s)
    seq = pl.BlockSpec((1, s, LANES), lambda bi, h, qi: (bi, 0, h))
    blk = pl.BlockSpec((1, bq, LANES), lambda bi, h, qi: (bi, qi, h))
    return pl.pallas_call(
        functools.partial(_diff_kernel, lam_init=lam_init),
        grid=(b, N_HEADS_B, s // bq),
        in_specs=[_resident(lam_p.shape), _resident(g.shape), blk, seq, seq],
        out_specs=blk,
        out_shape=jax.ShapeDtypeStruct((b, s, WIDTH), BF16),
        scratch_shapes=[pltpu.VMEM((2 * bq, 1), F32), pltpu.VMEM((2 * bq, 1), F32),
                        pltpu.VMEM((2 * bq, LANES), F32)],
        compiler_params=_params("arbitrary", "arbitrary", "arbitrary"),
        name="diff_attn",
    )(lam_p, g, q, k, v)


def _sample_kernel(lam_ref, g_ref, qa_ref, kan_ref, van_ref, qb_ref, kbn_ref, vbn_ref,
                   cak_ref, cav_ref, cbk_ref, cbv_ref, bias_c_ref, bias_n_ref,
                   oa_ref, ob_ref, *, lam_init):
    t = qa_ref.shape[0]
    n_grp = WIDTH // HEAD_DIM
    rows = n_grp * t
    grp_of_row = lax.broadcasted_iota(jnp.int32, (rows, WIDTH), 0) // t
    grp_of_lane = lax.broadcasted_iota(jnp.int32, (rows, WIDTH), 1) // HEAD_DIM
    own = grp_of_row == grp_of_lane
    lane_t = lax.broadcasted_iota(jnp.int32, (t, WIDTH), 1)

    def expand(q):
        qe = jnp.concatenate([q] * n_grp, axis=0)
        return jnp.where(own, qe, jnp.zeros_like(qe))

    def attend(qe, k_cache, k_new, v_cache, v_new, bias_c, bias_n):
        s_c = lax.dot_general(qe, k_cache, _NT, preferred_element_type=F32)
        s_n = lax.dot_general(qe, k_new, _NT, preferred_element_type=F32)
        if bias_c is not None:
            s_c = s_c + bias_c
            s_n = s_n + bias_n
        m = jnp.maximum(jnp.max(s_c, axis=-1, keepdims=True), jnp.max(s_n, axis=-1, keepdims=True))
        p_c = jnp.exp(s_c - m)
        p_n = jnp.exp(s_n - m)
        l = jnp.sum(p_c, axis=-1, keepdims=True) + jnp.sum(p_n, axis=-1, keepdims=True)
        o = (jnp.dot(p_c.astype(BF16), v_cache, preferred_element_type=F32)
             + jnp.dot(p_n.astype(BF16), v_new, preferred_element_type=F32))
        return o / l

    oa_all = attend(expand(qa_ref[...]), cak_ref[0].astype(BF16), kan_ref[...],
                    cav_ref[0].astype(BF16), van_ref[...], bias_c_ref[...], bias_n_ref[...])
    oa = jnp.zeros((t, WIDTH), F32)
    for h in range(N_HEADS_A):
        oa = jnp.where(lane_t // HEAD_DIM == h, oa_all[h * t:(h + 1) * t], oa)
    oa_ref[...] = oa.astype(BF16)

    ob_all = attend(expand(qb_ref[...]), cbk_ref[0].astype(BF16), kbn_ref[...],
                    cbv_ref[0].astype(BF16), vbn_ref[...], None, None)
    lam = _lambda(lam_ref, lam_init)
    g = g_ref[...]
    outs = []
    for h in range(N_HEADS_B):
        d = ob_all[2 * h * t:(2 * h + 1) * t] - lam * ob_all[(2 * h + 1) * t:(2 * h + 2) * t]
        outs.append(_rms(d[:, h * LANES:(h + 1) * LANES], g) * (1.0 - lam_init))
    ob_ref[...] = jnp.concatenate(outs, axis=1).astype(BF16)


def _sample_attn(qa, ka, va, qb, kb, vb, cak, cav, cbk, cbv, bias_c, bias_n, lam_p, g, lam_init, t):
    m = qa.shape[0]
    bs = m // t
    new = pl.BlockSpec((t, WIDTH), lambda i: (i, 0))
    cache = lambda c: pl.BlockSpec((1,) + c.shape[1:], lambda i: (i, 0, 0))
    out = jax.ShapeDtypeStruct((m, WIDTH), BF16)
    return pl.pallas_call(
        functools.partial(_sample_kernel, lam_init=lam_init),
        grid=(bs,),
        in_specs=[_resident(lam_p.shape), _resident(g.shape)] + [new] * 6
                 + [cache(cak), cache(cav), cache(cbk), cache(cbv),
                    _resident(bias_c.shape), _resident(bias_n.shape)],
        out_specs=[new, new],
        out_shape=[out, out],
        compiler_params=_params("arbitrary"),
        name="sample_attn",
    )(lam_p, g, qa, ka, va, qb, kb, vb, cak, cav, cbk, cbv, bias_c, bias_n)


def _conv_kernel(x_ref, g_ref, w_ref, cw_ref, st_ref, y_ref, tail_ref, carry_ref, *, tps, seg_len):
    tm, d = x_ref.shape
    h = _rms(x_ref[...], g_ref[...]).astype(BF16)
    carried = seg_len >= tm
    if carried:
        @pl.when(pl.program_id(0) % tps == 0)
        def _():
            carry_ref[...] = st_ref[0]

    cw = cw_ref[...]
    cc = 2 * FF_CHUNK
    for c in range(d // cc):
        cols = slice(c * cc, (c + 1) * cc)
        proj = lambda part: jnp.dot(h, w_ref[:, part * d + c * cc:part * d + (c + 1) * cc],
                                    preferred_element_type=F32)
        xin = proj(1) * proj(2)
        r = lax.broadcasted_iota(jnp.int32, xin.shape, 0)
        x1 = pltpu.roll(xin, 1, 0)
        x2 = pltpu.roll(xin, 2, 0)
        if carried:
            prev = carry_ref[:, cols]
            x1 = jnp.where(r == 0, prev[7:8], x1)
            x2 = jnp.where(r == 0, prev[6:7], jnp.where(r == 1, prev[7:8], x2))
            carry_ref[:, cols] = xin[tm - 8:]
            tail_ref[0, :, cols] = xin[tm - 8:]
        else:
            t = r % seg_len
            x1 = jnp.where(t >= 1, x1, st_ref[0, :, cols])
            x2 = jnp.where(t >= 2, x2, st_ref[1, :, cols])
            tail_ref[:, cols] = xin
        conv = cw[0:1, cols] * x2 + cw[1:2, cols] * x1 + cw[2:3, cols] * xin
        y_ref[:, cols] = (proj(0) * conv).astype(BF16)


def _conv_in(x, g, w, cw, state, *, seg_len):
    m, d = x.shape
    tm = min(ROW_TILE, m)
    row = lambda i: (i, 0)
    if seg_len >= tm:
        tps = seg_len // tm
        st_spec = pl.BlockSpec((1, 8, d), lambda i: (i // tps, 0, 0))
        tail_spec = pl.BlockSpec((1, 8, d), lambda i: (i // tps, 0, 0))
        tail_shape = jax.ShapeDtypeStruct((m // seg_len, 8, d), F32)
    else:
        tps = 1
        st_spec = pl.BlockSpec((2, tm, d), lambda i: (0, i, 0))
        tail_spec = pl.BlockSpec((tm, d), row)
        tail_shape = jax.ShapeDtypeStruct((m, d), F32)
    return pl.pallas_call(
        functools.partial(_conv_kernel, tps=tps, seg_len=seg_len),
        grid=(m // tm,),
        in_specs=[pl.BlockSpec((tm, d), row), _resident((1, d)), _resident(w.shape),
                  _resident(cw.shape), st_spec],
        out_specs=[pl.BlockSpec((tm, d), row), tail_spec],
        out_shape=[jax.ShapeDtypeStruct((m, d), BF16), tail_shape],
        scratch_shapes=[pltpu.VMEM((8, d), F32)],
        compiler_params=_params("arbitrary"),
        name="conv_in",
    )(x, g, w, cw, state)


def _ffn_kernel(*refs, n_parts, final):
    x_ref = refs[0]
    a_refs = refs[1:1 + n_parts]
    wo_ref, g_ref, wg_ref, wu_ref, wd_ref = refs[1 + n_parts:6 + n_parts]
    gf_ref = refs[6 + n_parts] if final else None
    o_ref, act_ref = refs[-2], refs[-1]

    a = a_refs[0][...] if n_parts == 1 else jnp.concatenate([r[...] for r in a_refs], axis=1)
    x1 = x_ref[...] + jnp.dot(a, wo_ref[...], preferred_element_type=F32)
    h = _rms(x1, g_ref[...]).astype(BF16)
    d_ff = wg_ref.shape[1]
    for c in range(d_ff // FF_CHUNK):
        cols = slice(c * FF_CHUNK, (c + 1) * FF_CHUNK)
        gate = jnp.dot(h, wg_ref[:, cols], preferred_element_type=F32)
        up = jnp.dot(h, wu_ref[:, cols], preferred_element_type=F32)
        act_ref[:, cols] = (gate * (1.0 / (1.0 + jnp.exp(-gate))) * up).astype(BF16)
    y = x1 + jnp.dot(act_ref[...], wd_ref[...], preferred_element_type=F32)
    o_ref[...] = _rms(y, gf_ref[...]) if final else y


def _mix_out_ffn(x, parts, wo, g, wg, wu, wd, g_final=None):
    m, d = x.shape
    tm = min(ROW_TILE, m)
    row = lambda i: (i, 0)
    final = g_final is not None
    extra = [g_final] if final else []
    return pl.pallas_call(
        functools.partial(_ffn_kernel, n_parts=len(parts), final=final),
        grid=(m // tm,),
        in_specs=[pl.BlockSpec((tm, d), row)] + [pl.BlockSpec((tm, p.shape[1]), row) for p in parts]
                 + [_resident(wo.shape), _resident((1, d)), _resident(wg.shape), _resident(wu.shape),
                    _resident(wd.shape)] + [_resident((1, d))] * len(extra),
        out_specs=pl.BlockSpec((tm, d), row),
        out_shape=jax.ShapeDtypeStruct((m, d), F32),
        scratch_shapes=[pltpu.VMEM((tm, wg.shape[1]), BF16)],
        compiler_params=_params("arbitrary"),
        name="mix_out_ffn",
    )(x, *parts, wo, g, wg, wu, wd, *extra)


def kernel(x_prompt, x_sample, cache_a_k, cache_a_v, cache_b_k, cache_b_v, state_conv, norm_mix, norm_ffn, norm_final, w_attn_in, w_attn_out, rel_bias, lambda_q1, lambda_k1, lambda_q2, lambda_k2, subln_g, w_conv_in, conv_w, w_conv_out, w_ffn_gate, w_ffn_up, w_ffn_down):
    b, s, d = x_prompt.shape
    bs, t, _ = x_sample.shape
    depth = norm_mix.shape[0]
    n_attn = w_attn_in.shape[0]
    past = cache_b_k.shape[2]
    a_past = cache_a_k.shape[2]
    keep = min(BAND_PAST, s)
    assert s % BAND_PAST == 0 and a_past == BAND_PAST and d % (2 * FF_CHUNK) == 0
    assert w_ffn_gate.shape[2] % FF_CHUNK == 0 and t >= 2

    xp = x_prompt.reshape(b * s, d)
    xs = x_sample.reshape(bs * t, d)
    row = lambda v: v.reshape(1, -1)

    cos_p, sin_p = _rope_tables(jnp.arange(s))
    cos_s, sin_s = (jnp.tile(tab, (bs, 1)) for tab in _rope_tables(past + jnp.arange(t)))

    bias = _build_bias(rel_bias.reshape(n_attn * N_HEADS_A, 2 * MAX_REL + 1))
    bias = bias.reshape(n_attn, N_HEADS_A, CHUNK, BIAS_LANES)[..., :BAND_KEYS]

    pak, pav, pbk, pbv, pcs = [], [], [], [], []
    sak, sav, sbk, sbv, scs = [], [], [], [], []
    for i in range(depth):
        j = i // 2
        g_mix = row(norm_mix[i])
        if i % 2 == 0:
            lam_init = 0.8 - 0.6 * math.exp(-0.3 * i)
            w_in = w_attn_in[j].astype(BF16)
            wo = w_attn_out[j].astype(BF16)
            lam_p = jnp.stack([lambda_q1[j], lambda_k1[j], lambda_q2[j], lambda_k2[j]])
            g_sub = row(subln_g[j])

            qa, ka, va, qb, kb, vb, kaf, vaf, kbf, vbf = _attn_in_proj(
                xp, g_mix, w_in, cos_p, sin_p, seg_len=s, keep=keep)
            pak.append(kaf.reshape(b, keep, N_HEADS_A, HEAD_DIM))
            pav.append(vaf.reshape(b, keep, N_HEADS_A, HEAD_DIM))
            pbk.append(kbf.reshape(b, s, N_HEADS_B, 2, HEAD_DIM))
            pbv.append(vbf.reshape(b, s, N_HEADS_B, 2 * HEAD_DIM))
            shp = (b, s, WIDTH)
            oa = _band_attn(qa.reshape(shp), ka.reshape(shp), va.reshape(shp), _band_bias_blocks(bias[j]))
            ob = _diff_attn(qb.reshape(shp), kb.reshape(shp), vb.reshape(shp), lam_p, g_sub, lam_init)
            parts_p = [oa.reshape(b * s, WIDTH), ob.reshape(b * s, WIDTH)]

            qa, ka, va, qb, kb, vb, kaf, vaf, kbf, vbf = _attn_in_proj(
                xs, g_mix, w_in, cos_s, sin_s, seg_len=t, keep=t)
            sak.append(kaf.reshape(bs, t, N_HEADS_A, HEAD_DIM))
            sav.append(vaf.reshape(bs, t, N_HEADS_A, HEAD_DIM))
            sbk.append(kbf.reshape(bs, t, N_HEADS_B, 2, HEAD_DIM))
            sbv.append(vbf.reshape(bs, t, N_HEADS_B, 2 * HEAD_DIM))
            bias_s = bias[j][:, :t, :a_past + t].reshape(N_HEADS_A * t, a_past + t)
            oa, ob = _sample_attn(
                qa, ka, va, qb, kb, vb,
                cache_a_k[j].reshape(bs, a_past, WIDTH), cache_a_v[j].reshape(bs, a_past, WIDTH),
                cache_b_k[j].reshape(bs, past, WIDTH), cache_b_v[j].reshape(bs, past, WIDTH),
                bias_s[:, :a_past], bias_s[:, a_past:], lam_p, g_sub, lam_init, t)
            parts_s = [oa, ob]
        else:
            w_in = w_conv_in[j].astype(BF16)
            wo = w_conv_out[j].astype(BF16)
            yp, tail = _conv_in(xp, g_mix, w_in, conv_w[j], jnp.zeros((b, 8, d), F32), seg_len=s)
            pcs.append(tail[:, 6:])
            st = state_conv[j]
            stand_in = jnp.stack([jnp.pad(st[:, 1:2], ((0, 0), (0, t - 1), (0, 0))),
                                  jnp.pad(st, ((0, 0), (0, t - 2), (0, 0)))]).reshape(2, bs * t, d)
            ys, xin = _conv_in(xs, g_mix, w_in, conv_w[j], stand_in, seg_len=t)
            scs.append(xin.reshape(bs, t, d)[:, t - 2:])
            parts_p, parts_s = [yp], [ys]

        ffn = (wo, row(norm_ffn[i]), w_ffn_gate[i].astype(BF16), w_ffn_up[i].astype(BF16),
               w_ffn_down[i].astype(BF16))
        g_final = row(norm_final) if i == depth - 1 else None
        xp = _mix_out_ffn(xp, parts_p, *ffn, g_final=g_final)
        xs = _mix_out_ffn(xs, parts_s, *ffn, g_final=g_final)

    return (xp.reshape(b, s, d), xs.reshape(bs, t, d),
            jnp.stack(pak), jnp.stack(pav), jnp.stack(pbk), jnp.stack(pbv), jnp.stack(pcs),
            jnp.stack(sak), jnp.stack(sav), jnp.stack(sbk), jnp.stack(sbv), jnp.stack(scs))
```

```python
import functools
import math

import jax
import jax.numpy as jnp
from jax import lax
from jax.experimental import pallas as pl
from jax.experimental.pallas import tpu as pltpu

F32 = jnp.float32
BF16 = jnp.bfloat16

EPS = 1e-6
NEG = -1e30
LOG2E = 1.4426950408889634
ROPE_THETA = 10000.0
CHUNK = 64
HEAD_DIM = 64
N_HEADS_A = 8
N_HEADS_B = 4
BAND_PAST = 512
MAX_REL = 128
BAND_KEYS = BAND_PAST + CHUNK
BAND_Q = 256
BAND_WIN = BAND_Q + BAND_PAST
BIAS_LANES = 640
WIDTH = N_HEADS_A * HEAD_DIM
LANES = 128
ROW_TILE = 512
FF_CHUNK = 256
VMEM_LIMIT = 56 * 1024 * 1024

_NT = (((1,), (1,)), ((), ()))
_TN = (((0,), (0,)), ((), ()))


def _resident(shape):
    return pl.BlockSpec(shape, lambda *_: (0,) * len(shape), pipeline_mode=pl.Buffered(1))


def _rms(x, g):
    return x * lax.rsqrt(jnp.mean(x * x, axis=-1, keepdims=True) + EPS) * g


def _params(*sem):
    return pltpu.CompilerParams(dimension_semantics=sem, vmem_limit_bytes=VMEM_LIMIT)


def _split_maps(q):
    low = lax.broadcasted_iota(jnp.int32, q.shape, 1) < HEAD_DIM
    zero = jnp.zeros_like(q)
    return jnp.concatenate([jnp.where(low, q, zero), jnp.where(low, zero, q)], axis=0)


N_STACKED = 4


def _attn_in_kernel(*refs):
    x_ref, g_ref, w_ref, cos_ref, sin_ref = refs[:5]
    (qa_ref, ka_ref, va_ref, qb_ref, kb_ref, vb_ref,
     kaf_ref, vaf_ref, kbf_ref, vbf_ref) = refs[-10:]
    h = _rms(x_ref[...], g_ref[...]).astype(BF16)
    q_scale = LOG2E / math.sqrt(HEAD_DIM)

    def proj(c):
        return jnp.dot(h, w_ref[:, c * WIDTH:(c + 1) * WIDTH], preferred_element_type=F32)

    cos = cos_ref[...]
    sin = sin_ref[...]
    low_half = (lax.broadcasted_iota(jnp.int32, cos.shape, 1) % HEAD_DIM) < (HEAD_DIM // 2)

    def rotary(z):
        parts = []
        for j in range(WIDTH // LANES):
            zj = z[:, j * LANES:(j + 1) * LANES]
            swapped = jnp.where(low_half, pltpu.roll(zj, LANES - HEAD_DIM // 2, 1),
                                pltpu.roll(zj, HEAD_DIM // 2, 1))
            parts.append(zj * cos + swapped * sin)
        return jnp.concatenate(parts, axis=1)

    qa_ref[...] = (proj(0) * q_scale).astype(BF16)
    ka = proj(1)
    ka_ref[...] = ka.astype(BF16)
    kaf_ref[...] = ka
    va = proj(2)
    va_ref[...] = va.astype(BF16)
    vaf_ref[...] = va
    qb_ref[...] = (rotary(proj(3)) * q_scale).astype(BF16)
    kb = rotary(proj(4))
    kb_ref[...] = kb.astype(BF16)
    kbf_ref[...] = kb
    vb = proj(5)
    vb_ref[...] = vb.astype(BF16)
    vbf_ref[...] = vb


def _attn_in_proj(x, g, w, cos_t, sin_t, stacked, *, seg_len, keep, layer, n_layers):
    m, d = x.shape
    tm = min(ROW_TILE, m)
    n_tiles = m // tm
    n_tab = cos_t.shape[0] // tm
    n_seg = m // seg_len
    keep_tiles = n_seg * keep // tm
    if seg_len >= tm:
        tps, kpt = seg_len // tm, keep // tm

        def keep_idx(i):
            return (layer * keep_tiles + (i // tps) * kpt + jnp.clip(i % tps - (tps - kpt), 0, kpt - 1), 0)
    else:
        assert keep == seg_len

        def keep_idx(i):
            return (layer * keep_tiles + i, 0)

    row = lambda i: (i, 0)
    tab = lambda i: (i % n_tab, 0)
    bf_out = jax.ShapeDtypeStruct((m, WIDTH), BF16)
    f_out = jax.ShapeDtypeStruct((n_layers * m, WIDTH), F32)
    keep_out = jax.ShapeDtypeStruct((n_layers * n_seg * keep, WIDTH), F32)
    blk = pl.BlockSpec((tm, WIDTH), row)
    f_blk = pl.BlockSpec((tm, WIDTH), lambda i: (layer * n_tiles + i, 0))
    carried = [] if stacked is None else list(stacked)
    n_in = 5
    return pl.pallas_call(
        _attn_in_kernel,
        grid=(n_tiles,),
        in_specs=[pl.BlockSpec((tm, d), row), _resident((1, d)), _resident(w.shape),
                  pl.BlockSpec((tm, LANES), tab), pl.BlockSpec((tm, LANES), tab)]
                 + [pl.BlockSpec(memory_space=pl.ANY)] * len(carried),
        out_specs=[blk] * 6 + [pl.BlockSpec((tm, WIDTH), keep_idx)] * 2 + [f_blk] * 2,
        out_shape=[bf_out] * 6 + [keep_out] * 2 + [f_out] * 2,
        input_output_aliases={n_in + k: 6 + k for k in range(len(carried))},
        compiler_params=_params("arbitrary"),
        name="attn_in_proj",
    )(x, g, w, cos_t, sin_t, *carried)


def _rope_tables(pos):
    half = HEAD_DIM // 2
    inv = ROPE_THETA ** (-jnp.arange(half, dtype=F32) / half)
    ang = pos.astype(F32)[:, None] * inv[None, :]
    c, s = jnp.cos(ang), jnp.sin(ang)
    reps = LANES // HEAD_DIM
    return jnp.tile(c, (1, 2 * reps)), jnp.tile(jnp.concatenate([-s, s], axis=1), (1, reps))


def _band_bias_kernel(tab_ref, o_ref):
    r = pl.program_id(0)
    shape = (BAND_WIN, 2 * BAND_Q)
    kj = lax.broadcasted_iota(jnp.int32, shape, 0)
    col = lax.broadcasted_iota(jnp.int32, shape, 1)
    qi = col % BAND_Q
    idx = jnp.clip(BAND_PAST + qi - kj, -MAX_REL, MAX_REL) + MAX_REL
    first_key = (qi // CHUNK) * CHUNK
    in_band = jnp.logical_and(kj >= first_key, kj < first_key + BAND_KEYS)
    second = lax.broadcasted_iota(jnp.int32, (1, 2 * BAND_Q), 1) >= BAND_Q

    def entry(d):
        return jnp.where(second, tab_ref[2 * r + 1, d], tab_ref[2 * r, d])

    def body(d, acc):
        return jnp.where(idx == d, entry(d), acc)

    acc = lax.fori_loop(MAX_REL - CHUNK + 1, 2 * MAX_REL, body,
                        jnp.broadcast_to(entry(2 * MAX_REL), shape))
    o_ref[0] = jnp.where(in_band, acc * LOG2E, NEG)


def _band_bias(table):
    n = table.shape[0] // 2
    return pl.pallas_call(
        _band_bias_kernel,
        grid=(n,),
        in_specs=[pl.BlockSpec(memory_space=pltpu.SMEM)],
        out_specs=pl.BlockSpec((1, BAND_WIN, 2 * BAND_Q), lambda r: (r, 0, 0)),
        out_shape=jax.ShapeDtypeStruct((n, BAND_WIN, 2 * BAND_Q), F32),
        compiler_params=_params("arbitrary"),
        name="band_bias",
    )(table)


def _sample_bias_kernel(tab_ref, o_ref, *, t):
    r = pl.program_id(0)
    shape = (N_HEADS_A * t, BIAS_LANES)
    row = lax.broadcasted_iota(jnp.int32, shape, 0)
    kj = lax.broadcasted_iota(jnp.int32, shape, 1)
    idx = jnp.clip(BAND_PAST + row % t - kj, -MAX_REL, MAX_REL) + MAX_REL
    head = lax.broadcasted_iota(jnp.int32, (N_HEADS_A * t, 1), 0) // t

    def body(d, acc):
        col = jnp.zeros(head.shape, F32)
        for h in range(N_HEADS_A):
            col = jnp.where(head == h, tab_ref[N_HEADS_A * r + h, d], col)
        return jnp.where(idx == d, col, acc)

    o_ref[0] = lax.fori_loop(0, 2 * MAX_REL + 1, body, jnp.zeros(shape, F32)) * LOG2E


def _sample_bias(table, t):
    n = table.shape[0] // N_HEADS_A
    return pl.pallas_call(
        functools.partial(_sample_bias_kernel, t=t),
        grid=(n,),
        in_specs=[pl.BlockSpec(memory_space=pltpu.SMEM)],
        out_specs=pl.BlockSpec((1, N_HEADS_A * t, BIAS_LANES), lambda r: (r, 0, 0)),
        out_shape=jax.ShapeDtypeStruct((n, N_HEADS_A * t, BIAS_LANES), F32),
        compiler_params=_params("arbitrary"),
        name="sample_bias",
    )(table)


def _band_kernel(q_ref, kp_ref, kc_ref, vp_ref, vc_ref, bias_ref, o_ref):
    qblk = pl.program_id(2)

    def window(prev_ref, cur_ref, sub, first):
        lo, hi = sub * BAND_Q, (sub + 1) * BAND_Q
        if first:
            return cur_ref[0, :hi]
        return jnp.concatenate([prev_ref[0, lo:], cur_ref[0, :hi]], axis=0)

    def scores(sub, first):
        keys = window(kp_ref, kc_ref, sub, first)
        q = q_ref[0, sub * BAND_Q:(sub + 1) * BAND_Q, :]
        bias = bias_ref[0, BAND_WIN - keys.shape[0]:, :]
        return lax.dot_general(keys, _split_maps(q), _NT, preferred_element_type=F32) + bias

    def finish(s, sub, first):
        p = jnp.exp2(s - jnp.max(s, axis=0, keepdims=True))
        l = jnp.sum(p, axis=0, keepdims=True)
        vals = window(vp_ref, vc_ref, sub, first)
        o_t = lax.dot_general(vals, p.astype(BF16), _TN, preferred_element_type=F32) / l
        o = jnp.concatenate([o_t[:HEAD_DIM, :BAND_Q], o_t[HEAD_DIM:, BAND_Q:]], axis=0).T
        o_ref[0, sub * BAND_Q:(sub + 1) * BAND_Q, :] = o.astype(BF16)

    def run(first):
        subs = range(BAND_PAST // BAND_Q)
        all_scores = [scores(sub, first) for sub in subs]
        for sub in subs:
            finish(all_scores[sub], sub, first)

    pl.when(qblk == 0)(functools.partial(run, True))
    pl.when(qblk > 0)(functools.partial(run, False))


def _band_attn(q, k, v, bias, layer):
    b, s, _ = q.shape
    bq = BAND_PAST
    n_pairs = N_HEADS_A // 2
    cur = lambda hp, bi, qi: (bi, qi, hp)
    prev = lambda hp, bi, qi: (bi, jnp.maximum(qi - 1, 0), hp)
    blk = (1, bq, LANES)
    return pl.pallas_call(
        _band_kernel,
        grid=(n_pairs, b, s // bq),
        in_specs=[pl.BlockSpec(blk, cur), pl.BlockSpec(blk, prev), pl.BlockSpec(blk, cur),
                  pl.BlockSpec(blk, prev), pl.BlockSpec(blk, cur),
                  pl.BlockSpec((1, BAND_WIN, 2 * BAND_Q), lambda hp, bi, qi: (layer * n_pairs + hp, 0, 0))],
        out_specs=pl.BlockSpec(blk, cur),
        out_shape=jax.ShapeDtypeStruct((b, s, WIDTH), BF16),
        compiler_params=_params("arbitrary", "arbitrary", "arbitrary"),
        name="band_attn",
    )(q, k, k, v, v, bias)


def _lambda(lam_ref, lam_init):
    lp = lam_ref[...]
    a1 = jnp.sum(lp[0:1] * lp[1:2], axis=-1, keepdims=True)
    a2 = jnp.sum(lp[2:3] * lp[3:4], axis=-1, keepdims=True)
    return jnp.exp(a1) - jnp.exp(a2) + lam_init


def _diff_kernel(lam_ref, g_ref, q_ref, k_ref, v_ref, o_ref, m_ref, l_ref, acc_ref,
                 s0_ref, s1_ref, mx0_ref, mx1_ref, *, lam_init):
    qi = pl.program_id(2)
    q = q_ref[0]
    bq = q.shape[0]
    q2 = _split_maps(q)
    bufs = ((s0_ref, mx0_ref), (s1_ref, mx1_ref))

    m_ref[...] = jnp.full(m_ref.shape, NEG, F32)
    l_ref[...] = jnp.zeros(l_ref.shape, F32)
    acc_ref[...] = jnp.zeros(acc_ref.shape, F32)

    def block(ref, kj):
        return ref[0, pl.ds(pl.multiple_of(kj * bq, bq), bq), :]

    def scores(kj, buf, diagonal=False):
        s_ref, mx_ref = bufs[buf]
        s = lax.dot_general(block(k_ref, kj), q2, _NT, preferred_element_type=F32)
        if diagonal:
            k_chunk = lax.broadcasted_iota(jnp.int32, s.shape, 0) // CHUNK
            q_chunk = (lax.broadcasted_iota(jnp.int32, s.shape, 1) % bq) // CHUNK
            s = jnp.where(k_chunk <= q_chunk, s, NEG)
        s_ref[...] = s
        mx_ref[...] = jnp.max(s, axis=0, keepdims=True)

    def absorb(kj, buf):
        s_ref, mx_ref = bufs[buf]
        m_old = m_ref[...]
        m_new = jnp.maximum(m_old, mx_ref[...])
        alpha = jnp.exp2(m_old - m_new)
        p = jnp.exp2(s_ref[...] - m_new)
        l_ref[...] = alpha * l_ref[...] + jnp.sum(p, axis=0, keepdims=True)
        pv = lax.dot_general(block(v_ref, kj), p.astype(BF16), _TN, preferred_element_type=F32)
        acc_ref[...] = alpha * acc_ref[...] + pv
        m_ref[...] = m_new

    @pl.when(qi == 0)
    def _():
        scores(0, 0, diagonal=True)
        absorb(0, 0)

    @pl.when(qi > 0)
    def _():
        scores(0, 0)

    def two_blocks(pair, carry):
        scores(2 * pair + 1, 1)
        absorb(2 * pair, 0)
        scores(2 * pair + 2, 0)
        absorb(2 * pair + 1, 1)
        return carry

    n_pairs = jnp.maximum(qi - 1, 0) // 2
    lax.fori_loop(0, n_pairs, two_blocks, 0)

    @pl.when(qi % 2 == 1)
    def _():
        scores(qi, 1, diagonal=True)
        absorb(qi - 1, 0)
        absorb(qi, 1)

    @pl.when(jnp.logical_and(qi % 2 == 0, qi > 0))
    def _():
        scores(qi - 1, 1)
        absorb(qi - 2, 0)
        scores(qi, 0, diagonal=True)
        absorb(qi - 1, 1)
        absorb(qi, 0)

    o_t = acc_ref[...] / l_ref[...]
    lam = _lambda(lam_ref, lam_init)
    d = (o_t[:, :bq] - lam * o_t[:, bq:]).T
    o_ref[0] = (_rms(d, g_ref[...]) * (1.0 - lam_init)).astype(BF16)


def _diff_attn(q, k, v, lam_p, g, lam_init):
    b, s, _ = q.shape
    bq = min(ROW_TILE, s)
    seq = pl.BlockSpec((1, s, LANES), lambda bi, h, qi: (bi, 0, h))
    blk = pl.BlockSpec((1, bq, LANES), lambda bi, h, qi: (bi, qi, h))
    return pl.pallas_call(
        functools.partial(_diff_kernel, lam_init=lam_init),
        grid=(b, N_HEADS_B, s // bq),
        in_specs=[_resident(lam_p.shape), _resident(g.shape), blk, seq, seq],
        out_specs=blk,
        out_shape=jax.ShapeDtypeStruct((b, s, WIDTH), BF16),
        scratch_shapes=[pltpu.VMEM((1, 2 * bq), F32), pltpu.VMEM((1, 2 * bq), F32),
                        pltpu.VMEM((LANES, 2 * bq), F32),
                        pltpu.VMEM((bq, 2 * bq), F32), pltpu.VMEM((bq, 2 * bq), F32),
                        pltpu.VMEM((1, 2 * bq), F32), pltpu.VMEM((1, 2 * bq), F32)],
        compiler_params=_params("arbitrary", "arbitrary", "arbitrary"),
        name="diff_attn",
    )(lam_p, g, q, k, v)


def _sample_kernel(lam_ref, g_ref, qa_ref, kan_ref, van_ref, qb_ref, kbn_ref, vbn_ref,
                   cak_ref, cav_ref, cbk_ref, cbv_ref, bias_c_ref, bias_n_ref,
                   oa_ref, ob_ref, *, lam_init):
    t = qa_ref.shape[0]
    n_grp = WIDTH // HEAD_DIM
    rows = n_grp * t
    grp_of_row = lax.broadcasted_iota(jnp.int32, (rows, WIDTH), 0) // t
    grp_of_lane = lax.broadcasted_iota(jnp.int32, (rows, WIDTH), 1) // HEAD_DIM
    own = grp_of_row == grp_of_lane
    lane_t = lax.broadcasted_iota(jnp.int32, (t, WIDTH), 1)

    def expand(q):
        qe = jnp.concatenate([q] * n_grp, axis=0)
        return jnp.where(own, qe, jnp.zeros_like(qe))

    def attend(qe, k_cache, k_new, v_cache, v_new, bias_c, bias_n):
        s_c = lax.dot_general(qe, k_cache, _NT, preferred_element_type=F32)
        s_n = lax.dot_general(qe, k_new, _NT, preferred_element_type=F32)
        if bias_c is not None:
            s_c = s_c + bias_c
            s_n = s_n + bias_n
        m = jnp.maximum(jnp.max(s_c, axis=-1, keepdims=True), jnp.max(s_n, axis=-1, keepdims=True))
        p_c = jnp.exp2(s_c - m)
        p_n = jnp.exp2(s_n - m)
        l = jnp.sum(p_c, axis=-1, keepdims=True) + jnp.sum(p_n, axis=-1, keepdims=True)
        o = (jnp.dot(p_c.astype(BF16), v_cache, preferred_element_type=F32)
             + jnp.dot(p_n.astype(BF16), v_new, preferred_element_type=F32))
        return o / l

    oa_all = attend(expand(qa_ref[...]), cak_ref[0].astype(BF16), kan_ref[...],
                    cav_ref[0].astype(BF16), van_ref[...], bias_c_ref[...], bias_n_ref[...])
    oa = jnp.zeros((t, WIDTH), F32)
    for h in range(N_HEADS_A):
        oa = jnp.where(lane_t // HEAD_DIM == h, oa_all[h * t:(h + 1) * t], oa)
    oa_ref[...] = oa.astype(BF16)

    ob_all = attend(expand(qb_ref[...]), cbk_ref[0].astype(BF16), kbn_ref[...],
                    cbv_ref[0].astype(BF16), vbn_ref[...], None, None)
    lam = _lambda(lam_ref, lam_init)
    g = g_ref[...]
    outs = []
    for h in range(N_HEADS_B):
        d = ob_all[2 * h * t:(2 * h + 1) * t] - lam * ob_all[(2 * h + 1) * t:(2 * h + 2) * t]
        outs.append(_rms(d[:, h * LANES:(h + 1) * LANES], g) * (1.0 - lam_init))
    ob_ref[...] = jnp.concatenate(outs, axis=1).astype(BF16)


def _sample_attn(qa, ka, va, qb, kb, vb, cak, cav, cbk, cbv, bias_c, bias_n, lam_p, g, lam_init, t):
    m = qa.shape[0]
    bs = m // t
    new = pl.BlockSpec((t, WIDTH), lambda i: (i, 0))
    cache = lambda c: pl.BlockSpec((1,) + c.shape[1:], lambda i: (i, 0, 0))
    out = jax.ShapeDtypeStruct((m, WIDTH), BF16)
    return pl.pallas_call(
        functools.partial(_sample_kernel, lam_init=lam_init),
        grid=(bs,),
        in_specs=[_resident(lam_p.shape), _resident(g.shape)] + [new] * 6
                 + [cache(cak), cache(cav), cache(cbk), cache(cbv),
                    _resident(bias_c.shape), _resident(bias_n.shape)],
        out_specs=[new, new],
        out_shape=[out, out],
        compiler_params=_params("arbitrary"),
        name="sample_attn",
    )(lam_p, g, qa, ka, va, qb, kb, vb, cak, cav, cbk, cbv, bias_c, bias_n)


def _conv_kernel(x_ref, g_ref, w_ref, cw_ref, st_ref, y_ref, tail_ref, carry_ref, *, tps, seg_len):
    tm, d = x_ref.shape
    h = _rms(x_ref[...], g_ref[...]).astype(BF16)
    carried = seg_len >= tm
    if carried:
        @pl.when(pl.program_id(0) % tps == 0)
        def _():
            carry_ref[...] = st_ref[0]

    cw = cw_ref[...]
    cc = 2 * FF_CHUNK
    for c in range(d // cc):
        cols = slice(c * cc, (c + 1) * cc)
        proj = lambda part: jnp.dot(h, w_ref[:, part * d + c * cc:part * d + (c + 1) * cc],
                                    preferred_element_type=F32)
        xin = proj(1) * proj(2)
        r = lax.broadcasted_iota(jnp.int32, xin.shape, 0)
        x1 = pltpu.roll(xin, 1, 0)
        x2 = pltpu.roll(xin, 2, 0)
        if carried:
            prev = carry_ref[:, cols]
            x1 = jnp.where(r == 0, prev[7:8], x1)
            x2 = jnp.where(r == 0, prev[6:7], jnp.where(r == 1, prev[7:8], x2))
            carry_ref[:, cols] = xin[tm - 8:]
            tail_ref[0, :, cols] = xin[tm - 8:]
        else:
            t = r % seg_len
            x1 = jnp.where(t >= 1, x1, st_ref[0, :, cols])
            x2 = jnp.where(t >= 2, x2, st_ref[1, :, cols])
            tail_ref[:, cols] = xin
        conv = cw[0:1, cols] * x2 + cw[1:2, cols] * x1 + cw[2:3, cols] * xin
        y_ref[:, cols] = (proj(0) * conv).astype(BF16)


def _conv_in(x, g, w, cw, state, *, seg_len):
    m, d = x.shape
    tm = min(ROW_TILE, m)
    row = lambda i: (i, 0)
    if seg_len >= tm:
        tps = seg_len // tm
        st_spec = pl.BlockSpec((1, 8, d), lambda i: (i // tps, 0, 0))
        tail_spec = pl.BlockSpec((1, 8, d), lambda i: (i // tps, 0, 0))
        tail_shape = jax.ShapeDtypeStruct((m // seg_len, 8, d), F32)
    else:
        tps = 1
        st_spec = pl.BlockSpec((2, tm, d), lambda i: (0, i, 0))
        tail_spec = pl.BlockSpec((tm, d), row)
        tail_shape = jax.ShapeDtypeStruct((m, d), F32)
    return pl.pallas_call(
        functools.partial(_conv_kernel, tps=tps, seg_len=seg_len),
        grid=(m // tm,),
        in_specs=[pl.BlockSpec((tm, d), row), _resident((1, d)), _resident(w.shape),
                  _resident(cw.shape), st_spec],
        out_specs=[pl.BlockSpec((tm, d), row), tail_spec],
        out_shape=[jax.ShapeDtypeStruct((m, d), BF16), tail_shape],
        scratch_shapes=[pltpu.VMEM((8, d), F32)],
        compiler_params=_params("arbitrary"),
        name="conv_in",
    )(x, g, w, cw, state)


def _ffn_kernel(*refs, n_parts, final):
    x_ref = refs[0]
    a_refs = refs[1:1 + n_parts]
    wo_ref, g_ref, wg_ref, wu_ref, wd_ref = refs[1 + n_parts:6 + n_parts]
    gf_ref = refs[6 + n_parts] if final else None
    o_ref, act_ref = refs[-2], refs[-1]

    a = a_refs[0][...] if n_parts == 1 else jnp.concatenate([r[...] for r in a_refs], axis=1)
    x1 = x_ref[...] + jnp.dot(a, wo_ref[...], preferred_element_type=F32)
    h = _rms(x1, g_ref[...]).astype(BF16)
    d_ff = wg_ref.shape[1]
    for c in range(d_ff // FF_CHUNK):
        cols = slice(c * FF_CHUNK, (c + 1) * FF_CHUNK)
        gate = jnp.dot(h, wg_ref[:, cols], preferred_element_type=F32)
        up = jnp.dot(h, wu_ref[:, cols], preferred_element_type=F32)
        act_ref[:, cols] = (gate * (1.0 / (1.0 + jnp.exp(-gate))) * up).astype(BF16)
    y = x1 + jnp.dot(act_ref[...], wd_ref[...], preferred_element_type=F32)
    o_ref[...] = _rms(y, gf_ref[...]) if final else y


def _mix_out_ffn(x, parts, wo, g, wg, wu, wd, g_final=None):
    m, d = x.shape
    tm = min(ROW_TILE, m)
    row = lambda i: (i, 0)
    final = g_final is not None
    extra = [g_final] if final else []
    return pl.pallas_call(
        functools.partial(_ffn_kernel, n_parts=len(parts), final=final),
        grid=(m // tm,),
        in_specs=[pl.BlockSpec((tm, d), row)] + [pl.BlockSpec((tm, p.shape[1]), row) for p in parts]
                 + [_resident(wo.shape), _resident((1, d)), _resident(wg.shape), _resident(wu.shape),
                    _resident(wd.shape)] + [_resident((1, d))] * len(extra),
        out_specs=pl.BlockSpec((tm, d), row),
        out_shape=jax.ShapeDtypeStruct((m, d), F32),
        scratch_shapes=[pltpu.VMEM((tm, wg.shape[1]), BF16)],
        compiler_params=_params("arbitrary"),
        name="mix_out_ffn",
    )(x, *parts, wo, g, wg, wu, wd, *extra)


def kernel(x_prompt, x_sample, cache_a_k, cache_a_v, cache_b_k, cache_b_v, state_conv, norm_mix, norm_ffn, norm_final, w_attn_in, w_attn_out, rel_bias, lambda_q1, lambda_k1, lambda_q2, lambda_k2, subln_g, w_conv_in, conv_w, w_conv_out, w_ffn_gate, w_ffn_up, w_ffn_down):
    b, s, d = x_prompt.shape
    bs, t, _ = x_sample.shape
    depth = norm_mix.shape[0]
    n_attn = w_attn_in.shape[0]
    n_conv = w_conv_in.shape[0]
    past = cache_b_k.shape[2]
    a_past = cache_a_k.shape[2]
    keep = min(BAND_PAST, s)
    assert s % BAND_PAST == 0 and a_past == BAND_PAST and d % (2 * FF_CHUNK) == 0
    assert w_ffn_gate.shape[2] % FF_CHUNK == 0 and 2 <= t and a_past + t <= BIAS_LANES

    xp = x_prompt.reshape(b * s, d)
    xs = x_sample.reshape(bs * t, d)
    row = lambda v: v.reshape(1, -1)

    cos_p, sin_p = _rope_tables(jnp.arange(s))
    cos_s, sin_s = (jnp.tile(tab, (bs, 1)) for tab in _rope_tables(past + jnp.arange(t)))

    table = rel_bias.reshape(n_attn * N_HEADS_A, 2 * MAX_REL + 1)
    band_bias = _band_bias(table)
    sample_bias = _sample_bias(table, t)

    stacked_p = stacked_s = None
    pcs, scs = [], []
    for i in range(depth):
        j = i // 2
        g_mix = row(norm_mix[i])
        if i % 2 == 0:
            lam_init = 0.8 - 0.6 * math.exp(-0.3 * i)
            w_in = w_attn_in[j].astype(BF16)
            wo = w_attn_out[j].astype(BF16)
            lam_p = jnp.stack([lambda_q1[j], lambda_k1[j], lambda_q2[j], lambda_k2[j]])
            g_sub = row(subln_g[j])

            outs = _attn_in_proj(xp, g_mix, w_in, cos_p, sin_p, stacked_p,
                                 seg_len=s, keep=keep, layer=j, n_layers=n_attn)
            qa, ka, va, qb, kb, vb = (o.reshape(b, s, WIDTH) for o in outs[:6])
            stacked_p = outs[6:]
            oa = _band_attn(qa, ka, va, band_bias, j)
            ob = _diff_attn(qb, kb, vb, lam_p, g_sub, lam_init)
            parts_p = [oa.reshape(b * s, WIDTH), ob.reshape(b * s, WIDTH)]

            outs = _attn_in_proj(xs, g_mix, w_in, cos_s, sin_s, stacked_s,
                                 seg_len=t, keep=t, layer=j, n_layers=n_attn)
            stacked_s = outs[6:]
            oa, ob = _sample_attn(
                *outs[:6],
                cache_a_k[j].reshape(bs, a_past, WIDTH), cache_a_v[j].reshape(bs, a_past, WIDTH),
                cache_b_k[j].reshape(bs, past, WIDTH), cache_b_v[j].reshape(bs, past, WIDTH),
                sample_bias[j, :, :a_past], sample_bias[j, :, a_past:a_past + t],
                lam_p, g_sub, lam_init, t)
            parts_s = [oa, ob]
        else:
            w_in = w_conv_in[j].astype(BF16)
            wo = w_conv_out[j].astype(BF16)
            yp, tail = _conv_in(xp, g_mix, w_in, conv_w[j], jnp.zeros((b, 8, d), F32), seg_len=s)
            pcs.append(tail[:, 6:])
            st = state_conv[j]
            stand_in = jnp.stack([jnp.pad(st[:, 1:2], ((0, 0), (0, t - 1), (0, 0))),
                                  jnp.pad(st, ((0, 0), (0, t - 2), (0, 0)))]).reshape(2, bs * t, d)
            ys, xin = _conv_in(xs, g_mix, w_in, conv_w[j], stand_in, seg_len=t)
            scs.append(xin.reshape(bs, t, d)[:, t - 2:])
            parts_p, parts_s = [yp], [ys]

        ffn = (wo, row(norm_ffn[i]), w_ffn_gate[i].astype(BF16), w_ffn_up[i].astype(BF16),
               w_ffn_down[i].astype(BF16))
        g_final = row(norm_final) if i == depth - 1 else None
        xp = _mix_out_ffn(xp, parts_p, *ffn, g_final=g_final)
        xs = _mix_out_ffn(xs, parts_s, *ffn, g_final=g_final)

    pak, pav, pbk, pbv = stacked_p
    sak, sav, sbk, sbv = stacked_s
    return (xp.reshape(b, s, d), xs.reshape(bs, t, d),
            pak.reshape(n_attn, b, keep, N_HEADS_A, HEAD_DIM), pav.reshape(n_attn, b, keep, N_HEADS_A, HEAD_DIM),
            pbk.reshape(n_attn, b, s, N_HEADS_B, 2, HEAD_DIM), pbv.reshape(n_attn, b, s, N_HEADS_B, 2 * HEAD_DIM),
            jnp.stack(pcs),
            sak.reshape(n_attn, bs, t, N_HEADS_A, HEAD_DIM), sav.reshape(n_attn, bs, t, N_HEADS_A, HEAD_DIM),
            sbk.reshape(n_attn, bs, t, N_HEADS_B, 2, HEAD_DIM), sbv.reshape(n_attn, bs, t, N_HEADS_B, 2 * HEAD_DIM),
            jnp.stack(scs))
```

```python
import functools
import math

import jax
import jax.numpy as jnp
from jax import lax
from jax.experimental import pallas as pl
from jax.experimental.pallas import tpu as pltpu

F32 = jnp.float32
BF16 = jnp.bfloat16

EPS = 1e-6
NEG = -1e30
LOG2E = 1.4426950408889634
ROPE_THETA = 10000.0
CHUNK = 64
HEAD_DIM = 64
N_HEADS_A = 8
N_HEADS_B = 4
BAND_PAST = 512
MAX_REL = 128
BAND_KEYS = BAND_PAST + CHUNK
BAND_Q = 256
BAND_WIN = BAND_Q + BAND_PAST
BIAS_LANES = 640
WIDTH = N_HEADS_A * HEAD_DIM
LANES = 128
ROW_TILE = 512
FF_CHUNK = 256
VMEM_LIMIT = 56 * 1024 * 1024

_NT = (((1,), (1,)), ((), ()))
_TN = (((0,), (0,)), ((), ()))
_NN = (((1,), (0,)), ((), ()))


def _resident(shape):
    return pl.BlockSpec(shape, lambda *_: (0,) * len(shape), pipeline_mode=pl.Buffered(1))


def _rms(x, g):
    return x * lax.rsqrt(jnp.mean(x * x, axis=-1, keepdims=True) + EPS) * g


def _params(*sem):
    return pltpu.CompilerParams(dimension_semantics=sem, vmem_limit_bytes=VMEM_LIMIT)


def _split_maps(q):
    low = lax.broadcasted_iota(jnp.int32, q.shape, 1) < HEAD_DIM
    zero = jnp.zeros_like(q)
    return jnp.concatenate([jnp.where(low, q, zero), jnp.where(low, zero, q)], axis=0)


def _attn_in_kernel(*refs, native):
    x_ref, g_ref, w_ref, cos_ref, sin_ref = refs[:5]
    (qa_ref, ka_ref, va_ref, qb_ref, kb_ref, vb_ref,
     kaf_ref, vaf_ref, kbf_ref, vbf_ref) = refs[-10:]
    h = _rms(x_ref[...], g_ref[...]).astype(BF16)
    q_scale = LOG2E / math.sqrt(HEAD_DIM)

    def proj(c):
        return jnp.dot(h, w_ref[:, c * WIDTH:(c + 1) * WIDTH], preferred_element_type=F32)

    cos = cos_ref[...]
    sin = sin_ref[...]
    low_half = (lax.broadcasted_iota(jnp.int32, cos.shape, 1) % HEAD_DIM) < (HEAD_DIM // 2)

    def rotary(z):
        parts = []
        for j in range(WIDTH // LANES):
            zj = z[:, j * LANES:(j + 1) * LANES]
            swapped = jnp.where(low_half, pltpu.roll(zj, LANES - HEAD_DIM // 2, 1),
                                pltpu.roll(zj, HEAD_DIM // 2, 1))
            parts.append(zj * cos + swapped * sin)
        return jnp.concatenate(parts, axis=1)

    def put(ref, z):
        if native:
            ref[0, 0] = z.T
        else:
            ref[...] = z

    qa_ref[...] = (proj(0) * q_scale).astype(BF16)
    ka = proj(1)
    ka_ref[...] = ka.astype(BF16)
    put(kaf_ref, ka)
    va = proj(2)
    va_ref[...] = va.astype(BF16)
    put(vaf_ref, va)
    qb_ref[...] = (rotary(proj(3)) * q_scale).astype(BF16)
    kb = rotary(proj(4))
    kb_ref[...] = kb.astype(BF16)
    put(kbf_ref, kb)
    vb = proj(5)
    vb_ref[...] = vb.astype(BF16)
    if native:
        tm = vb.shape[0]
        for h in range(N_HEADS_B):
            vbf_ref[pl.ds(h, tm, stride=N_HEADS_B), :] = vb[:, h * LANES:(h + 1) * LANES]
    else:
        vbf_ref[...] = vb


def _attn_in_proj(x, g, w, cos_t, sin_t, stacked, *, seg_len, keep, layer, n_layers, native):
    m, d = x.shape
    tm = min(ROW_TILE, m)
    n_tiles = m // tm
    n_tab = cos_t.shape[0] // tm
    n_seg = m // seg_len
    row = lambda i: (i, 0)
    tab = lambda i: (i % n_tab, 0)
    blk = pl.BlockSpec((tm, WIDTH), row)
    bf_out = jax.ShapeDtypeStruct((m, WIDTH), BF16)
    if native:
        tps, kpt = seg_len // tm, keep // tm
        t_blk = (1, 1, WIDTH, tm)
        keep_spec = pl.BlockSpec(
            t_blk, lambda i: (layer, i // tps, 0, jnp.clip(i % tps - (tps - kpt), 0, kpt - 1)))
        keep_out = jax.ShapeDtypeStruct((n_layers, n_seg, WIDTH, keep), F32)
        kb_spec = pl.BlockSpec(t_blk, lambda i: (layer, i // tps, 0, i % tps))
        kb_out = jax.ShapeDtypeStruct((n_layers, n_seg, WIDTH, seg_len), F32)
        vb_spec = pl.BlockSpec((tm * N_HEADS_B, LANES), lambda i: (layer * n_tiles + i, 0))
        vb_out = jax.ShapeDtypeStruct((n_layers * m * N_HEADS_B, LANES), F32)
    else:
        assert keep == seg_len
        keep_spec = kb_spec = vb_spec = pl.BlockSpec((tm, WIDTH), lambda i: (layer * n_tiles + i, 0))
        keep_out = kb_out = vb_out = jax.ShapeDtypeStruct((n_layers * m, WIDTH), F32)
    carried = [] if stacked is None else list(stacked)
    n_in = 5
    return pl.pallas_call(
        functools.partial(_attn_in_kernel, native=native),
        grid=(n_tiles,),
        in_specs=[pl.BlockSpec((tm, d), row), _resident((1, d)), _resident(w.shape),
                  pl.BlockSpec((tm, LANES), tab), pl.BlockSpec((tm, LANES), tab)]
                 + [pl.BlockSpec(memory_space=pl.ANY)] * len(carried),
        out_specs=[blk] * 6 + [keep_spec, keep_spec, kb_spec, vb_spec],
        out_shape=[bf_out] * 6 + [keep_out, keep_out, kb_out, vb_out],
        input_output_aliases={n_in + k: 6 + k for k in range(len(carried))},
        compiler_params=_params("arbitrary"),
        name="attn_in_proj",
    )(x, g, w, cos_t, sin_t, *carried)


def _rope_tables(pos):
    half = HEAD_DIM // 2
    inv = ROPE_THETA ** (-jnp.arange(half, dtype=F32) / half)
    ang = pos.astype(F32)[:, None] * inv[None, :]
    c, s = jnp.cos(ang), jnp.sin(ang)
    reps = LANES // HEAD_DIM
    return jnp.tile(c, (1, 2 * reps)), jnp.tile(jnp.concatenate([-s, s], axis=1), (1, reps))


def _band_bias_kernel(tab_ref, o_ref):
    r = pl.program_id(0)
    e_shape = (BAND_WIN + LANES, LANES)
    y = lax.broadcasted_iota(jnp.int32, e_shape, 0) - LANES
    lane = lax.broadcasted_iota(jnp.int32, e_shape, 1)
    idx = jnp.clip(BAND_PAST + lane - y, -MAX_REL, MAX_REL) + MAX_REL
    cols = []
    for hh in range(2):
        def body(d, acc, hh=hh):
            return jnp.where(idx == d, tab_ref[2 * r + hh, d], acc)

        e = lax.fori_loop(0, 2 * MAX_REL + 1, body, jnp.zeros(e_shape, F32))
        for c in range(BAND_Q // LANES):
            cols.append(e[LANES - LANES * c:LANES - LANES * c + BAND_WIN])
    bias = jnp.concatenate(cols, axis=1)

    kj = lax.broadcasted_iota(jnp.int32, bias.shape, 0)
    qi = lax.broadcasted_iota(jnp.int32, bias.shape, 1) % BAND_Q
    first_key = (qi // CHUNK) * CHUNK
    in_band = jnp.logical_and(kj >= first_key, kj < first_key + BAND_KEYS)
    o_ref[0] = jnp.where(in_band, bias * LOG2E, NEG)


def _band_bias(table):
    n = table.shape[0] // 2
    return pl.pallas_call(
        _band_bias_kernel,
        grid=(n,),
        in_specs=[pl.BlockSpec(memory_space=pltpu.SMEM)],
        out_specs=pl.BlockSpec((1, BAND_WIN, 2 * BAND_Q), lambda r: (r, 0, 0)),
        out_shape=jax.ShapeDtypeStruct((n, BAND_WIN, 2 * BAND_Q), F32),
        compiler_params=_params("arbitrary"),
        name="band_bias",
    )(table)


def _sample_bias_kernel(tab_ref, o_ref, *, t):
    r = pl.program_id(0)
    shape = (N_HEADS_A * t, BIAS_LANES)
    row = lax.broadcasted_iota(jnp.int32, shape, 0)
    kj = lax.broadcasted_iota(jnp.int32, shape, 1)
    idx = jnp.clip(BAND_PAST + row % t - kj, -MAX_REL, MAX_REL) + MAX_REL
    head = lax.broadcasted_iota(jnp.int32, (N_HEADS_A * t, 1), 0) // t

    def body(d, acc):
        col = jnp.zeros(head.shape, F32)
        for h in range(N_HEADS_A):
            col = jnp.where(head == h, tab_ref[N_HEADS_A * r + h, d], col)
        return jnp.where(idx == d, col, acc)

    o_ref[0] = lax.fori_loop(0, 2 * MAX_REL + 1, body, jnp.zeros(shape, F32)) * LOG2E


def _sample_bias(table, t):
    n = table.shape[0] // N_HEADS_A
    return pl.pallas_call(
        functools.partial(_sample_bias_kernel, t=t),
        grid=(n,),
        in_specs=[pl.BlockSpec(memory_space=pltpu.SMEM)],
        out_specs=pl.BlockSpec((1, N_HEADS_A * t, BIAS_LANES), lambda r: (r, 0, 0)),
        out_shape=jax.ShapeDtypeStruct((n, N_HEADS_A * t, BIAS_LANES), F32),
        compiler_params=_params("arbitrary"),
        name="sample_bias",
    )(table)


def _band_kernel(q_ref, kp_ref, kc_ref, vp_ref, vc_ref, bias_ref, o_ref):
    qblk = pl.program_id(2)
    n_sub = q_ref.shape[1] // BAND_Q

    def window(prev_ref, cur_ref, sub, first):
        lo, hi = sub * BAND_Q, (sub + 1) * BAND_Q
        parts = []
        if lo < BAND_PAST and not first:
            parts.append(prev_ref[0, lo:])
        parts.append(cur_ref[0, max(lo - BAND_PAST, 0):hi])
        return parts[0] if len(parts) == 1 else jnp.concatenate(parts, axis=0)

    def scores(sub, first):
        keys = window(kp_ref, kc_ref, sub, first)
        q = q_ref[0, sub * BAND_Q:(sub + 1) * BAND_Q, :]
        bias = bias_ref[0, BAND_WIN - keys.shape[0]:, :]
        return lax.dot_general(keys, _split_maps(q), _NT, preferred_element_type=F32) + bias

    def finish(s, sub, first):
        p = jnp.exp2(s - jnp.max(s, axis=0, keepdims=True))
        l = jnp.sum(p, axis=0, keepdims=True)
        vals = window(vp_ref, vc_ref, sub, first)
        o_t = lax.dot_general(vals, p.astype(BF16), _TN, preferred_element_type=F32) / l
        o = jnp.concatenate([o_t[:HEAD_DIM, :BAND_Q], o_t[HEAD_DIM:, BAND_Q:]], axis=0).T
        o_ref[0, sub * BAND_Q:(sub + 1) * BAND_Q, :] = o.astype(BF16)

    def run(first):
        nxt = scores(0, first)
        for sub in range(n_sub):
            cur_scores = nxt
            if sub + 1 < n_sub:
                nxt = scores(sub + 1, first)
            finish(cur_scores, sub, first)

    pl.when(qblk == 0)(functools.partial(run, True))
    pl.when(qblk > 0)(functools.partial(run, False))


def _band_attn(q, k, v, bias, layer):
    b, s, _ = q.shape
    bq = 2 * BAND_PAST if s % (2 * BAND_PAST) == 0 else BAND_PAST
    n_pairs = N_HEADS_A // 2
    cur = lambda hp, bi, qi: (bi, qi, hp)
    prev = lambda hp, bi, qi: (bi, jnp.maximum(qi * (bq // BAND_PAST) - 1, 0), hp)
    blk = (1, bq, LANES)
    pblk = (1, BAND_PAST, LANES)
    return pl.pallas_call(
        _band_kernel,
        grid=(n_pairs, b, s // bq),
        in_specs=[pl.BlockSpec(blk, cur), pl.BlockSpec(pblk, prev), pl.BlockSpec(blk, cur),
                  pl.BlockSpec(pblk, prev), pl.BlockSpec(blk, cur),
                  pl.BlockSpec((1, BAND_WIN, 2 * BAND_Q), lambda hp, bi, qi: (layer * n_pairs + hp, 0, 0))],
        out_specs=pl.BlockSpec(blk, cur),
        out_shape=jax.ShapeDtypeStruct((b, s, WIDTH), BF16),
        compiler_params=_params("arbitrary", "arbitrary", "arbitrary"),
        name="band_attn",
    )(q, k, k, v, v, bias)


def _lambda(lam_ref, lam_init):
    lp = lam_ref[...]
    a1 = jnp.sum(lp[0:1] * lp[1:2], axis=-1, keepdims=True)
    a2 = jnp.sum(lp[2:3] * lp[3:4], axis=-1, keepdims=True)
    return jnp.exp(a1) - jnp.exp(a2) + lam_init


def _diff_kernel(lam_ref, g_ref, q_ref, k_ref, v_ref, o_ref, m_ref, l_ref, acc_ref,
                 s0_ref, s1_ref, mx0_ref, mx1_ref, *, lam_init):
    qi = pl.program_id(2)
    q = q_ref[0]
    bq = q.shape[0]
    q2 = _split_maps(q)
    bufs = ((s0_ref, mx0_ref), (s1_ref, mx1_ref))

    m_ref[...] = jnp.full(m_ref.shape, NEG, F32)
    l_ref[...] = jnp.zeros(l_ref.shape, F32)
    acc_ref[...] = jnp.zeros(acc_ref.shape, F32)

    def block(ref, kj):
        return ref[0, pl.ds(pl.multiple_of(kj * bq, bq), bq), :]

    def scores(kj, buf, diagonal=False):
        s_ref, mx_ref = bufs[buf]
        s = lax.dot_general(block(k_ref, kj), q2, _NT, preferred_element_type=F32)
        if diagonal:
            k_chunk = lax.broadcasted_iota(jnp.int32, s.shape, 0) // CHUNK
            q_chunk = (lax.broadcasted_iota(jnp.int32, s.shape, 1) % bq) // CHUNK
            s = jnp.where(k_chunk <= q_chunk, s, NEG)
        s_ref[...] = s
        mx_ref[...] = jnp.max(s, axis=0, keepdims=True)

    def absorb(kj, buf):
        s_ref, mx_ref = bufs[buf]
        m_old = m_ref[...]
        m_new = jnp.maximum(m_old, mx_ref[...])
        alpha = jnp.exp2(m_old - m_new)
        p = jnp.exp2(s_ref[...] - m_new)
        l_ref[...] = alpha * l_ref[...] + jnp.sum(p, axis=0, keepdims=True)
        pv = lax.dot_general(block(v_ref, kj), p.astype(BF16), _TN, preferred_element_type=F32)
        acc_ref[...] = alpha * acc_ref[...] + pv
        m_ref[...] = m_new

    @pl.when(qi == 0)
    def _():
        scores(0, 0, diagonal=True)
        absorb(0, 0)

    @pl.when(qi > 0)
    def _():
        scores(0, 0)

    def two_blocks(pair, carry):
        scores(2 * pair + 1, 1)
        absorb(2 * pair, 0)
        scores(2 * pair + 2, 0)
        absorb(2 * pair + 1, 1)
        return carry

    n_pairs = jnp.maximum(qi - 1, 0) // 2
    lax.fori_loop(0, n_pairs, two_blocks, 0)

    @pl.when(qi % 2 == 1)
    def _():
        scores(qi, 1, diagonal=True)
        absorb(qi - 1, 0)
        absorb(qi, 1)

    @pl.when(jnp.logical_and(qi % 2 == 0, qi > 0))
    def _():
        scores(qi - 1, 1)
        absorb(qi - 2, 0)
        scores(qi, 0, diagonal=True)
        absorb(qi - 1, 1)
        absorb(qi, 0)

    o_t = acc_ref[...] / l_ref[...]
    lam = _lambda(lam_ref, lam_init)
    d = (o_t[:, :bq] - lam * o_t[:, bq:]).T
    o_ref[0] = (_rms(d, g_ref[...]) * (1.0 - lam_init)).astype(BF16)


def _diff_attn(q, k, v, lam_p, g, lam_init):
    b, s, _ = q.shape
    bq = min(ROW_TILE, s)
    seq = pl.BlockSpec((1, s, LANES), lambda bi, h, qi: (bi, 0, h))
    blk = pl.BlockSpec((1, bq, LANES), lambda bi, h, qi: (bi, qi, h))
    return pl.pallas_call(
        functools.partial(_diff_kernel, lam_init=lam_init),
        grid=(b, N_HEADS_B, s // bq),
        in_specs=[_resident(lam_p.shape), _resident(g.shape), blk, seq, seq],
        out_specs=blk,
        out_shape=jax.ShapeDtypeStruct((b, s, WIDTH), BF16),
        scratch_shapes=[pltpu.VMEM((1, 2 * bq), F32), pltpu.VMEM((1, 2 * bq), F32),
                        pltpu.VMEM((LANES, 2 * bq), F32),
                        pltpu.VMEM((bq, 2 * bq), F32), pltpu.VMEM((bq, 2 * bq), F32),
                        pltpu.VMEM((1, 2 * bq), F32), pltpu.VMEM((1, 2 * bq), F32)],
        compiler_params=_params("arbitrary", "arbitrary", "arbitrary"),
        name="diff_attn",
    )(lam_p, g, q, k, v)


def _sample_kernel(lam_ref, g_ref, qa_ref, kan_ref, van_ref, qb_ref, kbn_ref, vbn_ref,
                   cak_ref, cav_ref, cbk_ref, cbv_ref, bias_c_ref, bias_n_ref,
                   oa_ref, ob_ref, *, lam_init):
    t = qa_ref.shape[0]
    n_grp = WIDTH // HEAD_DIM
    rows = n_grp * t
    grp_of_row = lax.broadcasted_iota(jnp.int32, (rows, WIDTH), 0) // t
    grp_of_lane = lax.broadcasted_iota(jnp.int32, (rows, WIDTH), 1) // HEAD_DIM
    own = grp_of_row == grp_of_lane
    lane_t = lax.broadcasted_iota(jnp.int32, (t, WIDTH), 1)

    def expand(q):
        qe = jnp.concatenate([q] * n_grp, axis=0)
        return jnp.where(own, qe, jnp.zeros_like(qe))

    def attend(qe, kt_cache, k_new, v_cache, v_new, bias_c, bias_n, v_dims):
        s_c = jnp.dot(qe, kt_cache, preferred_element_type=F32)
        s_n = lax.dot_general(qe, k_new, _NT, preferred_element_type=F32)
        if bias_c is not None:
            s_c = s_c + bias_c
            s_n = s_n + bias_n
        m = jnp.maximum(jnp.max(s_c, axis=-1, keepdims=True), jnp.max(s_n, axis=-1, keepdims=True))
        p_c = jnp.exp2(s_c - m)
        p_n = jnp.exp2(s_n - m)
        l = jnp.sum(p_c, axis=-1, keepdims=True) + jnp.sum(p_n, axis=-1, keepdims=True)
        o = (lax.dot_general(p_c.astype(BF16), v_cache, v_dims, preferred_element_type=F32)
             + jnp.dot(p_n.astype(BF16), v_new, preferred_element_type=F32))
        return o / l

    oa_all = attend(expand(qa_ref[...]), cak_ref[0, 0].astype(BF16), kan_ref[...],
                    cav_ref[0, 0].astype(BF16), van_ref[...], bias_c_ref[...], bias_n_ref[...], _NT)
    oa = jnp.zeros((t, WIDTH), F32)
    for h in range(N_HEADS_A):
        oa = jnp.where(lane_t // HEAD_DIM == h, oa_all[h * t:(h + 1) * t], oa)
    oa_ref[...] = oa.astype(BF16)

    past = cbk_ref.shape[3]
    v_cache = jnp.concatenate([cbv_ref[pl.ds(h, past, stride=N_HEADS_B), :] for h in range(N_HEADS_B)],
                              axis=1).astype(BF16)
    ob_all = attend(expand(qb_ref[...]), cbk_ref[0, 0].astype(BF16), kbn_ref[...],
                    v_cache, vbn_ref[...], None, None, _NN)
    lam = _lambda(lam_ref, lam_init)
    g = g_ref[...]
    outs = []
    for h in range(N_HEADS_B):
        d = ob_all[2 * h * t:(2 * h + 1) * t] - lam * ob_all[(2 * h + 1) * t:(2 * h + 2) * t]
        outs.append(_rms(d[:, h * LANES:(h + 1) * LANES], g) * (1.0 - lam_init))
    ob_ref[...] = jnp.concatenate(outs, axis=1).astype(BF16)


def _sample_attn(qa, ka, va, qb, kb, vb, cak, cav, cbk, cbv, bias_c, bias_n, lam_p, g, lam_init, t, layer):
    m = qa.shape[0]
    bs = m // t
    past = cbk.shape[3]
    new = pl.BlockSpec((t, WIDTH), lambda i: (i, 0))
    cache = lambda c: pl.BlockSpec((1, 1) + c.shape[2:], lambda i: (layer, i, 0, 0))
    cbv_spec = pl.BlockSpec((past * N_HEADS_B, LANES), lambda i: (layer * bs + i, 0))
    out = jax.ShapeDtypeStruct((m, WIDTH), BF16)
    return pl.pallas_call(
        functools.partial(_sample_kernel, lam_init=lam_init),
        grid=(bs,),
        in_specs=[_resident(lam_p.shape), _resident(g.shape)] + [new] * 6
                 + [cache(cak), cache(cav), cache(cbk), cbv_spec,
                    _resident(bias_c.shape), _resident(bias_n.shape)],
        out_specs=[new, new],
        out_shape=[out, out],
        compiler_params=_params("arbitrary"),
        name="sample_attn",
    )(lam_p, g, qa, ka, va, qb, kb, vb, cak, cav, cbk, cbv, bias_c, bias_n)


def _conv_kernel(x_ref, g_ref, w_ref, cw_ref, st_ref, y_ref, tail_ref, carry_ref, *, tps, seg_len):
    tm, d = x_ref.shape
    h = _rms(x_ref[...], g_ref[...]).astype(BF16)
    carried = seg_len >= tm
    if carried:
        @pl.when(pl.program_id(0) % tps == 0)
        def _():
            carry_ref[...] = st_ref[0]

    cw = cw_ref[...]
    cc = 2 * FF_CHUNK
    for c in range(d // cc):
        cols = slice(c * cc, (c + 1) * cc)
        proj = lambda part: jnp.dot(h, w_ref[:, part * d + c * cc:part * d + (c + 1) * cc],
                                    preferred_element_type=F32)
        xin = proj(1) * proj(2)
        r = lax.broadcasted_iota(jnp.int32, xin.shape, 0)
        x1 = pltpu.roll(xin, 1, 0)
        x2 = pltpu.roll(xin, 2, 0)
        if carried:
            prev = carry_ref[:, cols]
            x1 = jnp.where(r == 0, prev[7:8], x1)
            x2 = jnp.where(r == 0, prev[6:7], jnp.where(r == 1, prev[7:8], x2))
            carry_ref[:, cols] = xin[tm - 8:]
            tail_ref[0, :, cols] = xin[tm - 8:]
        else:
            t = r % seg_len
            x1 = jnp.where(t >= 1, x1, st_ref[0, :, cols])
            x2 = jnp.where(t >= 2, x2, st_ref[1, :, cols])
            tail_ref[:, cols] = xin
        conv = cw[0:1, cols] * x2 + cw[1:2, cols] * x1 + cw[2:3, cols] * xin
        y_ref[:, cols] = (proj(0) * conv).astype(BF16)


def _conv_in(x, g, w, cw, state, *, seg_len):
    m, d = x.shape
    tm = min(ROW_TILE, m)
    row = lambda i: (i, 0)
    if seg_len >= tm:
        tps = seg_len // tm
        st_spec = pl.BlockSpec((1, 8, d), lambda i: (i // tps, 0, 0))
        tail_spec = pl.BlockSpec((1, 8, d), lambda i: (i // tps, 0, 0))
        tail_shape = jax.ShapeDtypeStruct((m // seg_len, 8, d), F32)
    else:
        tps = 1
        st_spec = pl.BlockSpec((2, tm, d), lambda i: (0, i, 0))
        tail_spec = pl.BlockSpec((tm, d), row)
        tail_shape = jax.ShapeDtypeStruct((m, d), F32)
    return pl.pallas_call(
        functools.partial(_conv_kernel, tps=tps, seg_len=seg_len),
        grid=(m // tm,),
        in_specs=[pl.BlockSpec((tm, d), row), _resident((1, d)), _resident(w.shape),
                  _resident(cw.shape), st_spec],
        out_specs=[pl.BlockSpec((tm, d), row), tail_spec],
        out_shape=[jax.ShapeDtypeStruct((m, d), BF16), tail_shape],
        scratch_shapes=[pltpu.VMEM((8, d), F32)],
        compiler_params=_params("arbitrary"),
        name="conv_in",
    )(x, g, w, cw, state)


def _ffn_kernel(*refs, n_parts, final):
    x_ref = refs[0]
    a_refs = refs[1:1 + n_parts]
    wo_ref, g_ref, wg_ref, wu_ref, wd_ref = refs[1 + n_parts:6 + n_parts]
    gf_ref = refs[6 + n_parts] if final else None
    o_ref, act_ref = refs[-2], refs[-1]

    a = a_refs[0][...] if n_parts == 1 else jnp.concatenate([r[...] for r in a_refs], axis=1)
    x1 = x_ref[...] + jnp.dot(a, wo_ref[...], preferred_element_type=F32)
    h = _rms(x1, g_ref[...]).astype(BF16)
    d_ff = wg_ref.shape[1]
    for c in range(d_ff // FF_CHUNK):
        cols = slice(c * FF_CHUNK, (c + 1) * FF_CHUNK)
        gate = jnp.dot(h, wg_ref[:, cols], preferred_element_type=F32)
        up = jnp.dot(h, wu_ref[:, cols], preferred_element_type=F32)
        act_ref[:, cols] = (gate * (1.0 / (1.0 + jnp.exp(-gate))) * up).astype(BF16)
    y = x1 + jnp.dot(act_ref[...], wd_ref[...], preferred_element_type=F32)
    o_ref[...] = _rms(y, gf_ref[...]) if final else y


def _mix_out_ffn(x, parts, wo, g, wg, wu, wd, g_final=None):
    m, d = x.shape
    tm = min(ROW_TILE, m)
    row = lambda i: (i, 0)
    final = g_final is not None
    extra = [g_final] if final else []
    return pl.pallas_call(
        functools.partial(_ffn_kernel, n_parts=len(parts), final=final),
        grid=(m // tm,),
        in_specs=[pl.BlockSpec((tm, d), row)] + [pl.BlockSpec((tm, p.shape[1]), row) for p in parts]
                 + [_resident(wo.shape), _resident((1, d)), _resident(wg.shape), _resident(wu.shape),
                    _resident(wd.shape)] + [_resident((1, d))] * len(extra),
        out_specs=pl.BlockSpec((tm, d), row),
        out_shape=jax.ShapeDtypeStruct((m, d), F32),
        scratch_shapes=[pltpu.VMEM((tm, wg.shape[1]), BF16)],
        compiler_params=_params("arbitrary"),
        name="mix_out_ffn",
    )(x, *parts, wo, g, wg, wu, wd, *extra)


def kernel(x_prompt, x_sample, cache_a_k, cache_a_v, cache_b_k, cache_b_v, state_conv, norm_mix, norm_ffn, norm_final, w_attn_in, w_attn_out, rel_bias, lambda_q1, lambda_k1, lambda_q2, lambda_k2, subln_g, w_conv_in, conv_w, w_conv_out, w_ffn_gate, w_ffn_up, w_ffn_down):
    b, s, d = x_prompt.shape
    bs, t, _ = x_sample.shape
    depth = norm_mix.shape[0]
    n_attn = w_attn_in.shape[0]
    n_conv = w_conv_in.shape[0]
    past = cache_b_k.shape[2]
    a_past = cache_a_k.shape[2]
    keep = min(BAND_PAST, s)
    assert s % BAND_PAST == 0 and a_past == BAND_PAST and d % (2 * FF_CHUNK) == 0
    assert w_ffn_gate.shape[2] % FF_CHUNK == 0 and 2 <= t and a_past + t <= BIAS_LANES

    xp = x_prompt.reshape(b * s, d)
    xs = x_sample.reshape(bs * t, d)
    row = lambda v: v.reshape(1, -1)

    cos_p, sin_p = _rope_tables(jnp.arange(s))
    cos_s, sin_s = (jnp.tile(tab, (bs, 1)) for tab in _rope_tables(past + jnp.arange(t)))

    table = rel_bias.reshape(n_attn * N_HEADS_A, 2 * MAX_REL + 1)
    band_bias = _band_bias(table)
    sample_bias = _sample_bias(table, t)

    cak_t = cache_a_k.transpose(0, 1, 3, 4, 2).reshape(n_attn, bs, WIDTH, a_past)
    cav_t = cache_a_v.transpose(0, 1, 3, 4, 2).reshape(n_attn, bs, WIDTH, a_past)
    cbk_t = cache_b_k.transpose(0, 1, 3, 4, 5, 2).reshape(n_attn, bs, WIDTH, past)
    cbv_rows = cache_b_v.reshape(n_attn * bs * past * N_HEADS_B, LANES)

    stacked_p = stacked_s = None
    pcs, scs = [], []
    for i in range(depth):
        j = i // 2
        g_mix = row(norm_mix[i])
        if i % 2 == 0:
            lam_init = 0.8 - 0.6 * math.exp(-0.3 * i)
            w_in = w_attn_in[j].astype(BF16)
            wo = w_attn_out[j].astype(BF16)
            lam_p = jnp.stack([lambda_q1[j], lambda_k1[j], lambda_q2[j], lambda_k2[j]])
            g_sub = row(subln_g[j])

            outs = _attn_in_proj(xp, g_mix, w_in, cos_p, sin_p, stacked_p,
                                 seg_len=s, keep=keep, layer=j, n_layers=n_attn, native=True)
            qa, ka, va, qb, kb, vb = (o.reshape(b, s, WIDTH) for o in outs[:6])
            stacked_p = outs[6:]
            oa = _band_attn(qa, ka, va, band_bias, j)
            ob = _diff_attn(qb, kb, vb, lam_p, g_sub, lam_init)
            parts_p = [oa.reshape(b * s, WIDTH), ob.reshape(b * s, WIDTH)]

            outs = _attn_in_proj(xs, g_mix, w_in, cos_s, sin_s, stacked_s,
                                 seg_len=t, keep=t, layer=j, n_layers=n_attn, native=False)
            stacked_s = outs[6:]
            oa, ob = _sample_attn(
                *outs[:6], cak_t, cav_t, cbk_t, cbv_rows,
                sample_bias[j, :, :a_past], sample_bias[j, :, a_past:a_past + t],
                lam_p, g_sub, lam_init, t, j)
            parts_s = [oa, ob]
        else:
            w_in = w_conv_in[j].astype(BF16)
            wo = w_conv_out[j].astype(BF16)
            yp, tail = _conv_in(xp, g_mix, w_in, conv_w[j], jnp.zeros((b, 8, d), F32), seg_len=s)
            pcs.append(tail[:, 6:])
            st = state_conv[j]
            stand_in = jnp.stack([jnp.pad(st[:, 1:2], ((0, 0), (0, t - 1), (0, 0))),
                                  jnp.pad(st, ((0, 0), (0, t - 2), (0, 0)))]).reshape(2, bs * t, d)
            ys, xin = _conv_in(xs, g_mix, w_in, conv_w[j], stand_in, seg_len=t)
            scs.append(xin.reshape(bs, t, d)[:, t - 2:])
            parts_p, parts_s = [yp], [ys]

        ffn = (wo, row(norm_ffn[i]), w_ffn_gate[i].astype(BF16), w_ffn_up[i].astype(BF16),
               w_ffn_down[i].astype(BF16))
        g_final = row(norm_final) if i == depth - 1 else None
        xp = _mix_out_ffn(xp, parts_p, *ffn, g_final=g_final)
        xs = _mix_out_ffn(xs, parts_s, *ffn, g_final=g_final)

    pak, pav, pbk, pbv = stacked_p
    sak, sav, sbk, sbv = stacked_s
    heads_a = lambda z: z.reshape(n_attn, b, N_HEADS_A, HEAD_DIM, keep).transpose(0, 1, 4, 2, 3)
    return (xp.reshape(b, s, d), xs.reshape(bs, t, d),
            heads_a(pak), heads_a(pav),
            pbk.reshape(n_attn, b, N_HEADS_B, 2, HEAD_DIM, s).transpose(0, 1, 5, 2, 3, 4),
            pbv.reshape(n_attn, b, s, N_HEADS_B, 2 * HEAD_DIM),
            jnp.stack(pcs),
            sak.reshape(n_attn, bs, t, N_HEADS_A, HEAD_DIM), sav.reshape(n_attn, bs, t, N_HEADS_A, HEAD_DIM),
            sbk.reshape(n_attn, bs, t, N_HEADS_B, 2, HEAD_DIM), sbv.reshape(n_attn, bs, t, N_HEADS_B, 2 * HEAD_DIM),
            jnp.stack(scs))
```

```python
import functools
import math

import jax
import jax.numpy as jnp
from jax import lax
from jax.experimental import pallas as pl
from jax.experimental.pallas import tpu as pltpu

F32 = jnp.float32
BF16 = jnp.bfloat16

EPS = 1e-6
NEG = -1e30
LOG2E = 1.4426950408889634
ROPE_THETA = 10000.0
CHUNK = 64
HEAD_DIM = 64
N_HEADS_A = 8
N_HEADS_B = 4
BAND_PAST = 512
MAX_REL = 128
BAND_KEYS = BAND_PAST + CHUNK
BAND_Q = 256
BAND_WIN = BAND_Q + BAND_PAST
DIFF_KEYS = 512
BIAS_LANES = 640
WIDTH = N_HEADS_A * HEAD_DIM
LANES = 128
ROW_TILE = 512
FF_CHUNK = 256
VMEM_LIMIT = 56 * 1024 * 1024

_NT = (((1,), (1,)), ((), ()))
_TN = (((0,), (0,)), ((), ()))
_NN = (((1,), (0,)), ((), ()))


def _resident(shape):
    return pl.BlockSpec(shape, lambda *_: (0,) * len(shape), pipeline_mode=pl.Buffered(1))


def _rms(x, g):
    return x * lax.rsqrt(jnp.mean(x * x, axis=-1, keepdims=True) + EPS) * g


def _params(*sem):
    return pltpu.CompilerParams(dimension_semantics=sem, vmem_limit_bytes=VMEM_LIMIT)


def _split_maps(q):
    low = lax.broadcasted_iota(jnp.int32, q.shape, 1) < HEAD_DIM
    zero = jnp.zeros_like(q)
    return jnp.concatenate([jnp.where(low, q, zero), jnp.where(low, zero, q)], axis=0)


def _attn_in_kernel(*refs, native):
    x_ref, g_ref, w_ref, cos_ref, sin_ref = refs[:5]
    (qa_ref, ka_ref, va_ref, qb_ref, kb_ref, vb_ref,
     kaf_ref, vaf_ref, kbf_ref, vbf_ref) = refs[-10:]
    h = _rms(x_ref[...], g_ref[...]).astype(BF16)
    q_scale = LOG2E / math.sqrt(HEAD_DIM)

    def proj(c):
        return jnp.dot(h, w_ref[:, c * WIDTH:(c + 1) * WIDTH], preferred_element_type=F32)

    cos = cos_ref[...]
    sin = sin_ref[...]
    low_half = (lax.broadcasted_iota(jnp.int32, cos.shape, 1) % HEAD_DIM) < (HEAD_DIM // 2)

    def rotary(z):
        parts = []
        for j in range(WIDTH // LANES):
            zj = z[:, j * LANES:(j + 1) * LANES]
            swapped = jnp.where(low_half, pltpu.roll(zj, LANES - HEAD_DIM // 2, 1),
                                pltpu.roll(zj, HEAD_DIM // 2, 1))
            parts.append(zj * cos + swapped * sin)
        return jnp.concatenate(parts, axis=1)

    def put(ref, z):
        if native:
            ref[0, 0] = z.T
        else:
            ref[...] = z

    qa_ref[...] = (proj(0) * q_scale).astype(BF16)
    ka = proj(1)
    ka_ref[...] = ka.astype(BF16)
    put(kaf_ref, ka)
    va = proj(2)
    va_ref[...] = va.astype(BF16)
    put(vaf_ref, va)
    qb_ref[...] = (rotary(proj(3)) * q_scale).astype(BF16)
    kb = rotary(proj(4))
    kb_ref[...] = kb.astype(BF16)
    put(kbf_ref, kb)
    vb = proj(5)
    vb_ref[...] = vb.astype(BF16)
    if native:
        tm = vb.shape[0]
        for h in range(N_HEADS_B):
            vbf_ref[pl.ds(h, tm, stride=N_HEADS_B), :] = vb[:, h * LANES:(h + 1) * LANES]
    else:
        vbf_ref[...] = vb


def _attn_in_proj(x, g, w, cos_t, sin_t, stacked, *, seg_len, keep, layer, n_layers, native):
    m, d = x.shape
    tm = min(ROW_TILE, m)
    n_tiles = m // tm
    n_tab = cos_t.shape[0] // tm
    n_seg = m // seg_len
    row = lambda i: (i, 0)
    tab = lambda i: (i % n_tab, 0)
    blk = pl.BlockSpec((tm, WIDTH), row)
    bf_out = jax.ShapeDtypeStruct((m, WIDTH), BF16)
    if native:
        tps, kpt = seg_len // tm, keep // tm
        t_blk = (1, 1, WIDTH, tm)
        keep_spec = pl.BlockSpec(
            t_blk, lambda i: (layer, i // tps, 0, jnp.clip(i % tps - (tps - kpt), 0, kpt - 1)))
        keep_out = jax.ShapeDtypeStruct((n_layers, n_seg, WIDTH, keep), F32)
        kb_spec = pl.BlockSpec(t_blk, lambda i: (layer, i // tps, 0, i % tps))
        kb_out = jax.ShapeDtypeStruct((n_layers, n_seg, WIDTH, seg_len), F32)
        vb_spec = pl.BlockSpec((tm * N_HEADS_B, LANES), lambda i: (layer * n_tiles + i, 0))
        vb_out = jax.ShapeDtypeStruct((n_layers * m * N_HEADS_B, LANES), F32)
    else:
        assert keep == seg_len
        keep_spec = kb_spec = vb_spec = pl.BlockSpec((tm, WIDTH), lambda i: (layer * n_tiles + i, 0))
        keep_out = kb_out = vb_out = jax.ShapeDtypeStruct((n_layers * m, WIDTH), F32)
    carried = [] if stacked is None else list(stacked)
    n_in = 5
    return pl.pallas_call(
        functools.partial(_attn_in_kernel, native=native),
        grid=(n_tiles,),
        in_specs=[pl.BlockSpec((tm, d), row), _resident((1, d)), _resident(w.shape),
                  pl.BlockSpec((tm, LANES), tab), pl.BlockSpec((tm, LANES), tab)]
                 + [pl.BlockSpec(memory_space=pl.ANY)] * len(carried),
        out_specs=[blk] * 6 + [keep_spec, keep_spec, kb_spec, vb_spec],
        out_shape=[bf_out] * 6 + [keep_out, keep_out, kb_out, vb_out],
        input_output_aliases={n_in + k: 6 + k for k in range(len(carried))},
        compiler_params=_params("arbitrary"),
        name="attn_in_proj",
    )(x, g, w, cos_t, sin_t, *carried)


def _rope_tables(pos):
    half = HEAD_DIM // 2
    inv = ROPE_THETA ** (-jnp.arange(half, dtype=F32) / half)
    ang = pos.astype(F32)[:, None] * inv[None, :]
    c, s = jnp.cos(ang), jnp.sin(ang)
    reps = LANES // HEAD_DIM
    return jnp.tile(c, (1, 2 * reps)), jnp.tile(jnp.concatenate([-s, s], axis=1), (1, reps))


def _band_bias_kernel(tab_ref, o_ref):
    r = pl.program_id(0)
    e_shape = (BAND_WIN + LANES, LANES)
    y = lax.broadcasted_iota(jnp.int32, e_shape, 0) - LANES
    lane = lax.broadcasted_iota(jnp.int32, e_shape, 1)
    idx = jnp.clip(BAND_PAST + lane - y, -MAX_REL, MAX_REL) + MAX_REL
    cols = []
    for hh in range(2):
        def body(d, acc, hh=hh):
            return jnp.where(idx == d, tab_ref[2 * r + hh, d], acc)

        e = lax.fori_loop(0, 2 * MAX_REL + 1, body, jnp.zeros(e_shape, F32))
        for c in range(BAND_Q // LANES):
            cols.append(e[LANES - LANES * c:LANES - LANES * c + BAND_WIN])
    bias = jnp.concatenate(cols, axis=1)

    kj = lax.broadcasted_iota(jnp.int32, bias.shape, 0)
    qi = lax.broadcasted_iota(jnp.int32, bias.shape, 1) % BAND_Q
    first_key = (qi // CHUNK) * CHUNK
    in_band = jnp.logical_and(kj >= first_key, kj < first_key + BAND_KEYS)
    o_ref[0] = jnp.where(in_band, bias * LOG2E, NEG)


def _band_bias(table):
    n = table.shape[0] // 2
    return pl.pallas_call(
        _band_bias_kernel,
        grid=(n,),
        in_specs=[pl.BlockSpec(memory_space=pltpu.SMEM)],
        out_specs=pl.BlockSpec((1, BAND_WIN, 2 * BAND_Q), lambda r: (r, 0, 0)),
        out_shape=jax.ShapeDtypeStruct((n, BAND_WIN, 2 * BAND_Q), F32),
        compiler_params=_params("arbitrary"),
        name="band_bias",
    )(table)


def _sample_bias_kernel(tab_ref, o_ref, *, t):
    r = pl.program_id(0)
    shape = (N_HEADS_A * t, BIAS_LANES)
    row = lax.broadcasted_iota(jnp.int32, shape, 0)
    kj = lax.broadcasted_iota(jnp.int32, shape, 1)
    idx = jnp.clip(BAND_PAST + row % t - kj, -MAX_REL, MAX_REL) + MAX_REL
    head = lax.broadcasted_iota(jnp.int32, (N_HEADS_A * t, 1), 0) // t

    def body(d, acc):
        col = jnp.zeros(head.shape, F32)
        for h in range(N_HEADS_A):
            col = jnp.where(head == h, tab_ref[N_HEADS_A * r + h, d], col)
        return jnp.where(idx == d, col, acc)

    o_ref[0] = lax.fori_loop(0, 2 * MAX_REL + 1, body, jnp.zeros(shape, F32)) * LOG2E


def _sample_bias(table, t):
    n = table.shape[0] // N_HEADS_A
    return pl.pallas_call(
        functools.partial(_sample_bias_kernel, t=t),
        grid=(n,),
        in_specs=[pl.BlockSpec(memory_space=pltpu.SMEM)],
        out_specs=pl.BlockSpec((1, N_HEADS_A * t, BIAS_LANES), lambda r: (r, 0, 0)),
        out_shape=jax.ShapeDtypeStruct((n, N_HEADS_A * t, BIAS_LANES), F32),
        compiler_params=_params("arbitrary"),
        name="sample_bias",
    )(table)


def _band_kernel(q_ref, kp_ref, kc_ref, vp_ref, vc_ref, bias_ref, o_ref):
    qblk = pl.program_id(2)
    n_sub = q_ref.shape[1] // BAND_Q

    def window(prev_ref, cur_ref, sub, first):
        lo, hi = sub * BAND_Q, (sub + 1) * BAND_Q
        parts = []
        if lo < BAND_PAST and not first:
            parts.append(prev_ref[0, lo:])
        parts.append(cur_ref[0, max(lo - BAND_PAST, 0):hi])
        return parts[0] if len(parts) == 1 else jnp.concatenate(parts, axis=0)

    def scores(sub, first):
        keys = window(kp_ref, kc_ref, sub, first)
        q = q_ref[0, sub * BAND_Q:(sub + 1) * BAND_Q, :]
        bias = bias_ref[0, BAND_WIN - keys.shape[0]:, :]
        return lax.dot_general(keys, _split_maps(q), _NT, preferred_element_type=F32) + bias

    def finish(s, sub, first):
        p = jnp.exp2(s - jnp.max(s, axis=0, keepdims=True))
        l = jnp.sum(p, axis=0, keepdims=True)
        vals = window(vp_ref, vc_ref, sub, first)
        o_t = lax.dot_general(vals, p.astype(BF16), _TN, preferred_element_type=F32) / l
        o = jnp.concatenate([o_t[:HEAD_DIM, :BAND_Q], o_t[HEAD_DIM:, BAND_Q:]], axis=0).T
        o_ref[0, sub * BAND_Q:(sub + 1) * BAND_Q, :] = o.astype(BF16)

    def run(first):
        nxt = scores(0, first)
        for sub in range(n_sub):
            cur_scores = nxt
            if sub + 1 < n_sub:
                nxt = scores(sub + 1, first)
            finish(cur_scores, sub, first)

    pl.when(qblk == 0)(functools.partial(run, True))
    pl.when(qblk > 0)(functools.partial(run, False))


def _band_attn(q, k, v, bias, layer):
    b, s, _ = q.shape
    bq = 2 * BAND_PAST if s % (2 * BAND_PAST) == 0 else BAND_PAST
    n_pairs = N_HEADS_A // 2
    cur = lambda hp, bi, qi: (bi, qi, hp)
    prev = lambda hp, bi, qi: (bi, jnp.maximum(qi * (bq // BAND_PAST) - 1, 0), hp)
    blk = (1, bq, LANES)
    pblk = (1, BAND_PAST, LANES)
    return pl.pallas_call(
        _band_kernel,
        grid=(n_pairs, b, s // bq),
        in_specs=[pl.BlockSpec(blk, cur), pl.BlockSpec(pblk, prev), pl.BlockSpec(blk, cur),
                  pl.BlockSpec(pblk, prev), pl.BlockSpec(blk, cur),
                  pl.BlockSpec((1, BAND_WIN, 2 * BAND_Q), lambda hp, bi, qi: (layer * n_pairs + hp, 0, 0))],
        out_specs=pl.BlockSpec(blk, cur),
        out_shape=jax.ShapeDtypeStruct((b, s, WIDTH), BF16),
        compiler_params=_params("arbitrary", "arbitrary", "arbitrary"),
        name="band_attn",
    )(q, k, k, v, v, bias)


def _lambda(lam_ref, lam_init):
    lp = lam_ref[...]
    a1 = jnp.sum(lp[0:1] * lp[1:2], axis=-1, keepdims=True)
    a2 = jnp.sum(lp[2:3] * lp[3:4], axis=-1, keepdims=True)
    return jnp.exp(a1) - jnp.exp(a2) + lam_init


def _diff_kernel(lam_ref, g_ref, q_ref, k_ref, v_ref, o_ref, m_ref, l_ref, acc_ref, *bufs, lam_init):
    qi = pl.program_id(2)
    q = q_ref[0]
    bq = q.shape[0]
    bk = bq // 2
    q2 = _split_maps(q)
    queries = (q2[:bq], q2[bq:])

    m_ref[...] = jnp.full(m_ref.shape, NEG, F32)
    l_ref[...] = jnp.zeros(l_ref.shape, F32)
    acc_ref[...] = jnp.zeros(acc_ref.shape, F32)

    def block(ref, kj):
        return ref[0, pl.ds(pl.multiple_of(kj * bk, bk), bk), :]

    def scores(kj, stream, buf, diagonal=None):
        s_ref, mx_ref = bufs[4 * stream + 2 * buf], bufs[4 * stream + 2 * buf + 1]
        s = lax.dot_general(block(k_ref, kj), queries[stream], _NT, preferred_element_type=F32)
        if diagonal is not None:
            k_chunk = lax.broadcasted_iota(jnp.int32, s.shape, 0) // CHUNK + diagonal * (bk // CHUNK)
            q_chunk = lax.broadcasted_iota(jnp.int32, s.shape, 1) // CHUNK
            s = jnp.where(k_chunk <= q_chunk, s, NEG)
        s_ref[...] = s
        mx_ref[...] = jnp.max(s, axis=0, keepdims=True)

    def absorb(kj, stream, buf):
        s_ref, mx_ref = bufs[4 * stream + 2 * buf], bufs[4 * stream + 2 * buf + 1]
        cols = slice(stream * bq, (stream + 1) * bq)
        m_old = m_ref[:, cols]
        m_new = jnp.maximum(m_old, mx_ref[...])
        alpha = jnp.exp2(m_old - m_new)
        p = jnp.exp2(s_ref[...] - m_new)
        l_ref[:, cols] = alpha * l_ref[:, cols] + jnp.sum(p, axis=0, keepdims=True)
        pv = lax.dot_general(block(v_ref, kj), p.astype(BF16), _TN, preferred_element_type=F32)
        acc_ref[:, cols] = alpha * acc_ref[:, cols] + pv
        m_ref[:, cols] = m_new

    def stage(kj, buf, look=True, look_diagonal=None):
        if look:
            scores(kj + 1, 0, 1 - buf, look_diagonal)
        absorb(kj, 1, buf)
        if look:
            scores(kj + 1, 1, 1 - buf, look_diagonal)
        absorb(kj, 0, buf)

    @pl.when(qi == 0)
    def _():
        scores(0, 0, 0, diagonal=0)
        scores(0, 1, 0, diagonal=0)
        stage(0, 0, look_diagonal=1)
        stage(1, 1, look=False)

    @pl.when(qi > 0)
    def _():
        scores(0, 0, 0)
        scores(0, 1, 0)

    def two_blocks(pair, carry):
        stage(2 * pair, 0)
        stage(2 * pair + 1, 1)
        return carry

    lax.fori_loop(0, jnp.maximum(qi - 1, 0), two_blocks, 0)

    @pl.when(qi > 0)
    def _():
        last = 2 * qi
        stage(last - 2, 0)
        stage(last - 1, 1, look_diagonal=0)
        stage(last, 0, look_diagonal=1)
        stage(last + 1, 1, look=False)

    o_t = acc_ref[...] / l_ref[...]
    lam = _lambda(lam_ref, lam_init)
    d = (o_t[:, :bq] - lam * o_t[:, bq:]).T
    o_ref[0] = (_rms(d, g_ref[...]) * (1.0 - lam_init)).astype(BF16)


def _diff_attn(q, k, v, lam_p, g, lam_init):
    b, s, _ = q.shape
    bq = 2 * DIFF_KEYS
    assert s % bq == 0
    seq = pl.BlockSpec((1, s, LANES), lambda bi, h, qi: (bi, 0, h))
    blk = pl.BlockSpec((1, bq, LANES), lambda bi, h, qi: (bi, qi, h))
    return pl.pallas_call(
        functools.partial(_diff_kernel, lam_init=lam_init),
        grid=(b, N_HEADS_B, s // bq),
        in_specs=[_resident(lam_p.shape), _resident(g.shape), blk, seq, seq],
        out_specs=blk,
        out_shape=jax.ShapeDtypeStruct((b, s, WIDTH), BF16),
        scratch_shapes=[pltpu.VMEM((1, 2 * bq), F32), pltpu.VMEM((1, 2 * bq), F32),
                        pltpu.VMEM((LANES, 2 * bq), F32),
                        ] + [pltpu.VMEM((DIFF_KEYS, bq), F32), pltpu.VMEM((1, bq), F32)] * 4,
        compiler_params=_params("arbitrary", "arbitrary", "arbitrary"),
        name="diff_attn",
    )(lam_p, g, q, k, v)


def _sample_kernel(lam_ref, g_ref, qa_ref, kan_ref, van_ref, qb_ref, kbn_ref, vbn_ref,
                   cak_ref, cav_ref, cbk_ref, cbv_ref, bias_c_ref, bias_n_ref,
                   oa_ref, ob_ref, *, lam_init):
    t = qa_ref.shape[0]
    n_grp = WIDTH // HEAD_DIM
    rows = n_grp * t
    grp_of_row = lax.broadcasted_iota(jnp.int32, (rows, WIDTH), 0) // t
    grp_of_lane = lax.broadcasted_iota(jnp.int32, (rows, WIDTH), 1) // HEAD_DIM
    own = grp_of_row == grp_of_lane
    lane_t = lax.broadcasted_iota(jnp.int32, (t, WIDTH), 1)

    def expand(q):
        qe = jnp.concatenate([q] * n_grp, axis=0)
        return jnp.where(own, qe, jnp.zeros_like(qe))

    def attend(qe, kt_cache, k_new, v_cache, v_new, bias_c, bias_n, v_dims):
        s_c = jnp.dot(qe, kt_cache, preferred_element_type=F32)
        s_n = lax.dot_general(qe, k_new, _NT, preferred_element_type=F32)
        if bias_c is not None:
            s_c = s_c + bias_c
            s_n = s_n + bias_n
        m = jnp.maximum(jnp.max(s_c, axis=-1, keepdims=True), jnp.max(s_n, axis=-1, keepdims=True))
        p_c = jnp.exp2(s_c - m)
        p_n = jnp.exp2(s_n - m)
        l = jnp.sum(p_c, axis=-1, keepdims=True) + jnp.sum(p_n, axis=-1, keepdims=True)
        o = (lax.dot_general(p_c.astype(BF16), v_cache, v_dims, preferred_element_type=F32)
             + jnp.dot(p_n.astype(BF16), v_new, preferred_element_type=F32))
        return o / l

    oa_all = attend(expand(qa_ref[...]), cak_ref[0, 0].astype(BF16), kan_ref[...],
                    cav_ref[0, 0].astype(BF16), van_ref[...], bias_c_ref[...], bias_n_ref[...], _NT)
    oa = jnp.zeros((t, WIDTH), F32)
    for h in range(N_HEADS_A):
        oa = jnp.where(lane_t // HEAD_DIM == h, oa_all[h * t:(h + 1) * t], oa)
    oa_ref[...] = oa.astype(BF16)

    past = cbk_ref.shape[3]
    v_cache = jnp.concatenate([cbv_ref[pl.ds(h, past, stride=N_HEADS_B), :] for h in range(N_HEADS_B)],
                              axis=1).astype(BF16)
    ob_all = attend(expand(qb_ref[...]), cbk_ref[0, 0].astype(BF16), kbn_ref[...],
                    v_cache, vbn_ref[...], None, None, _NN)
    lam = _lambda(lam_ref, lam_init)
    g = g_ref[...]
    outs = []
    for h in range(N_HEADS_B):
        d = ob_all[2 * h * t:(2 * h + 1) * t] - lam * ob_all[(2 * h + 1) * t:(2 * h + 2) * t]
        outs.append(_rms(d[:, h * LANES:(h + 1) * LANES], g) * (1.0 - lam_init))
    ob_ref[...] = jnp.concatenate(outs, axis=1).astype(BF16)


def _sample_attn(qa, ka, va, qb, kb, vb, cak, cav, cbk, cbv, bias_c, bias_n, lam_p, g, lam_init, t, layer):
    m = qa.shape[0]
    bs = m // t
    past = cbk.shape[3]
    new = pl.BlockSpec((t, WIDTH), lambda i: (i, 0))
    cache = lambda c: pl.BlockSpec((1, 1) + c.shape[2:], lambda i: (layer, i, 0, 0))
    cbv_spec = pl.BlockSpec((past * N_HEADS_B, LANES), lambda i: (layer * bs + i, 0))
    out = jax.ShapeDtypeStruct((m, WIDTH), BF16)
    return pl.pallas_call(
        functools.partial(_sample_kernel, lam_init=lam_init),
        grid=(bs,),
        in_specs=[_resident(lam_p.shape), _resident(g.shape)] + [new] * 6
                 + [cache(cak), cache(cav), cache(cbk), cbv_spec,
                    _resident(bias_c.shape), _resident(bias_n.shape)],
        out_specs=[new, new],
        out_shape=[out, out],
        compiler_params=_params("arbitrary"),
        name="sample_attn",
    )(lam_p, g, qa, ka, va, qb, kb, vb, cak, cav, cbk, cbv, bias_c, bias_n)


def _conv_kernel(x_ref, g_ref, w_ref, cw_ref, st_ref, y_ref, tail_ref, carry_ref, *, tps, seg_len):
    tm, d = x_ref.shape
    h = _rms(x_ref[...], g_ref[...]).astype(BF16)
    carried = seg_len >= tm
    if carried:
        @pl.when(pl.program_id(0) % tps == 0)
        def _():
            carry_ref[...] = st_ref[0]

    cw = cw_ref[...]
    cc = 2 * FF_CHUNK
    for c in range(d // cc):
        cols = slice(c * cc, (c + 1) * cc)
        proj = lambda part: jnp.dot(h, w_ref[:, part * d + c * cc:part * d + (c + 1) * cc],
                                    preferred_element_type=F32)
        xin = proj(1) * proj(2)
        r = lax.broadcasted_iota(jnp.int32, xin.shape, 0)
        x1 = pltpu.roll(xin, 1, 0)
        x2 = pltpu.roll(xin, 2, 0)
        if carried:
            prev = carry_ref[:, cols]
            x1 = jnp.where(r == 0, prev[7:8], x1)
            x2 = jnp.where(r == 0, prev[6:7], jnp.where(r == 1, prev[7:8], x2))
            carry_ref[:, cols] = xin[tm - 8:]
            tail_ref[0, :, cols] = xin[tm - 8:]
        else:
            t = r % seg_len
            x1 = jnp.where(t >= 1, x1, st_ref[0, :, cols])
            x2 = jnp.where(t >= 2, x2, st_ref[1, :, cols])
            tail_ref[:, cols] = xin
        conv = cw[0:1, cols] * x2 + cw[1:2, cols] * x1 + cw[2:3, cols] * xin
        y_ref[:, cols] = (proj(0) * conv).astype(BF16)


def _conv_in(x, g, w, cw, state, *, seg_len):
    m, d = x.shape
    tm = min(ROW_TILE, m)
    row = lambda i: (i, 0)
    if seg_len >= tm:
        tps = seg_len // tm
        st_spec = pl.BlockSpec((1, 8, d), lambda i: (i // tps, 0, 0))
        tail_spec = pl.BlockSpec((1, 8, d), lambda i: (i // tps, 0, 0))
        tail_shape = jax.ShapeDtypeStruct((m // seg_len, 8, d), F32)
    else:
        tps = 1
        st_spec = pl.BlockSpec((2, tm, d), lambda i: (0, i, 0))
        tail_spec = pl.BlockSpec((tm, d), row)
        tail_shape = jax.ShapeDtypeStruct((m, d), F32)
    return pl.pallas_call(
        functools.partial(_conv_kernel, tps=tps, seg_len=seg_len),
        grid=(m // tm,),
        in_specs=[pl.BlockSpec((tm, d), row), _resident((1, d)), _resident(w.shape),
                  _resident(cw.shape), st_spec],
        out_specs=[pl.BlockSpec((tm, d), row), tail_spec],
        out_shape=[jax.ShapeDtypeStruct((m, d), BF16), tail_shape],
        scratch_shapes=[pltpu.VMEM((8, d), F32)],
        compiler_params=_params("arbitrary"),
        name="conv_in",
    )(x, g, w, cw, state)


def _ffn_kernel(*refs, n_parts, final):
    x_ref = refs[0]
    a_refs = refs[1:1 + n_parts]
    wo_ref, g_ref, wg_ref, wu_ref, wd_ref = refs[1 + n_parts:6 + n_parts]
    gf_ref = refs[6 + n_parts] if final else None
    o_ref, act_ref = refs[-2], refs[-1]

    a = a_refs[0][...] if n_parts == 1 else jnp.concatenate([r[...] for r in a_refs], axis=1)
    x1 = x_ref[...] + jnp.dot(a, wo_ref[...], preferred_element_type=F32)
    h = _rms(x1, g_ref[...]).astype(BF16)
    d_ff = wg_ref.shape[1]
    for c in range(d_ff // FF_CHUNK):
        cols = slice(c * FF_CHUNK, (c + 1) * FF_CHUNK)
        gate = jnp.dot(h, wg_ref[:, cols], preferred_element_type=F32)
        up = jnp.dot(h, wu_ref[:, cols], preferred_element_type=F32)
        act_ref[:, cols] = (gate * (1.0 / (1.0 + jnp.exp(-gate))) * up).astype(BF16)
    y = x1 + jnp.dot(act_ref[...], wd_ref[...], preferred_element_type=F32)
    o_ref[...] = _rms(y, gf_ref[...]) if final else y


def _mix_out_ffn(x, parts, wo, g, wg, wu, wd, g_final=None):
    m, d = x.shape
    tm = min(ROW_TILE, m)
    row = lambda i: (i, 0)
    final = g_final is not None
    extra = [g_final] if final else []
    return pl.pallas_call(
        functools.partial(_ffn_kernel, n_parts=len(parts), final=final),
        grid=(m // tm,),
        in_specs=[pl.BlockSpec((tm, d), row)] + [pl.BlockSpec((tm, p.shape[1]), row) for p in parts]
                 + [_resident(wo.shape), _resident((1, d)), _resident(wg.shape), _resident(wu.shape),
                    _resident(wd.shape)] + [_resident((1, d))] * len(extra),
        out_specs=pl.BlockSpec((tm, d), row),
        out_shape=jax.ShapeDtypeStruct((m, d), F32),
        scratch_shapes=[pltpu.VMEM((tm, wg.shape[1]), BF16)],
        compiler_params=_params("arbitrary"),
        name="mix_out_ffn",
    )(x, *parts, wo, g, wg, wu, wd, *extra)


def kernel(x_prompt, x_sample, cache_a_k, cache_a_v, cache_b_k, cache_b_v, state_conv, norm_mix, norm_ffn, norm_final, w_attn_in, w_attn_out, rel_bias, lambda_q1, lambda_k1, lambda_q2, lambda_k2, subln_g, w_conv_in, conv_w, w_conv_out, w_ffn_gate, w_ffn_up, w_ffn_down):
    b, s, d = x_prompt.shape
    bs, t, _ = x_sample.shape
    depth = norm_mix.shape[0]
    n_attn = w_attn_in.shape[0]
    n_conv = w_conv_in.shape[0]
    past = cache_b_k.shape[2]
    a_past = cache_a_k.shape[2]
    keep = min(BAND_PAST, s)
    assert s % BAND_PAST == 0 and a_past == BAND_PAST and d % (2 * FF_CHUNK) == 0
    assert w_ffn_gate.shape[2] % FF_CHUNK == 0 and 2 <= t and a_past + t <= BIAS_LANES

    xp = x_prompt.reshape(b * s, d)
    xs = x_sample.reshape(bs * t, d)
    row = lambda v: v.reshape(1, -1)

    cos_p, sin_p = _rope_tables(jnp.arange(s))
    cos_s, sin_s = (jnp.tile(tab, (bs, 1)) for tab in _rope_tables(past + jnp.arange(t)))

    table = rel_bias.reshape(n_attn * N_HEADS_A, 2 * MAX_REL + 1)
    band_bias = _band_bias(table)
    sample_bias = _sample_bias(table, t)

    cak_t = cache_a_k.transpose(0, 1, 3, 4, 2).reshape(n_attn, bs, WIDTH, a_past)
    cav_t = cache_a_v.transpose(0, 1, 3, 4, 2).reshape(n_attn, bs, WIDTH, a_past)
    cbk_t = cache_b_k.transpose(0, 1, 3, 4, 5, 2).reshape(n_attn, bs, WIDTH, past)
    cbv_rows = cache_b_v.reshape(n_attn * bs * past * N_HEADS_B, LANES)

    stacked_p = stacked_s = None
    pcs, scs = [], []
    for i in range(depth):
        j = i // 2
        g_mix = row(norm_mix[i])
        if i % 2 == 0:
            lam_init = 0.8 - 0.6 * math.exp(-0.3 * i)
            w_in = w_attn_in[j].astype(BF16)
            wo = w_attn_out[j].astype(BF16)
            lam_p = jnp.stack([lambda_q1[j], lambda_k1[j], lambda_q2[j], lambda_k2[j]])
            g_sub = row(subln_g[j])

            outs = _attn_in_proj(xp, g_mix, w_in, cos_p, sin_p, stacked_p,
                                 seg_len=s, keep=keep, layer=j, n_layers=n_attn, native=True)
            qa, ka, va, qb, kb, vb = (o.reshape(b, s, WIDTH) for o in outs[:6])
            stacked_p = outs[6:]
            oa = _band_attn(qa, ka, va, band_bias, j)
            ob = _diff_attn(qb, kb, vb, lam_p, g_sub, lam_init)
            parts_p = [oa.reshape(b * s, WIDTH), ob.reshape(b * s, WIDTH)]

            outs = _attn_in_proj(xs, g_mix, w_in, cos_s, sin_s, stacked_s,
                                 seg_len=t, keep=t, layer=j, n_layers=n_attn, native=False)
            stacked_s = outs[6:]
            oa, ob = _sample_attn(
                *outs[:6], cak_t, cav_t, cbk_t, cbv_rows,
                sample_bias[j, :, :a_past], sample_bias[j, :, a_past:a_past + t],
                lam_p, g_sub, lam_init, t, j)
            parts_s = [oa, ob]
        else:
            w_in = w_conv_in[j].astype(BF16)
            wo = w_conv_out[j].astype(BF16)
            yp, tail = _conv_in(xp, g_mix, w_in, conv_w[j], jnp.zeros((b, 8, d), F32), seg_len=s)
            pcs.append(tail[:, 6:])
            st = state_conv[j]
            stand_in = jnp.stack([jnp.pad(st[:, 1:2], ((0, 0), (0, t - 1), (0, 0))),
                                  jnp.pad(st, ((0, 0), (0, t - 2), (0, 0)))]).reshape(2, bs * t, d)
            ys, xin = _conv_in(xs, g_mix, w_in, conv_w[j], stand_in, seg_len=t)
            scs.append(xin.reshape(bs, t, d)[:, t - 2:])
            parts_p, parts_s = [yp], [ys]

        ffn = (wo, row(norm_ffn[i]), w_ffn_gate[i].astype(BF16), w_ffn_up[i].astype(BF16),
               w_ffn_down[i].astype(BF16))
        g_final = row(norm_final) if i == depth - 1 else None
        xp = _mix_out_ffn(xp, parts_p, *ffn, g_final=g_final)
        xs = _mix_out_ffn(xs, parts_s, *ffn, g_final=g_final)

    pak, pav, pbk, pbv = stacked_p
    sak, sav, sbk, sbv = stacked_s
    heads_a = lambda z: z.reshape(n_attn, b, N_HEADS_A, HEAD_DIM, keep).transpose(0, 1, 4, 2, 3)
    return (xp.reshape(b, s, d), xs.reshape(bs, t, d),
            heads_a(pak), heads_a(pav),
            pbk.reshape(n_attn, b, N_HEADS_B, 2, HEAD_DIM, s).transpose(0, 1, 5, 2, 3, 4),
            pbv.reshape(n_attn, b, s, N_HEADS_B, 2 * HEAD_DIM),
            jnp.stack(pcs),
            sak.reshape(n_attn, bs, t, N_HEADS_A, HEAD_DIM), sav.reshape(n_attn, bs, t, N_HEADS_A, HEAD_DIM),
            sbk.reshape(n_attn, bs, t, N_HEADS_B, 2, HEAD_DIM), sbv.reshape(n_attn, bs, t, N_HEADS_B, 2 * HEAD_DIM),
            jnp.stack(scs))
```

```python
import functools
import math

import jax
import jax.numpy as jnp
from jax import lax
from jax.experimental import pallas as pl
from jax.experimental.pallas import tpu as pltpu

F32 = jnp.float32
BF16 = jnp.bfloat16

EPS = 1e-6
NEG = -1e30
LOG2E = 1.4426950408889634
ROPE_THETA = 10000.0
CHUNK = 64
HEAD_DIM = 64
N_HEADS_A = 8
N_HEADS_B = 4
BAND_PAST = 512
MAX_REL = 128
BAND_KEYS = BAND_PAST + CHUNK
BAND_Q = 256
BAND_WIN = BAND_Q + BAND_PAST
DIFF_KEYS = 512
BIAS_LANES = 640
WIDTH = N_HEADS_A * HEAD_DIM
LANES = 128
ROW_TILE = 512
FF_CHUNK = 256
VMEM_LIMIT = 56 * 1024 * 1024

_NT = (((1,), (1,)), ((), ()))
_TN = (((0,), (0,)), ((), ()))
_NN = (((1,), (0,)), ((), ()))


def _resident(shape):
    return pl.BlockSpec(shape, lambda *_: (0,) * len(shape), pipeline_mode=pl.Buffered(1))


def _rms(x, g):
    return x * lax.rsqrt(jnp.mean(x * x, axis=-1, keepdims=True) + EPS) * g


def _params(*sem):
    return pltpu.CompilerParams(dimension_semantics=sem, vmem_limit_bytes=VMEM_LIMIT)


SUM_ROWS = 16


def _values_and_sum(v_t, p):
    ones = jnp.ones((SUM_ROWS, v_t.shape[1]), v_t.dtype)
    return jnp.dot(jnp.concatenate([v_t, ones], axis=0), p, preferred_element_type=F32)


def _split_maps(q):
    low = lax.broadcasted_iota(jnp.int32, q.shape, 1) < HEAD_DIM
    zero = jnp.zeros_like(q)
    return jnp.concatenate([jnp.where(low, q, zero), jnp.where(low, zero, q)], axis=0)


def _attn_in_kernel(*refs, native):
    x_ref, g_ref, w_ref, cos_ref, sin_ref = refs[:5]
    (qa_ref, ka_ref, va_ref, qb_ref, kb_ref, vb_ref,
     kaf_ref, vaf_ref, kbf_ref, vbf_ref) = refs[-10:]
    h = _rms(x_ref[...], g_ref[...]).astype(BF16)
    q_scale = LOG2E / math.sqrt(HEAD_DIM)

    def proj(c):
        return jnp.dot(h, w_ref[:, c * WIDTH:(c + 1) * WIDTH], preferred_element_type=F32)

    cos = cos_ref[...]
    sin = sin_ref[...]
    low_half = (lax.broadcasted_iota(jnp.int32, cos.shape, 1) % HEAD_DIM) < (HEAD_DIM // 2)

    def rotary(z):
        parts = []
        for j in range(WIDTH // LANES):
            zj = z[:, j * LANES:(j + 1) * LANES]
            swapped = jnp.where(low_half, pltpu.roll(zj, LANES - HEAD_DIM // 2, 1),
                                pltpu.roll(zj, HEAD_DIM // 2, 1))
            parts.append(zj * cos + swapped * sin)
        return jnp.concatenate(parts, axis=1)

    qa_ref[...] = (proj(0) * q_scale).astype(BF16)
    ka = proj(1)
    ka_ref[...] = ka.astype(BF16)
    va = proj(2)
    qb_ref[...] = (rotary(proj(3)) * q_scale).astype(BF16)
    kb = rotary(proj(4))
    kb_ref[...] = kb.astype(BF16)
    vb = proj(5)
    if native:
        kaf_ref[0, 0] = ka.T
        va_t = va.T
        vaf_ref[0, 0] = va_t
        va_ref[0] = va_t.astype(BF16)
        kbf_ref[0, 0] = kb.T
        vb_ref[0, 0] = vb.T.astype(BF16)
        tm = vb.shape[0]
        for h in range(N_HEADS_B):
            vbf_ref[pl.ds(h, tm, stride=N_HEADS_B), :] = vb[:, h * LANES:(h + 1) * LANES]
    else:
        kaf_ref[...] = ka
        vaf_ref[...] = va
        va_ref[...] = va.astype(BF16)
        kbf_ref[...] = kb
        vb_ref[...] = vb.astype(BF16)
        vbf_ref[...] = vb


def _attn_in_proj(x, g, w, cos_t, sin_t, stacked, *, seg_len, keep, layer, n_layers, native):
    m, d = x.shape
    tm = min(ROW_TILE, m)
    n_tiles = m // tm
    n_tab = cos_t.shape[0] // tm
    n_seg = m // seg_len
    row = lambda i: (i, 0)
    tab = lambda i: (i % n_tab, 0)
    blk = pl.BlockSpec((tm, WIDTH), row)
    bf_out = jax.ShapeDtypeStruct((m, WIDTH), BF16)
    if native:
        tps, kpt = seg_len // tm, keep // tm
        t_blk = (1, 1, WIDTH, tm)
        keep_spec = pl.BlockSpec(
            t_blk, lambda i: (layer, i // tps, 0, jnp.clip(i % tps - (tps - kpt), 0, kpt - 1)))
        keep_out = jax.ShapeDtypeStruct((n_layers, n_seg, WIDTH, keep), F32)
        kb_spec = pl.BlockSpec(t_blk, lambda i: (layer, i // tps, 0, i % tps))
        kb_out = jax.ShapeDtypeStruct((n_layers, n_seg, WIDTH, seg_len), F32)
        vb_spec = pl.BlockSpec((tm * N_HEADS_B, LANES), lambda i: (layer * n_tiles + i, 0))
        vb_out = jax.ShapeDtypeStruct((n_layers * m * N_HEADS_B, LANES), F32)
        va_spec = pl.BlockSpec((1, WIDTH, tm), lambda i: (i // tps, 0, i % tps))
        va_out = jax.ShapeDtypeStruct((n_seg, WIDTH, seg_len), BF16)
        vbb_spec = pl.BlockSpec((1, 1, WIDTH, tm), lambda i: (i // tps, i % tps, 0, 0))
        vbb_out = jax.ShapeDtypeStruct((n_seg, tps, WIDTH, tm), BF16)
    else:
        assert keep == seg_len
        keep_spec = kb_spec = vb_spec = pl.BlockSpec((tm, WIDTH), lambda i: (layer * n_tiles + i, 0))
        keep_out = kb_out = vb_out = jax.ShapeDtypeStruct((n_layers * m, WIDTH), F32)
        va_spec = vbb_spec = blk
        va_out = vbb_out = bf_out
    carried = [] if stacked is None else list(stacked)
    n_in = 5
    return pl.pallas_call(
        functools.partial(_attn_in_kernel, native=native),
        grid=(n_tiles,),
        in_specs=[pl.BlockSpec((tm, d), row), _resident((1, d)), _resident(w.shape),
                  pl.BlockSpec((tm, LANES), tab), pl.BlockSpec((tm, LANES), tab)]
                 + [pl.BlockSpec(memory_space=pl.ANY)] * len(carried),
        out_specs=[blk, blk, va_spec, blk, blk, vbb_spec, keep_spec, keep_spec, kb_spec, vb_spec],
        out_shape=[bf_out, bf_out, va_out, bf_out, bf_out, vbb_out, keep_out, keep_out, kb_out, vb_out],
        input_output_aliases={n_in + k: 6 + k for k in range(len(carried))},
        compiler_params=_params("arbitrary"),
        name="attn_in_proj",
    )(x, g, w, cos_t, sin_t, *carried)


def _rope_tables(pos):
    half = HEAD_DIM // 2
    inv = ROPE_THETA ** (-jnp.arange(half, dtype=F32) / half)
    ang = pos.astype(F32)[:, None] * inv[None, :]
    c, s = jnp.cos(ang), jnp.sin(ang)
    reps = LANES // HEAD_DIM
    return jnp.tile(c, (1, 2 * reps)), jnp.tile(jnp.concatenate([-s, s], axis=1), (1, reps))


def _band_bias_kernel(tab_ref, o_ref):
    r = pl.program_id(0)
    e_shape = (BAND_WIN + LANES, LANES)
    y = lax.broadcasted_iota(jnp.int32, e_shape, 0) - LANES
    lane = lax.broadcasted_iota(jnp.int32, e_shape, 1)
    idx = jnp.clip(BAND_PAST + lane - y, -MAX_REL, MAX_REL) + MAX_REL
    cols = []
    for hh in range(2):
        def body(d, acc, hh=hh):
            return jnp.where(idx == d, tab_ref[2 * r + hh, d], acc)

        e = lax.fori_loop(0, 2 * MAX_REL + 1, body, jnp.zeros(e_shape, F32))
        for c in range(BAND_Q // LANES):
            cols.append(e[LANES - LANES * c:LANES - LANES * c + BAND_WIN])
    bias = jnp.concatenate(cols, axis=1)

    kj = lax.broadcasted_iota(jnp.int32, bias.shape, 0)
    qi = lax.broadcasted_iota(jnp.int32, bias.shape, 1) % BAND_Q
    first_key = (qi // CHUNK) * CHUNK
    in_band = jnp.logical_and(kj >= first_key, kj < first_key + BAND_KEYS)
    o_ref[0] = jnp.where(in_band, bias * LOG2E, NEG)


def _band_bias(table):
    n = table.shape[0] // 2
    return pl.pallas_call(
        _band_bias_kernel,
        grid=(n,),
        in_specs=[pl.BlockSpec(memory_space=pltpu.SMEM)],
        out_specs=pl.BlockSpec((1, BAND_WIN, 2 * BAND_Q), lambda r: (r, 0, 0)),
        out_shape=jax.ShapeDtypeStruct((n, BAND_WIN, 2 * BAND_Q), F32),
        compiler_params=_params("arbitrary"),
        name="band_bias",
    )(table)


def _sample_bias_kernel(tab_ref, o_ref, *, t):
    r = pl.program_id(0)
    shape = (N_HEADS_A * t, BIAS_LANES)
    row = lax.broadcasted_iota(jnp.int32, shape, 0)
    kj = lax.broadcasted_iota(jnp.int32, shape, 1)
    idx = jnp.clip(BAND_PAST + row % t - kj, -MAX_REL, MAX_REL) + MAX_REL
    head = lax.broadcasted_iota(jnp.int32, (N_HEADS_A * t, 1), 0) // t

    def body(d, acc):
        col = jnp.zeros(head.shape, F32)
        for h in range(N_HEADS_A):
            col = jnp.where(head == h, tab_ref[N_HEADS_A * r + h, d], col)
        return jnp.where(idx == d, col, acc)

    o_ref[0] = lax.fori_loop(0, 2 * MAX_REL + 1, body, jnp.zeros(shape, F32)) * LOG2E


def _sample_bias(table, t):
    n = table.shape[0] // N_HEADS_A
    return pl.pallas_call(
        functools.partial(_sample_bias_kernel, t=t),
        grid=(n,),
        in_specs=[pl.BlockSpec(memory_space=pltpu.SMEM)],
        out_specs=pl.BlockSpec((1, N_HEADS_A * t, BIAS_LANES), lambda r: (r, 0, 0)),
        out_shape=jax.ShapeDtypeStruct((n, N_HEADS_A * t, BIAS_LANES), F32),
        compiler_params=_params("arbitrary"),
        name="sample_bias",
    )(table)


def _band_kernel(q_ref, kp_ref, kc_ref, vp_ref, vc_ref, bias_ref, o_ref):
    qblk = pl.program_id(2)
    n_sub = q_ref.shape[1] // BAND_Q

    def window(prev_ref, cur_ref, sub, first, axis):
        lo, hi = sub * BAND_Q, (sub + 1) * BAND_Q
        take = lambda ref, a, b: ref[0, a:b] if axis == 0 else ref[0, :, a:b]
        parts = []
        if lo < BAND_PAST and not first:
            parts.append(take(prev_ref, lo, BAND_PAST))
        parts.append(take(cur_ref, max(lo - BAND_PAST, 0), hi))
        return parts[0] if len(parts) == 1 else jnp.concatenate(parts, axis=axis)

    def scores(sub, first):
        keys = window(kp_ref, kc_ref, sub, first, 0)
        q = q_ref[0, sub * BAND_Q:(sub + 1) * BAND_Q, :]
        bias = bias_ref[0, BAND_WIN - keys.shape[0]:, :]
        return lax.dot_general(keys, _split_maps(q), _NT, preferred_element_type=F32) + bias

    def finish(s, sub, first):
        p = jnp.exp2(s - jnp.max(s, axis=0, keepdims=True)).astype(BF16)
        o_t = _values_and_sum(window(vp_ref, vc_ref, sub, first, 1), p)
        o_t = o_t[:LANES] / o_t[LANES:LANES + 1]
        o = jnp.concatenate([o_t[:HEAD_DIM, :BAND_Q], o_t[HEAD_DIM:, BAND_Q:]], axis=0).T
        o_ref[0, sub * BAND_Q:(sub + 1) * BAND_Q, :] = o.astype(BF16)

    def run(first):
        nxt = scores(0, first)
        for sub in range(n_sub):
            cur_scores = nxt
            if sub + 1 < n_sub:
                nxt = scores(sub + 1, first)
            finish(cur_scores, sub, first)

    pl.when(qblk == 0)(functools.partial(run, True))
    pl.when(qblk > 0)(functools.partial(run, False))


def _band_attn(q, k, v_t, bias, layer):
    b, s, _ = q.shape
    bq = 2 * BAND_PAST if s % (2 * BAND_PAST) == 0 else BAND_PAST
    n_pairs = N_HEADS_A // 2
    prev_of = lambda qi: jnp.maximum(qi * (bq // BAND_PAST) - 1, 0)
    cur = lambda hp, bi, qi: (bi, qi, hp)
    prev = lambda hp, bi, qi: (bi, prev_of(qi), hp)
    cur_t = lambda hp, bi, qi: (bi, hp, qi)
    prev_t = lambda hp, bi, qi: (bi, hp, prev_of(qi))
    blk = (1, bq, LANES)
    return pl.pallas_call(
        _band_kernel,
        grid=(n_pairs, b, s // bq),
        in_specs=[pl.BlockSpec(blk, cur), pl.BlockSpec((1, BAND_PAST, LANES), prev), pl.BlockSpec(blk, cur),
                  pl.BlockSpec((1, LANES, BAND_PAST), prev_t), pl.BlockSpec((1, LANES, bq), cur_t),
                  pl.BlockSpec((1, BAND_WIN, 2 * BAND_Q), lambda hp, bi, qi: (layer * n_pairs + hp, 0, 0))],
        out_specs=pl.BlockSpec(blk, cur),
        out_shape=jax.ShapeDtypeStruct((b, s, WIDTH), BF16),
        compiler_params=_params("arbitrary", "arbitrary", "arbitrary"),
        name="band_attn",
    )(q, k, k, v_t, v_t, bias)


def _lambda(lam_ref, lam_init):
    lp = lam_ref[...]
    a1 = jnp.sum(lp[0:1] * lp[1:2], axis=-1, keepdims=True)
    a2 = jnp.sum(lp[2:3] * lp[3:4], axis=-1, keepdims=True)
    return jnp.exp(a1) - jnp.exp(a2) + lam_init


def _diff_kernel(lam_ref, g_ref, q_ref, k_ref, v_ref, o_ref, m_ref, acc_ref, *bufs, lam_init):
    qi = pl.program_id(2)
    q = q_ref[0]
    bq = q.shape[0]
    bk = bq // 2
    q2 = _split_maps(q)
    queries = (q2[:bq], q2[bq:])

    m_ref[...] = jnp.full(m_ref.shape, NEG, F32)
    acc_ref[...] = jnp.zeros(acc_ref.shape, F32)

    def block(ref, kj):
        return ref[0, pl.ds(pl.multiple_of(kj * bk, bk), bk), :]

    def scores(kj, stream, buf, diagonal=None):
        s_ref, mx_ref = bufs[4 * stream + 2 * buf], bufs[4 * stream + 2 * buf + 1]
        s = lax.dot_general(block(k_ref, kj), queries[stream], _NT, preferred_element_type=F32)
        if diagonal is not None:
            k_chunk = lax.broadcasted_iota(jnp.int32, s.shape, 0) // CHUNK + diagonal * (bk // CHUNK)
            q_chunk = lax.broadcasted_iota(jnp.int32, s.shape, 1) // CHUNK
            s = jnp.where(k_chunk <= q_chunk, s, NEG)
        s_ref[...] = s
        mx_ref[...] = jnp.max(s, axis=0, keepdims=True)

    def absorb(kj, stream, buf):
        s_ref, mx_ref = bufs[4 * stream + 2 * buf], bufs[4 * stream + 2 * buf + 1]
        cols = slice(stream * bq, (stream + 1) * bq)
        m_old = m_ref[:, cols]
        m_new = jnp.maximum(m_old, mx_ref[...])
        alpha = jnp.exp2(m_old - m_new)
        p = jnp.exp2(s_ref[...] - m_new).astype(BF16)
        acc_ref[:, cols] = alpha * acc_ref[:, cols] + _values_and_sum(v_ref[0, kj], p)
        m_ref[:, cols] = m_new

    def stage(kj, buf, look=True, look_diagonal=None):
        if look:
            scores(kj + 1, 0, 1 - buf, look_diagonal)
        absorb(kj, 1, buf)
        if look:
            scores(kj + 1, 1, 1 - buf, look_diagonal)
        absorb(kj, 0, buf)

    @pl.when(qi == 0)
    def _():
        scores(0, 0, 0, diagonal=0)
        scores(0, 1, 0, diagonal=0)
        stage(0, 0, look_diagonal=1)
        stage(1, 1, look=False)

    @pl.when(qi > 0)
    def _():
        scores(0, 0, 0)
        scores(0, 1, 0)

    def two_blocks(pair, carry):
        stage(2 * pair, 0)
        stage(2 * pair + 1, 1)
        return carry

    lax.fori_loop(0, jnp.maximum(qi - 1, 0), two_blocks, 0)

    @pl.when(qi > 0)
    def _():
        last = 2 * qi
        stage(last - 2, 0)
        stage(last - 1, 1, look_diagonal=0)
        stage(last, 0, look_diagonal=1)
        stage(last + 1, 1, look=False)

    o_t = acc_ref[:LANES] / acc_ref[LANES:LANES + 1]
    lam = _lambda(lam_ref, lam_init)
    d = (o_t[:, :bq] - lam * o_t[:, bq:]).T
    o_ref[0] = (_rms(d, g_ref[...]) * (1.0 - lam_init)).astype(BF16)


def _diff_attn(q, k, v_t, lam_p, g, lam_init):
    b, s, _ = q.shape
    bq = 2 * DIFF_KEYS
    assert s % bq == 0 and v_t.shape[3] == DIFF_KEYS
    seq = pl.BlockSpec((1, s, LANES), lambda bi, h, qi: (bi, 0, h))
    seq_t = pl.BlockSpec((1, s // DIFF_KEYS, LANES, DIFF_KEYS), lambda bi, h, qi: (bi, 0, h, 0))
    blk = pl.BlockSpec((1, bq, LANES), lambda bi, h, qi: (bi, qi, h))
    return pl.pallas_call(
        functools.partial(_diff_kernel, lam_init=lam_init),
        grid=(b, N_HEADS_B, s // bq),
        in_specs=[_resident(lam_p.shape), _resident(g.shape), blk, seq, seq_t],
        out_specs=blk,
        out_shape=jax.ShapeDtypeStruct((b, s, WIDTH), BF16),
        scratch_shapes=[pltpu.VMEM((1, 2 * bq), F32), pltpu.VMEM((LANES + SUM_ROWS, 2 * bq), F32),
                        ] + [pltpu.VMEM((DIFF_KEYS, bq), F32), pltpu.VMEM((1, bq), F32)] * 4,
        compiler_params=_params("arbitrary", "arbitrary", "arbitrary"),
        name="diff_attn",
    )(lam_p, g, q, k, v_t)


def _sample_kernel(lam_ref, g_ref, qa_ref, kan_ref, van_ref, qb_ref, kbn_ref, vbn_ref,
                   cak_ref, cav_ref, cbk_ref, cbv_ref, bias_c_ref, bias_n_ref,
                   oa_ref, ob_ref, *, lam_init):
    t = qa_ref.shape[0]
    n_grp = WIDTH // HEAD_DIM
    rows = n_grp * t
    grp_of_row = lax.broadcasted_iota(jnp.int32, (rows, WIDTH), 0) // t
    grp_of_lane = lax.broadcasted_iota(jnp.int32, (rows, WIDTH), 1) // HEAD_DIM
    own = grp_of_row == grp_of_lane
    lane_t = lax.broadcasted_iota(jnp.int32, (t, WIDTH), 1)

    def expand(q):
        qe = jnp.concatenate([q] * n_grp, axis=0)
        return jnp.where(own, qe, jnp.zeros_like(qe))

    def attend(qe, kt_cache, k_new, v_cache, v_new, bias_c, bias_n, v_dims):
        s_c = jnp.dot(qe, kt_cache, preferred_element_type=F32)
        s_n = lax.dot_general(qe, k_new, _NT, preferred_element_type=F32)
        if bias_c is not None:
            s_c = s_c + bias_c
            s_n = s_n + bias_n
        m = jnp.maximum(jnp.max(s_c, axis=-1, keepdims=True), jnp.max(s_n, axis=-1, keepdims=True))
        p_c = jnp.exp2(s_c - m)
        p_n = jnp.exp2(s_n - m)
        l = jnp.sum(p_c, axis=-1, keepdims=True) + jnp.sum(p_n, axis=-1, keepdims=True)
        o = (lax.dot_general(p_c.astype(BF16), v_cache, v_dims, preferred_element_type=F32)
             + jnp.dot(p_n.astype(BF16), v_new, preferred_element_type=F32))
        return o / l

    oa_all = attend(expand(qa_ref[...]), cak_ref[0, 0].astype(BF16), kan_ref[...],
                    cav_ref[0, 0].astype(BF16), van_ref[...], bias_c_ref[...], bias_n_ref[...], _NT)
    oa = jnp.zeros((t, WIDTH), F32)
    for h in range(N_HEADS_A):
        oa = jnp.where(lane_t // HEAD_DIM == h, oa_all[h * t:(h + 1) * t], oa)
    oa_ref[...] = oa.astype(BF16)

    past = cbk_ref.shape[3]
    v_cache = jnp.concatenate([cbv_ref[pl.ds(h, past, stride=N_HEADS_B), :] for h in range(N_HEADS_B)],
                              axis=1).astype(BF16)
    ob_all = attend(expand(qb_ref[...]), cbk_ref[0, 0].astype(BF16), kbn_ref[...],
                    v_cache, vbn_ref[...], None, None, _NN)
    lam = _lambda(lam_ref, lam_init)
    g = g_ref[...]
    outs = []
    for h in range(N_HEADS_B):
        d = ob_all[2 * h * t:(2 * h + 1) * t] - lam * ob_all[(2 * h + 1) * t:(2 * h + 2) * t]
        outs.append(_rms(d[:, h * LANES:(h + 1) * LANES], g) * (1.0 - lam_init))
    ob_ref[...] = jnp.concatenate(outs, axis=1).astype(BF16)


def _sample_attn(qa, ka, va, qb, kb, vb, cak, cav, cbk, cbv, bias_c, bias_n, lam_p, g, lam_init, t, layer):
    m = qa.shape[0]
    bs = m // t
    past = cbk.shape[3]
    new = pl.BlockSpec((t, WIDTH), lambda i: (i, 0))
    cache = lambda c: pl.BlockSpec((1, 1) + c.shape[2:], lambda i: (layer, i, 0, 0))
    cbv_spec = pl.BlockSpec((past * N_HEADS_B, LANES), lambda i: (layer * bs + i, 0))
    out = jax.ShapeDtypeStruct((m, WIDTH), BF16)
    return pl.pallas_call(
        functools.partial(_sample_kernel, lam_init=lam_init),
        grid=(bs,),
        in_specs=[_resident(lam_p.shape), _resident(g.shape)] + [new] * 6
                 + [cache(cak), cache(cav), cache(cbk), cbv_spec,
                    _resident(bias_c.shape), _resident(bias_n.shape)],
        out_specs=[new, new],
        out_shape=[out, out],
        compiler_params=_params("arbitrary"),
        name="sample_attn",
    )(lam_p, g, qa, ka, va, qb, kb, vb, cak, cav, cbk, cbv, bias_c, bias_n)


def _conv_kernel(x_ref, g_ref, w_ref, cw_ref, st_ref, y_ref, tail_ref, carry_ref, *, tps, seg_len):
    tm, d = x_ref.shape
    h = _rms(x_ref[...], g_ref[...]).astype(BF16)
    carried = seg_len >= tm
    if carried:
        @pl.when(pl.program_id(0) % tps == 0)
        def _():
            carry_ref[...] = st_ref[0]

    cw = cw_ref[...]
    cc = 2 * FF_CHUNK
    for c in range(d // cc):
        cols = slice(c * cc, (c + 1) * cc)
        proj = lambda part: jnp.dot(h, w_ref[:, part * d + c * cc:part * d + (c + 1) * cc],
                                    preferred_element_type=F32)
        xin = proj(1) * proj(2)
        r = lax.broadcasted_iota(jnp.int32, xin.shape, 0)
        x1 = pltpu.roll(xin, 1, 0)
        x2 = pltpu.roll(xin, 2, 0)
        if carried:
            prev = carry_ref[:, cols]
            x1 = jnp.where(r == 0, prev[7:8], x1)
            x2 = jnp.where(r == 0, prev[6:7], jnp.where(r == 1, prev[7:8], x2))
            carry_ref[:, cols] = xin[tm - 8:]
            tail_ref[0, :, cols] = xin[tm - 8:]
        else:
            t = r % seg_len
            x1 = jnp.where(t >= 1, x1, st_ref[0, :, cols])
            x2 = jnp.where(t >= 2, x2, st_ref[1, :, cols])
            tail_ref[:, cols] = xin
        conv = cw[0:1, cols] * x2 + cw[1:2, cols] * x1 + cw[2:3, cols] * xin
        y_ref[:, cols] = (proj(0) * conv).astype(BF16)


def _conv_in(x, g, w, cw, state, *, seg_len):
    m, d = x.shape
    tm = min(ROW_TILE, m)
    row = lambda i: (i, 0)
    if seg_len >= tm:
        tps = seg_len // tm
        st_spec = pl.BlockSpec((1, 8, d), lambda i: (i // tps, 0, 0))
        tail_spec = pl.BlockSpec((1, 8, d), lambda i: (i // tps, 0, 0))
        tail_shape = jax.ShapeDtypeStruct((m // seg_len, 8, d), F32)
    else:
        tps = 1
        st_spec = pl.BlockSpec((2, tm, d), lambda i: (0, i, 0))
        tail_spec = pl.BlockSpec((tm, d), row)
        tail_shape = jax.ShapeDtypeStruct((m, d), F32)
    return pl.pallas_call(
        functools.partial(_conv_kernel, tps=tps, seg_len=seg_len),
        grid=(m // tm,),
        in_specs=[pl.BlockSpec((tm, d), row), _resident((1, d)), _resident(w.shape),
                  _resident(cw.shape), st_spec],
        out_specs=[pl.BlockSpec((tm, d), row), tail_spec],
        out_shape=[jax.ShapeDtypeStruct((m, d), BF16), tail_shape],
        scratch_shapes=[pltpu.VMEM((8, d), F32)],
        compiler_params=_params("arbitrary"),
        name="conv_in",
    )(x, g, w, cw, state)


def _ffn_kernel(*refs, n_parts, final):
    x_ref = refs[0]
    a_refs = refs[1:1 + n_parts]
    wo_ref, g_ref, wg_ref, wu_ref, wd_ref = refs[1 + n_parts:6 + n_parts]
    gf_ref = refs[6 + n_parts] if final else None
    o_ref, act_ref = refs[-2], refs[-1]

    a = a_refs[0][...] if n_parts == 1 else jnp.concatenate([r[...] for r in a_refs], axis=1)
    x1 = x_ref[...] + jnp.dot(a, wo_ref[...], preferred_element_type=F32)
    h = _rms(x1, g_ref[...]).astype(BF16)
    d_ff = wg_ref.shape[1]
    for c in range(d_ff // FF_CHUNK):
        cols = slice(c * FF_CHUNK, (c + 1) * FF_CHUNK)
        gate = jnp.dot(h, wg_ref[:, cols], preferred_element_type=F32)
        up = jnp.dot(h, wu_ref[:, cols], preferred_element_type=F32)
        act_ref[:, cols] = (gate * (1.0 / (1.0 + jnp.exp(-gate))) * up).astype(BF16)
    y = x1 + jnp.dot(act_ref[...], wd_ref[...], preferred_element_type=F32)
    o_ref[...] = _rms(y, gf_ref[...]) if final else y


def _mix_out_ffn(x, parts, wo, g, wg, wu, wd, g_final=None):
    m, d = x.shape
    tm = min(ROW_TILE, m)
    row = lambda i: (i, 0)
    final = g_final is not None
    extra = [g_final] if final else []
    return pl.pallas_call(
        functools.partial(_ffn_kernel, n_parts=len(parts), final=final),
        grid=(m // tm,),
        in_specs=[pl.BlockSpec((tm, d), row)] + [pl.BlockSpec((tm, p.shape[1]), row) for p in parts]
                 + [_resident(wo.shape), _resident((1, d)), _resident(wg.shape), _resident(wu.shape),
                    _resident(wd.shape)] + [_resident((1, d))] * len(extra),
        out_specs=pl.BlockSpec((tm, d), row),
        out_shape=jax.ShapeDtypeStruct((m, d), F32),
        scratch_shapes=[pltpu.VMEM((tm, wg.shape[1]), BF16)],
        compiler_params=_params("arbitrary"),
        name="mix_out_ffn",
    )(x, *parts, wo, g, wg, wu, wd, *extra)


def kernel(x_prompt, x_sample, cache_a_k, cache_a_v, cache_b_k, cache_b_v, state_conv, norm_mix, norm_ffn, norm_final, w_attn_in, w_attn_out, rel_bias, lambda_q1, lambda_k1, lambda_q2, lambda_k2, subln_g, w_conv_in, conv_w, w_conv_out, w_ffn_gate, w_ffn_up, w_ffn_down):
    b, s, d = x_prompt.shape
    bs, t, _ = x_sample.shape
    depth = norm_mix.shape[0]
    n_attn = w_attn_in.shape[0]
    n_conv = w_conv_in.shape[0]
    past = cache_b_k.shape[2]
    a_past = cache_a_k.shape[2]
    keep = min(BAND_PAST, s)
    assert s % BAND_PAST == 0 and a_past == BAND_PAST and d % (2 * FF_CHUNK) == 0
    assert w_ffn_gate.shape[2] % FF_CHUNK == 0 and 2 <= t and a_past + t <= BIAS_LANES

    xp = x_prompt.reshape(b * s, d)
    xs = x_sample.reshape(bs * t, d)
    row = lambda v: v.reshape(1, -1)

    cos_p, sin_p = _rope_tables(jnp.arange(s))
    cos_s, sin_s = (jnp.tile(tab, (bs, 1)) for tab in _rope_tables(past + jnp.arange(t)))

    table = rel_bias.reshape(n_attn * N_HEADS_A, 2 * MAX_REL + 1)
    band_bias = _band_bias(table)
    sample_bias = _sample_bias(table, t)

    cak_t = cache_a_k.transpose(0, 1, 3, 4, 2).reshape(n_attn, bs, WIDTH, a_past)
    cav_t = cache_a_v.transpose(0, 1, 3, 4, 2).reshape(n_attn, bs, WIDTH, a_past)
    cbk_t = cache_b_k.transpose(0, 1, 3, 4, 5, 2).reshape(n_attn, bs, WIDTH, past)
    cbv_rows = cache_b_v.reshape(n_attn * bs * past * N_HEADS_B, LANES)

    stacked_p = stacked_s = None
    pcs, scs = [], []
    for i in range(depth):
        j = i // 2
        g_mix = row(norm_mix[i])
        if i % 2 == 0:
            lam_init = 0.8 - 0.6 * math.exp(-0.3 * i)
            w_in = w_attn_in[j].astype(BF16)
            wo = w_attn_out[j].astype(BF16)
            lam_p = jnp.stack([lambda_q1[j], lambda_k1[j], lambda_q2[j], lambda_k2[j]])
            g_sub = row(subln_g[j])

            outs = _attn_in_proj(xp, g_mix, w_in, cos_p, sin_p, stacked_p,
                                 seg_len=s, keep=keep, layer=j, n_layers=n_attn, native=True)
            qa, ka, va_t, qb, kb, vb_t = outs[:6]
            qa, ka, qb, kb = (o.reshape(b, s, WIDTH) for o in (qa, ka, qb, kb))
            stacked_p = outs[6:]
            oa = _band_attn(qa, ka, va_t, band_bias, j)
            ob = _diff_attn(qb, kb, vb_t, lam_p, g_sub, lam_init)
            parts_p = [oa.reshape(b * s, WIDTH), ob.reshape(b * s, WIDTH)]

            outs = _attn_in_proj(xs, g_mix, w_in, cos_s, sin_s, stacked_s,
                                 seg_len=t, keep=t, layer=j, n_layers=n_attn, native=False)
            stacked_s = outs[6:]
            oa, ob = _sample_attn(
                *outs[:6], cak_t, cav_t, cbk_t, cbv_rows,
                sample_bias[j, :, :a_past], sample_bias[j, :, a_past:a_past + t],
                lam_p, g_sub, lam_init, t, j)
            parts_s = [oa, ob]
        else:
            w_in = w_conv_in[j].astype(BF16)
            wo = w_conv_out[j].astype(BF16)
            yp, tail = _conv_in(xp, g_mix, w_in, conv_w[j], jnp.zeros((b, 8, d), F32), seg_len=s)
            pcs.append(tail[:, 6:])
            st = state_conv[j]
            stand_in = jnp.stack([jnp.pad(st[:, 1:2], ((0, 0), (0, t - 1), (0, 0))),
                                  jnp.pad(st, ((0, 0), (0, t - 2), (0, 0)))]).reshape(2, bs * t, d)
            ys, xin = _conv_in(xs, g_mix, w_in, conv_w[j], stand_in, seg_len=t)
            scs.append(xin.reshape(bs, t, d)[:, t - 2:])
            parts_p, parts_s = [yp], [ys]

        ffn = (wo, row(norm_ffn[i]), w_ffn_gate[i].astype(BF16), w_ffn_up[i].astype(BF16),
               w_ffn_down[i].astype(BF16))
        g_final = row(norm_final) if i == depth - 1 else None
        xp = _mix_out_ffn(xp, parts_p, *ffn, g_final=g_final)
        xs = _mix_out_ffn(xs, parts_s, *ffn, g_final=g_final)

    pak, pav, pbk, pbv = stacked_p
    sak, sav, sbk, sbv = stacked_s
    heads_a = lambda z: z.reshape(n_attn, b, N_HEADS_A, HEAD_DIM, keep).transpose(0, 1, 4, 2, 3)
    return (xp.reshape(b, s, d), xs.reshape(bs, t, d),
            heads_a(pak), heads_a(pav),
            pbk.reshape(n_attn, b, N_HEADS_B, 2, HEAD_DIM, s).transpose(0, 1, 5, 2, 3, 4),
            pbv.reshape(n_attn, b, s, N_HEADS_B, 2 * HEAD_DIM),
            jnp.stack(pcs),
            sak.reshape(n_attn, bs, t, N_HEADS_A, HEAD_DIM), sav.reshape(n_attn, bs, t, N_HEADS_A, HEAD_DIM),
            sbk.reshape(n_attn, bs, t, N_HEADS_B, 2, HEAD_DIM), sbv.reshape(n_attn, bs, t, N_HEADS_B, 2 * HEAD_DIM),
            jnp.stack(scs))
```

```python
import functools
import math

import jax
import jax.numpy as jnp
from jax import lax
from jax.experimental import pallas as pl
from jax.experimental.pallas import tpu as pltpu

F32 = jnp.float32
BF16 = jnp.bfloat16

EPS = 1e-6
NEG = -1e30
LOG2E = 1.4426950408889634
ROPE_THETA = 10000.0
CHUNK = 64
HEAD_DIM = 64
N_HEADS_A = 8
N_HEADS_B = 4
BAND_PAST = 512
MAX_REL = 128
BAND_KEYS = BAND_PAST + CHUNK
BAND_Q = 256
BAND_WIN = BAND_Q + BAND_PAST
DIFF_KEYS = 512
BIAS_LANES = 640
WIDTH = N_HEADS_A * HEAD_DIM
LANES = 128
ROW_TILE = 512
FF_CHUNK = 256
VMEM_LIMIT = 56 * 1024 * 1024

_NT = (((1,), (1,)), ((), ()))
_TN = (((0,), (0,)), ((), ()))
_NN = (((1,), (0,)), ((), ()))


def _resident(shape):
    return pl.BlockSpec(shape, lambda *_: (0,) * len(shape), pipeline_mode=pl.Buffered(1))


def _rms(x, g):
    return x * lax.rsqrt(jnp.mean(x * x, axis=-1, keepdims=True) + EPS) * g


def _params(*sem):
    return pltpu.CompilerParams(dimension_semantics=sem, vmem_limit_bytes=VMEM_LIMIT)


SUM_ROWS = 16


def _values_and_sum(v_t, p):
    ones = jnp.ones((SUM_ROWS, v_t.shape[1]), v_t.dtype)
    return jnp.dot(jnp.concatenate([v_t, ones], axis=0), p, preferred_element_type=F32)


def _split_maps(q):
    low = lax.broadcasted_iota(jnp.int32, q.shape, 1) < HEAD_DIM
    zero = jnp.zeros_like(q)
    return jnp.concatenate([jnp.where(low, q, zero), jnp.where(low, zero, q)], axis=0)


def _attn_in_kernel(*refs, native):
    x_ref, g_ref, w_ref, cos_ref, sin_ref = refs[:5]
    (qa_ref, ka_ref, va_ref, qb_ref, kb_ref, vb_ref,
     kaf_ref, vaf_ref, kbf_ref, vbf_ref) = refs[-10:]
    q_scale = LOG2E / math.sqrt(HEAD_DIM)
    tm = x_ref.shape[0]
    halves = [(0, tm // 2), (tm // 2, tm)] if tm % 32 == 0 else [(0, tm)]
    normed = [_rms(x_ref[lo:hi, :], g_ref[...]).astype(BF16) for lo, hi in halves]

    for (lo, hi), h in zip(halves, normed):
        def proj(c):
            return jnp.dot(h, w_ref[:, c * WIDTH:(c + 1) * WIDTH], preferred_element_type=F32)

        cos = cos_ref[lo:hi, :]
        sin = sin_ref[lo:hi, :]
        low_half = (lax.broadcasted_iota(jnp.int32, cos.shape, 1) % HEAD_DIM) < (HEAD_DIM // 2)

        def rotary(z):
            parts = []
            for j in range(WIDTH // LANES):
                zj = z[:, j * LANES:(j + 1) * LANES]
                swapped = jnp.where(low_half, pltpu.roll(zj, LANES - HEAD_DIM // 2, 1),
                                    pltpu.roll(zj, HEAD_DIM // 2, 1))
                parts.append(zj * cos + swapped * sin)
            return jnp.concatenate(parts, axis=1)

        vb = proj(5)
        if native:
            vb_ref[0, 0, :, lo:hi] = vb.T.astype(BF16)
            for head in range(N_HEADS_B):
                vbf_ref[pl.ds(lo * N_HEADS_B + head, hi - lo, stride=N_HEADS_B), :] = (
                    vb[:, head * LANES:(head + 1) * LANES])
        else:
            vb_ref[lo:hi, :] = vb.astype(BF16)
            vbf_ref[lo:hi, :] = vb
        kb = rotary(proj(4))
        kb_ref[lo:hi, :] = kb.astype(BF16)
        va = proj(2)
        ka = proj(1)
        ka_ref[lo:hi, :] = ka.astype(BF16)
        if native:
            kbf_ref[0, 0, :, lo:hi] = kb.T
            va_t = va.T
            vaf_ref[0, 0, :, lo:hi] = va_t
            va_ref[0, :, lo:hi] = va_t.astype(BF16)
            kaf_ref[0, 0, :, lo:hi] = ka.T
        else:
            kbf_ref[lo:hi, :] = kb
            vaf_ref[lo:hi, :] = va
            va_ref[lo:hi, :] = va.astype(BF16)
            kaf_ref[lo:hi, :] = ka
        qb_ref[lo:hi, :] = (rotary(proj(3)) * q_scale).astype(BF16)
        qa_ref[lo:hi, :] = (proj(0) * q_scale).astype(BF16)


def _attn_in_proj(x, g, w, cos_t, sin_t, stacked, *, seg_len, keep, layer, n_layers, native):
    m, d = x.shape
    tm = min(ROW_TILE, m)
    n_tiles = m // tm
    n_tab = cos_t.shape[0] // tm
    n_seg = m // seg_len
    row = lambda i: (i, 0)
    tab = lambda i: (i % n_tab, 0)
    blk = pl.BlockSpec((tm, WIDTH), row)
    bf_out = jax.ShapeDtypeStruct((m, WIDTH), BF16)
    if native:
        tps, kpt = seg_len // tm, keep // tm
        t_blk = (1, 1, WIDTH, tm)
        keep_spec = pl.BlockSpec(
            t_blk, lambda i: (layer, i // tps, 0, jnp.clip(i % tps - (tps - kpt), 0, kpt - 1)))
        keep_out = jax.ShapeDtypeStruct((n_layers, n_seg, WIDTH, keep), F32)
        kb_spec = pl.BlockSpec(t_blk, lambda i: (layer, i // tps, 0, i % tps))
        kb_out = jax.ShapeDtypeStruct((n_layers, n_seg, WIDTH, seg_len), F32)
        vb_spec = pl.BlockSpec((tm * N_HEADS_B, LANES), lambda i: (layer * n_tiles + i, 0))
        vb_out = jax.ShapeDtypeStruct((n_layers * m * N_HEADS_B, LANES), F32)
        va_spec = pl.BlockSpec((1, WIDTH, tm), lambda i: (i // tps, 0, i % tps))
        va_out = jax.ShapeDtypeStruct((n_seg, WIDTH, seg_len), BF16)
        vbb_spec = pl.BlockSpec((1, 1, WIDTH, tm), lambda i: (i // tps, i % tps, 0, 0))
        vbb_out = jax.ShapeDtypeStruct((n_seg, tps, WIDTH, tm), BF16)
    else:
        assert keep == seg_len
        keep_spec = kb_spec = vb_spec = pl.BlockSpec((tm, WIDTH), lambda i: (layer * n_tiles + i, 0))
        keep_out = kb_out = vb_out = jax.ShapeDtypeStruct((n_layers * m, WIDTH), F32)
        va_spec = vbb_spec = blk
        va_out = vbb_out = bf_out
    carried = [] if stacked is None else list(stacked)
    n_in = 5
    return pl.pallas_call(
        functools.partial(_attn_in_kernel, native=native),
        grid=(n_tiles,),
        in_specs=[pl.BlockSpec((tm, d), row), _resident((1, d)), _resident(w.shape),
                  pl.BlockSpec((tm, LANES), tab), pl.BlockSpec((tm, LANES), tab)]
                 + [pl.BlockSpec(memory_space=pl.ANY)] * len(carried),
        out_specs=[blk, blk, va_spec, blk, blk, vbb_spec, keep_spec, keep_spec, kb_spec, vb_spec],
        out_shape=[bf_out, bf_out, va_out, bf_out, bf_out, vbb_out, keep_out, keep_out, kb_out, vb_out],
        input_output_aliases={n_in + k: 6 + k for k in range(len(carried))},
        compiler_params=_params("arbitrary"),
        name="attn_in_proj",
    )(x, g, w, cos_t, sin_t, *carried)


def _rope_tables(pos):
    half = HEAD_DIM // 2
    inv = ROPE_THETA ** (-jnp.arange(half, dtype=F32) / half)
    ang = pos.astype(F32)[:, None] * inv[None, :]
    c, s = jnp.cos(ang), jnp.sin(ang)
    reps = LANES // HEAD_DIM
    return jnp.tile(c, (1, 2 * reps)), jnp.tile(jnp.concatenate([-s, s], axis=1), (1, reps))


def _band_bias_kernel(tab_ref, o_ref):
    r = pl.program_id(0)
    e_shape = (BAND_WIN + LANES, LANES)
    y = lax.broadcasted_iota(jnp.int32, e_shape, 0) - LANES
    lane = lax.broadcasted_iota(jnp.int32, e_shape, 1)
    idx = jnp.clip(BAND_PAST + lane - y, -MAX_REL, MAX_REL) + MAX_REL
    cols = []
    for hh in range(2):
        def body(d, acc, hh=hh):
            return jnp.where(idx == d, tab_ref[2 * r + hh, d], acc)

        e = lax.fori_loop(MAX_REL - CHUNK + 1, 2 * MAX_REL + 1, body, jnp.zeros(e_shape, F32))
        for c in range(BAND_Q // LANES):
            cols.append(e[LANES - LANES * c:LANES - LANES * c + BAND_WIN])
    bias = jnp.concatenate(cols, axis=1)

    kj = lax.broadcasted_iota(jnp.int32, bias.shape, 0)
    qi = lax.broadcasted_iota(jnp.int32, bias.shape, 1) % BAND_Q
    first_key = (qi // CHUNK) * CHUNK
    in_band = jnp.logical_and(kj >= first_key, kj < first_key + BAND_KEYS)
    o_ref[0] = jnp.where(in_band, bias * LOG2E, NEG)


def _band_bias(table):
    n = table.shape[0] // 2
    return pl.pallas_call(
        _band_bias_kernel,
        grid=(n,),
        in_specs=[pl.BlockSpec(memory_space=pltpu.SMEM)],
        out_specs=pl.BlockSpec((1, BAND_WIN, 2 * BAND_Q), lambda r: (r, 0, 0)),
        out_shape=jax.ShapeDtypeStruct((n, BAND_WIN, 2 * BAND_Q), F32),
        compiler_params=_params("arbitrary"),
        name="band_bias",
    )(table)


def _sample_bias_kernel(tab_ref, o_ref, *, t):
    r = pl.program_id(0)
    shape = (N_HEADS_A * t, BIAS_LANES)
    row = lax.broadcasted_iota(jnp.int32, shape, 0)
    kj = lax.broadcasted_iota(jnp.int32, shape, 1)
    idx = jnp.clip(BAND_PAST + row % t - kj, -MAX_REL, MAX_REL) + MAX_REL
    head = lax.broadcasted_iota(jnp.int32, (N_HEADS_A * t, 1), 0) // t

    def body(d, acc):
        col = jnp.zeros(head.shape, F32)
        for h in range(N_HEADS_A):
            col = jnp.where(head == h, tab_ref[N_HEADS_A * r + h, d], col)
        return jnp.where(idx == d, col, acc)

    o_ref[0] = lax.fori_loop(0, 2 * MAX_REL + 1, body, jnp.zeros(shape, F32)) * LOG2E


def _sample_bias(table, t):
    n = table.shape[0] // N_HEADS_A
    return pl.pallas_call(
        functools.partial(_sample_bias_kernel, t=t),
        grid=(n,),
        in_specs=[pl.BlockSpec(memory_space=pltpu.SMEM)],
        out_specs=pl.BlockSpec((1, N_HEADS_A * t, BIAS_LANES), lambda r: (r, 0, 0)),
        out_shape=jax.ShapeDtypeStruct((n, N_HEADS_A * t, BIAS_LANES), F32),
        compiler_params=_params("arbitrary"),
        name="sample_bias",
    )(table)


def _band_kernel(q_ref, kp_ref, kc_ref, vp_ref, vc_ref, bias_ref, o_ref):
    qblk = pl.program_id(2)
    n_sub = q_ref.shape[1] // BAND_Q

    def window(prev_ref, cur_ref, sub, first, axis):
        lo, hi = sub * BAND_Q, (sub + 1) * BAND_Q
        take = lambda ref, a, b: ref[0, a:b] if axis == 0 else ref[0, :, a:b]
        parts = []
        if lo < BAND_PAST and not first:
            parts.append(take(prev_ref, lo, BAND_PAST))
        parts.append(take(cur_ref, max(lo - BAND_PAST, 0), hi))
        return parts[0] if len(parts) == 1 else jnp.concatenate(parts, axis=axis)

    def scores(sub, first):
        keys = window(kp_ref, kc_ref, sub, first, 0)
        q = q_ref[0, sub * BAND_Q:(sub + 1) * BAND_Q, :]
        bias = bias_ref[0, BAND_WIN - keys.shape[0]:, :]
        return lax.dot_general(keys, _split_maps(q), _NT, preferred_element_type=F32) + bias

    def finish(s, sub, first):
        p = jnp.exp2(s - jnp.max(s, axis=0, keepdims=True)).astype(BF16)
        o_t = _values_and_sum(window(vp_ref, vc_ref, sub, first, 1), p)
        o_t = o_t[:LANES] / o_t[LANES:LANES + 1]
        o = jnp.concatenate([o_t[:HEAD_DIM, :BAND_Q], o_t[HEAD_DIM:, BAND_Q:]], axis=0).T
        o_ref[0, sub * BAND_Q:(sub + 1) * BAND_Q, :] = o.astype(BF16)

    def run(first):
        nxt = scores(0, first)
        for sub in range(n_sub):
            cur_scores = nxt
            if sub + 1 < n_sub:
                nxt = scores(sub + 1, first)
            finish(cur_scores, sub, first)

    pl.when(qblk == 0)(functools.partial(run, True))
    pl.when(qblk > 0)(functools.partial(run, False))


def _band_attn(q, k, v_t, bias, layer):
    b, s, _ = q.shape
    bq = 2 * BAND_PAST if s % (2 * BAND_PAST) == 0 else BAND_PAST
    n_pairs = N_HEADS_A // 2
    prev_of = lambda qi: jnp.maximum(qi * (bq // BAND_PAST) - 1, 0)
    cur = lambda hp, bi, qi: (bi, qi, hp)
    prev = lambda hp, bi, qi: (bi, prev_of(qi), hp)
    cur_t = lambda hp, bi, qi: (bi, hp, qi)
    prev_t = lambda hp, bi, qi: (bi, hp, prev_of(qi))
    blk = (1, bq, LANES)
    return pl.pallas_call(
        _band_kernel,
        grid=(n_pairs, b, s // bq),
        in_specs=[pl.BlockSpec(blk, cur), pl.BlockSpec((1, BAND_PAST, LANES), prev), pl.BlockSpec(blk, cur),
                  pl.BlockSpec((1, LANES, BAND_PAST), prev_t), pl.BlockSpec((1, LANES, bq), cur_t),
                  pl.BlockSpec((1, BAND_WIN, 2 * BAND_Q), lambda hp, bi, qi: (layer * n_pairs + hp, 0, 0))],
        out_specs=pl.BlockSpec(blk, cur),
        out_shape=jax.ShapeDtypeStruct((b, s, WIDTH), BF16),
        compiler_params=_params("arbitrary", "arbitrary", "arbitrary"),
        name="band_attn",
    )(q, k, k, v_t, v_t, bias)


def _lambda(lam_ref, lam_init):
    lp = lam_ref[...]
    a1 = jnp.sum(lp[0:1] * lp[1:2], axis=-1, keepdims=True)
    a2 = jnp.sum(lp[2:3] * lp[3:4], axis=-1, keepdims=True)
    return jnp.exp(a1) - jnp.exp(a2) + lam_init


def _diff_kernel(lam_ref, g_ref, q_ref, k_ref, v_ref, o_ref, m_ref, acc_ref, *bufs, lam_init):
    qi = pl.program_id(2)
    q = q_ref[0]
    bq = q.shape[0]
    bk = bq // 2
    q2 = _split_maps(q)
    queries = (q2[:bq], q2[bq:])

    m_ref[...] = jnp.full(m_ref.shape, NEG, F32)
    acc_ref[...] = jnp.zeros(acc_ref.shape, F32)

    def block(ref, kj):
        return ref[0, pl.ds(pl.multiple_of(kj * bk, bk), bk), :]

    def scores(kj, stream, buf, diagonal=None):
        s_ref, mx_ref = bufs[4 * stream + 2 * buf], bufs[4 * stream + 2 * buf + 1]
        qs = queries[stream][bk:] if diagonal == 1 else queries[stream]
        s = lax.dot_general(block(k_ref, kj), qs, _NT, preferred_element_type=F32)
        if diagonal is not None:
            k_chunk = lax.broadcasted_iota(jnp.int32, s.shape, 0) // CHUNK
            q_chunk = lax.broadcasted_iota(jnp.int32, s.shape, 1) // CHUNK
            s = jnp.where(k_chunk <= q_chunk, s, NEG)
        n = s.shape[1]
        s_ref[:, :n] = s
        mx_ref[:, :n] = jnp.max(s, axis=0, keepdims=True)

    def absorb(kj, stream, buf, second_half=False):
        s_ref, mx_ref = bufs[4 * stream + 2 * buf], bufs[4 * stream + 2 * buf + 1]
        n = bk if second_half else bq
        cols = slice((stream + 1) * bq - n, (stream + 1) * bq)
        m_old = m_ref[:, cols]
        m_new = jnp.maximum(m_old, mx_ref[:, :n])
        alpha = jnp.exp2(m_old - m_new)
        p = jnp.exp2(s_ref[:, :n] - m_new).astype(BF16)
        acc_ref[:, cols] = alpha * acc_ref[:, cols] + _values_and_sum(v_ref[0, kj], p)
        m_ref[:, cols] = m_new

    def stage(kj, buf, look=True, look_diagonal=None, second_half=False):
        if look:
            scores(kj + 1, 0, 1 - buf, look_diagonal)
        absorb(kj, 1, buf, second_half)
        if look:
            scores(kj + 1, 1, 1 - buf, look_diagonal)
        absorb(kj, 0, buf, second_half)

    @pl.when(qi == 0)
    def _():
        scores(0, 0, 0, diagonal=0)
        scores(0, 1, 0, diagonal=0)
        stage(0, 0, look_diagonal=1)
        stage(1, 1, look=False, second_half=True)

    @pl.when(qi > 0)
    def _():
        scores(0, 0, 0)
        scores(0, 1, 0)

    def two_blocks(pair, carry):
        stage(2 * pair, 0)
        stage(2 * pair + 1, 1)
        return carry

    lax.fori_loop(0, jnp.maximum(qi - 1, 0), two_blocks, 0)

    @pl.when(qi > 0)
    def _():
        last = 2 * qi
        stage(last - 2, 0)
        stage(last - 1, 1, look_diagonal=0)
        stage(last, 0, look_diagonal=1)
        stage(last + 1, 1, look=False, second_half=True)

    o_t = acc_ref[:LANES] / acc_ref[LANES:LANES + 1]
    lam = _lambda(lam_ref, lam_init)
    d_t = o_t[:, :bq] - lam * o_t[:, bq:]
    inv = lax.rsqrt(jnp.mean(d_t * d_t, axis=0, keepdims=True) + EPS)
    o_ref[0] = ((d_t * inv).T * (g_ref[...] * (1.0 - lam_init))).astype(BF16)


def _diff_attn(q, k, v_t, lam_p, g, lam_init):
    b, s, _ = q.shape
    bq = 2 * DIFF_KEYS
    assert s % bq == 0 and v_t.shape[3] == DIFF_KEYS
    seq = pl.BlockSpec((1, s, LANES), lambda bi, h, qi: (bi, 0, h))
    seq_t = pl.BlockSpec((1, s // DIFF_KEYS, LANES, DIFF_KEYS), lambda bi, h, qi: (bi, 0, h, 0))
    blk = pl.BlockSpec((1, bq, LANES), lambda bi, h, qi: (bi, qi, h))
    return pl.pallas_call(
        functools.partial(_diff_kernel, lam_init=lam_init),
        grid=(b, N_HEADS_B, s // bq),
        in_specs=[_resident(lam_p.shape), _resident(g.shape), blk, seq, seq_t],
        out_specs=blk,
        out_shape=jax.ShapeDtypeStruct((b, s, WIDTH), BF16),
        scratch_shapes=[pltpu.VMEM((1, 2 * bq), F32), pltpu.VMEM((LANES + SUM_ROWS, 2 * bq), F32),
                        ] + [pltpu.VMEM((DIFF_KEYS, bq), F32), pltpu.VMEM((1, bq), F32)] * 4,
        compiler_params=_params("arbitrary", "arbitrary", "arbitrary"),
        name="diff_attn",
    )(lam_p, g, q, k, v_t)


def _sample_kernel(lam_ref, g_ref, qa_ref, kan_ref, van_ref, qb_ref, kbn_ref, vbn_ref,
                   cak_ref, cav_ref, cbk_ref, cbv_ref, bias_c_ref, bias_n_ref,
                   oa_ref, ob_ref, *, lam_init):
    t = qa_ref.shape[0]
    n_grp = WIDTH // HEAD_DIM
    rows = n_grp * t
    grp_of_row = lax.broadcasted_iota(jnp.int32, (rows, WIDTH), 0) // t
    grp_of_lane = lax.broadcasted_iota(jnp.int32, (rows, WIDTH), 1) // HEAD_DIM
    own = grp_of_row == grp_of_lane
    lane_t = lax.broadcasted_iota(jnp.int32, (t, WIDTH), 1)

    def expand(q):
        qe = jnp.concatenate([q] * n_grp, axis=0)
        return jnp.where(own, qe, jnp.zeros_like(qe))

    def attend(qe, kt_cache, k_new, v_cache, v_new, bias_c, bias_n, v_dims):
        s_c = jnp.dot(qe, kt_cache, preferred_element_type=F32)
        s_n = lax.dot_general(qe, k_new, _NT, preferred_element_type=F32)
        if bias_c is not None:
            s_c = s_c + bias_c
            s_n = s_n + bias_n
        m = jnp.maximum(jnp.max(s_c, axis=-1, keepdims=True), jnp.max(s_n, axis=-1, keepdims=True))
        p_c = jnp.exp2(s_c - m)
        p_n = jnp.exp2(s_n - m)
        l = jnp.sum(p_c, axis=-1, keepdims=True) + jnp.sum(p_n, axis=-1, keepdims=True)
        o = (lax.dot_general(p_c.astype(BF16), v_cache, v_dims, preferred_element_type=F32)
             + jnp.dot(p_n.astype(BF16), v_new, preferred_element_type=F32))
        return o / l

    oa_all = attend(expand(qa_ref[...]), cak_ref[0, 0].astype(BF16), kan_ref[...],
                    cav_ref[0, 0].astype(BF16), van_ref[...], bias_c_ref[...], bias_n_ref[...], _NT)
    oa = jnp.zeros((t, WIDTH), F32)
    for h in range(N_HEADS_A):
        oa = jnp.where(lane_t // HEAD_DIM == h, oa_all[h * t:(h + 1) * t], oa)
    oa_ref[...] = oa.astype(BF16)

    past = cbk_ref.shape[3]
    v_cache = jnp.concatenate([cbv_ref[pl.ds(h, past, stride=N_HEADS_B), :] for h in range(N_HEADS_B)],
                              axis=1).astype(BF16)
    ob_all = attend(expand(qb_ref[...]), cbk_ref[0, 0].astype(BF16), kbn_ref[...],
                    v_cache, vbn_ref[...], None, None, _NN)
    lam = _lambda(lam_ref, lam_init)
    g = g_ref[...]
    outs = []
    for h in range(N_HEADS_B):
        d = ob_all[2 * h * t:(2 * h + 1) * t] - lam * ob_all[(2 * h + 1) * t:(2 * h + 2) * t]
        outs.append(_rms(d[:, h * LANES:(h + 1) * LANES], g) * (1.0 - lam_init))
    ob_ref[...] = jnp.concatenate(outs, axis=1).astype(BF16)


def _sample_attn(qa, ka, va, qb, kb, vb, cak, cav, cbk, cbv, bias_c, bias_n, lam_p, g, lam_init, t, layer):
    m = qa.shape[0]
    bs = m // t
    past = cbk.shape[3]
    new = pl.BlockSpec((t, WIDTH), lambda i: (i, 0))
    cache = lambda c: pl.BlockSpec((1, 1) + c.shape[2:], lambda i: (layer, i, 0, 0))
    cbv_spec = pl.BlockSpec((past * N_HEADS_B, LANES), lambda i: (layer * bs + i, 0))
    out = jax.ShapeDtypeStruct((m, WIDTH), BF16)
    return pl.pallas_call(
        functools.partial(_sample_kernel, lam_init=lam_init),
        grid=(bs,),
        in_specs=[_resident(lam_p.shape), _resident(g.shape)] + [new] * 6
                 + [cache(cak), cache(cav), cache(cbk), cbv_spec,
                    _resident(bias_c.shape), _resident(bias_n.shape)],
        out_specs=[new, new],
        out_shape=[out, out],
        compiler_params=_params("arbitrary"),
        name="sample_attn",
    )(lam_p, g, qa, ka, va, qb, kb, vb, cak, cav, cbk, cbv, bias_c, bias_n)


def _conv_kernel(x_ref, g_ref, w_ref, cw_ref, st_ref, y_ref, tail_ref, carry_ref, *, tps, seg_len):
    tm, d = x_ref.shape
    carried = seg_len >= tm
    if carried:
        @pl.when(pl.program_id(0) % tps == 0)
        def _():
            carry_ref[...] = st_ref[0]

    split = tm % 32 == 0 and (carried or (tm // 2) % seg_len == 0)
    halves = [(0, tm // 2), (tm // 2, tm)] if split else [(0, tm)]
    normed = [_rms(x_ref[lo:hi, :], g_ref[...]).astype(BF16) for lo, hi in halves]
    cw = cw_ref[...]
    cc = 2 * FF_CHUNK
    chunks = [slice(c * cc, (c + 1) * cc) for c in range(d // cc)]
    last_rows = [carry_ref[:, cols] for cols in chunks] if carried else None
    for (lo, hi), h in zip(halves, normed):
        for c, cols in enumerate(chunks):
            proj = lambda part: jnp.dot(h, w_ref[:, part * d + c * cc:part * d + (c + 1) * cc],
                                        preferred_element_type=F32)
            xin = proj(1) * proj(2)
            r = lax.broadcasted_iota(jnp.int32, xin.shape, 0)
            x1 = pltpu.roll(xin, 1, 0)
            x2 = pltpu.roll(xin, 2, 0)
            if carried:
                prev = last_rows[c]
                x1 = jnp.where(r == 0, prev[7:8], x1)
                x2 = jnp.where(r == 0, prev[6:7], jnp.where(r == 1, prev[7:8], x2))
                last_rows[c] = xin[hi - lo - 8:]
            else:
                t = r % seg_len
                x1 = jnp.where(t >= 1, x1, st_ref[0, lo:hi, cols])
                x2 = jnp.where(t >= 2, x2, st_ref[1, lo:hi, cols])
                tail_ref[lo:hi, cols] = xin
            conv = cw[0:1, cols] * x2 + cw[1:2, cols] * x1 + cw[2:3, cols] * xin
            y_ref[lo:hi, cols] = (proj(0) * conv).astype(BF16)
    if carried:
        for c, cols in enumerate(chunks):
            carry_ref[:, cols] = last_rows[c]
            tail_ref[0, :, cols] = last_rows[c]


def _conv_in(x, g, w, cw, state, *, seg_len):
    m, d = x.shape
    tm = min(ROW_TILE, m)
    row = lambda i: (i, 0)
    if seg_len >= tm:
        tps = seg_len // tm
        st_spec = pl.BlockSpec((1, 8, d), lambda i: (i // tps, 0, 0))
        tail_spec = pl.BlockSpec((1, 8, d), lambda i: (i // tps, 0, 0))
        tail_shape = jax.ShapeDtypeStruct((m // seg_len, 8, d), F32)
    else:
        tps = 1
        st_spec = pl.BlockSpec((2, tm, d), lambda i: (0, i, 0))
        tail_spec = pl.BlockSpec((tm, d), row)
        tail_shape = jax.ShapeDtypeStruct((m, d), F32)
    return pl.pallas_call(
        functools.partial(_conv_kernel, tps=tps, seg_len=seg_len),
        grid=(m // tm,),
        in_specs=[pl.BlockSpec((tm, d), row), _resident((1, d)), _resident(w.shape),
                  _resident(cw.shape), st_spec],
        out_specs=[pl.BlockSpec((tm, d), row), tail_spec],
        out_shape=[jax.ShapeDtypeStruct((m, d), BF16), tail_shape],
        scratch_shapes=[pltpu.VMEM((8, d), F32)],
        compiler_params=_params("arbitrary"),
        name="conv_in",
    )(x, g, w, cw, state)


def _ffn_kernel(*refs, n_parts, final):
    x_ref = refs[0]
    a_refs = refs[1:1 + n_parts]
    wo_ref, g_ref, wg_ref, wu_ref, wd_ref = refs[1 + n_parts:6 + n_parts]
    gf_ref = refs[6 + n_parts] if final else None
    o_ref, act_ref = refs[-2], refs[-1]

    tm = x_ref.shape[0]
    halves = [slice(0, tm // 2), slice(tm // 2, tm)] if tm % 32 == 0 else [slice(0, tm)]
    x1s = []
    for rows in halves:
        a = jnp.concatenate([r[rows, :] for r in a_refs], axis=1) if n_parts > 1 else a_refs[0][rows, :]
        x1s.append(x_ref[rows, :] + jnp.dot(a, wo_ref[...], preferred_element_type=F32))
    d_ff = wg_ref.shape[1]
    for rows, x1 in zip(halves, x1s):
        h = _rms(x1, g_ref[...]).astype(BF16)
        for c in range(d_ff // FF_CHUNK):
            cols = slice(c * FF_CHUNK, (c + 1) * FF_CHUNK)
            gate = jnp.dot(h, wg_ref[:, cols], preferred_element_type=F32)
            up = jnp.dot(h, wu_ref[:, cols], preferred_element_type=F32)
            act_ref[rows, cols] = (gate * (1.0 / (1.0 + jnp.exp(-gate))) * up).astype(BF16)
        y = x1 + jnp.dot(act_ref[rows, :], wd_ref[...], preferred_element_type=F32)
        o_ref[rows, :] = _rms(y, gf_ref[...]) if final else y


def _mix_out_ffn(x, parts, wo, g, wg, wu, wd, g_final=None):
    m, d = x.shape
    tm = min(ROW_TILE, m)
    row = lambda i: (i, 0)
    final = g_final is not None
    extra = [g_final] if final else []
    return pl.pallas_call(
        functools.partial(_ffn_kernel, n_parts=len(parts), final=final),
        grid=(m // tm,),
        in_specs=[pl.BlockSpec((tm, d), row)] + [pl.BlockSpec((tm, p.shape[1]), row) for p in parts]
                 + [_resident(wo.shape), _resident((1, d)), _resident(wg.shape), _resident(wu.shape),
                    _resident(wd.shape)] + [_resident((1, d))] * len(extra),
        out_specs=pl.BlockSpec((tm, d), row),
        out_shape=jax.ShapeDtypeStruct((m, d), F32),
        scratch_shapes=[pltpu.VMEM((tm, wg.shape[1]), BF16)],
        compiler_params=_params("arbitrary"),
        name="mix_out_ffn",
    )(x, *parts, wo, g, wg, wu, wd, *extra)


def kernel(x_prompt, x_sample, cache_a_k, cache_a_v, cache_b_k, cache_b_v, state_conv, norm_mix, norm_ffn, norm_final, w_attn_in, w_attn_out, rel_bias, lambda_q1, lambda_k1, lambda_q2, lambda_k2, subln_g, w_conv_in, conv_w, w_conv_out, w_ffn_gate, w_ffn_up, w_ffn_down):
    b, s, d = x_prompt.shape
    bs, t, _ = x_sample.shape
    depth = norm_mix.shape[0]
    n_attn = w_attn_in.shape[0]
    n_conv = w_conv_in.shape[0]
    past = cache_b_k.shape[2]
    a_past = cache_a_k.shape[2]
    keep = min(BAND_PAST, s)
    assert s % BAND_PAST == 0 and a_past == BAND_PAST and d % (2 * FF_CHUNK) == 0
    assert w_ffn_gate.shape[2] % FF_CHUNK == 0 and 2 <= t and a_past + t <= BIAS_LANES

    xp = x_prompt.reshape(b * s, d)
    xs = x_sample.reshape(bs * t, d)
    row = lambda v: v.reshape(1, -1)

    cos_p, sin_p = _rope_tables(jnp.arange(s))
    cos_s, sin_s = (jnp.tile(tab, (bs, 1)) for tab in _rope_tables(past + jnp.arange(t)))

    table = rel_bias.reshape(n_attn * N_HEADS_A, 2 * MAX_REL + 1)
    band_bias = _band_bias(table)
    sample_bias = _sample_bias(table, t)

    cak_t = cache_a_k.transpose(0, 1, 3, 4, 2).reshape(n_attn, bs, WIDTH, a_past)
    cav_t = cache_a_v.transpose(0, 1, 3, 4, 2).reshape(n_attn, bs, WIDTH, a_past)
    cbk_t = cache_b_k.transpose(0, 1, 3, 4, 5, 2).reshape(n_attn, bs, WIDTH, past)
    cbv_rows = cache_b_v.reshape(n_attn * bs * past * N_HEADS_B, LANES)

    stacked_p = stacked_s = None
    pcs, scs = [], []
    for i in range(depth):
        j = i // 2
        g_mix = row(norm_mix[i])
        if i % 2 == 0:
            lam_init = 0.8 - 0.6 * math.exp(-0.3 * i)
            w_in = w_attn_in[j].astype(BF16)
            wo = w_attn_out[j].astype(BF16)
            lam_p = jnp.stack([lambda_q1[j], lambda_k1[j], lambda_q2[j], lambda_k2[j]])
            g_sub = row(subln_g[j])

            outs = _attn_in_proj(xp, g_mix, w_in, cos_p, sin_p, stacked_p,
                                 seg_len=s, keep=keep, layer=j, n_layers=n_attn, native=True)
            qa, ka, va_t, qb, kb, vb_t = outs[:6]
            qa, ka, qb, kb = (o.reshape(b, s, WIDTH) for o in (qa, ka, qb, kb))
            stacked_p = outs[6:]
            oa = _band_attn(qa, ka, va_t, band_bias, j)
            ob = _diff_attn(qb, kb, vb_t, lam_p, g_sub, lam_init)
            parts_p = [oa.reshape(b * s, WIDTH), ob.reshape(b * s, WIDTH)]

            outs = _attn_in_proj(xs, g_mix, w_in, cos_s, sin_s, stacked_s,
                                 seg_len=t, keep=t, layer=j, n_layers=n_attn, native=False)
            stacked_s = outs[6:]
            oa, ob = _sample_attn(
                *outs[:6], cak_t, cav_t, cbk_t, cbv_rows,
                sample_bias[j, :, :a_past], sample_bias[j, :, a_past:a_past + t],
                lam_p, g_sub, lam_init, t, j)
            parts_s = [oa, ob]
        else:
            w_in = w_conv_in[j].astype(BF16)
            wo = w_conv_out[j].astype(BF16)
            yp, tail = _conv_in(xp, g_mix, w_in, conv_w[j], jnp.zeros((b, 8, d), F32), seg_len=s)
            pcs.append(tail[:, 6:])
            st = state_conv[j]
            stand_in = jnp.stack([jnp.pad(st[:, 1:2], ((0, 0), (0, t - 1), (0, 0))),
                                  jnp.pad(st, ((0, 0), (0, t - 2), (0, 0)))]).reshape(2, bs * t, d)
            ys, xin = _conv_in(xs, g_mix, w_in, conv_w[j], stand_in, seg_len=t)
            scs.append(xin.reshape(bs, t, d)[:, t - 2:])
            parts_p, parts_s = [yp], [ys]

        ffn = (wo, row(norm_ffn[i]), w_ffn_gate[i].astype(BF16), w_ffn_up[i].astype(BF16),
               w_ffn_down[i].astype(BF16))
        g_final = row(norm_final) if i == depth - 1 else None
        xp = _mix_out_ffn(xp, parts_p, *ffn, g_final=g_final)
        xs = _mix_out_ffn(xs, parts_s, *ffn, g_final=g_final)

    pak, pav, pbk, pbv = stacked_p
    sak, sav, sbk, sbv = stacked_s
    heads_a = lambda z: z.reshape(n_attn, b, N_HEADS_A, HEAD_DIM, keep).transpose(0, 1, 4, 2, 3)
    return (xp.reshape(b, s, d), xs.reshape(bs, t, d),
            heads_a(pak), heads_a(pav),
            pbk.reshape(n_attn, b, N_HEADS_B, 2, HEAD_DIM, s).transpose(0, 1, 5, 2, 3, 4),
            pbv.reshape(n_attn, b, s, N_HEADS_B, 2 * HEAD_DIM),
            jnp.stack(pcs),
            sak.reshape(n_attn, bs, t, N_HEADS_A, HEAD_DIM), sav.reshape(n_attn, bs, t, N_HEADS_A, HEAD_DIM),
            sbk.reshape(n_attn, bs, t, N_HEADS_B, 2, HEAD_DIM), sbv.reshape(n_attn, bs, t, N_HEADS_B, 2 * HEAD_DIM),
            jnp.stack(scs))
```

```python
import functools
import math

import jax
import jax.numpy as jnp
from jax import lax
from jax.experimental import pallas as pl
from jax.experimental.pallas import tpu as pltpu

F32 = jnp.float32
BF16 = jnp.bfloat16

EPS = 1e-6
NEG = -1e30
LOG2E = 1.4426950408889634
ROPE_THETA = 10000.0
CHUNK = 64
HEAD_DIM = 64
N_HEADS_A = 8
N_HEADS_B = 4
BAND_PAST = 512
MAX_REL = 128
BAND_KEYS = BAND_PAST + CHUNK
BAND_Q = 256
BAND_WIN = BAND_Q + BAND_PAST
DIFF_KEYS = 512
BIAS_LANES = 640
WIDTH = N_HEADS_A * HEAD_DIM
LANES = 128
ROW_TILE = 512
FF_CHUNK = 256
VMEM_LIMIT = 56 * 1024 * 1024

_NT = (((1,), (1,)), ((), ()))
_TN = (((0,), (0,)), ((), ()))
_NN = (((1,), (0,)), ((), ()))


def _resident(shape):
    return pl.BlockSpec(shape, lambda *_: (0,) * len(shape), pipeline_mode=pl.Buffered(1))


def _rms(x, g):
    return x * lax.rsqrt(jnp.mean(x * x, axis=-1, keepdims=True) + EPS) * g


def _params(*sem):
    return pltpu.CompilerParams(dimension_semantics=sem, vmem_limit_bytes=VMEM_LIMIT)


SUM_ROWS = 16


def _values_and_sum(v_t, p):
    ones = jnp.ones((SUM_ROWS, v_t.shape[1]), v_t.dtype)
    return jnp.dot(jnp.concatenate([v_t, ones], axis=0), p, preferred_element_type=F32)


def _split_maps(q):
    low = lax.broadcasted_iota(jnp.int32, q.shape, 1) < HEAD_DIM
    zero = jnp.zeros_like(q)
    return jnp.concatenate([jnp.where(low, q, zero), jnp.where(low, zero, q)], axis=0)


def _attn_in_kernel(*refs, native):
    x_ref, g_ref, w_ref, cos_ref, sin_ref = refs[:5]
    (qa_ref, ka_ref, va_ref, qb_ref, kb_ref, vb_ref,
     kaf_ref, vaf_ref, kbf_ref, vbf_ref) = refs[-10:]
    q_scale = LOG2E / math.sqrt(HEAD_DIM)
    tm = x_ref.shape[0]
    halves = [(0, tm // 2), (tm // 2, tm)] if tm % 32 == 0 else [(0, tm)]
    normed = [_rms(x_ref[lo:hi, :], g_ref[...]).astype(BF16) for lo, hi in halves]

    for (lo, hi), h in zip(halves, normed):
        def proj(c):
            return jnp.dot(h, w_ref[:, c * WIDTH:(c + 1) * WIDTH], preferred_element_type=F32)

        cos = cos_ref[lo:hi, :]
        sin = sin_ref[lo:hi, :]
        low_half = (lax.broadcasted_iota(jnp.int32, cos.shape, 1) % HEAD_DIM) < (HEAD_DIM // 2)

        def rotary(z):
            parts = []
            for j in range(WIDTH // LANES):
                zj = z[:, j * LANES:(j + 1) * LANES]
                swapped = jnp.where(low_half, pltpu.roll(zj, LANES - HEAD_DIM // 2, 1),
                                    pltpu.roll(zj, HEAD_DIM // 2, 1))
                parts.append(zj * cos + swapped * sin)
            return jnp.concatenate(parts, axis=1)

        vb = proj(5)
        if native:
            vb_ref[0, 0, :, lo:hi] = vb.T.astype(BF16)
            for head in range(N_HEADS_B):
                vbf_ref[pl.ds(lo * N_HEADS_B + head, hi - lo, stride=N_HEADS_B), :] = (
                    vb[:, head * LANES:(head + 1) * LANES])
        else:
            vb_ref[lo:hi, :] = vb.astype(BF16)
            vbf_ref[lo:hi, :] = vb
        kb = rotary(proj(4))
        kb_ref[lo:hi, :] = kb.astype(BF16)
        va = proj(2)
        ka = proj(1)
        ka_ref[lo:hi, :] = ka.astype(BF16)
        if native:
            kbf_ref[0, 0, :, lo:hi] = kb.T
            va_t = va.T
            vaf_ref[0, 0, :, lo:hi] = va_t
            va_ref[0, :, lo:hi] = va_t.astype(BF16)
            kaf_ref[0, 0, :, lo:hi] = ka.T
        else:
            kbf_ref[lo:hi, :] = kb
            vaf_ref[lo:hi, :] = va
            va_ref[lo:hi, :] = va.astype(BF16)
            kaf_ref[lo:hi, :] = ka
        qb_ref[lo:hi, :] = (rotary(proj(3)) * q_scale).astype(BF16)
        qa_ref[lo:hi, :] = (proj(0) * q_scale).astype(BF16)


def _attn_in_proj(x, g, w, cos_t, sin_t, stacked, *, seg_len, keep, layer, n_layers, native):
    m, d = x.shape
    tm = min(ROW_TILE, m)
    n_tiles = m // tm
    n_tab = cos_t.shape[0] // tm
    n_seg = m // seg_len
    row = lambda i: (i, 0)
    tab = lambda i: (i % n_tab, 0)
    blk = pl.BlockSpec((tm, WIDTH), row)
    bf_out = jax.ShapeDtypeStruct((m, WIDTH), BF16)
    if native:
        tps, kpt = seg_len // tm, keep // tm
        t_blk = (1, 1, WIDTH, tm)
        keep_spec = pl.BlockSpec(
            t_blk, lambda i: (layer, i // tps, 0, jnp.clip(i % tps - (tps - kpt), 0, kpt - 1)))
        keep_out = jax.ShapeDtypeStruct((n_layers, n_seg, WIDTH, keep), F32)
        kb_spec = pl.BlockSpec(t_blk, lambda i: (layer, i // tps, 0, i % tps))
        kb_out = jax.ShapeDtypeStruct((n_layers, n_seg, WIDTH, seg_len), F32)
        vb_spec = pl.BlockSpec((tm * N_HEADS_B, LANES), lambda i: (layer * n_tiles + i, 0))
        vb_out = jax.ShapeDtypeStruct((n_layers * m * N_HEADS_B, LANES), F32)
        va_spec = pl.BlockSpec((1, WIDTH, tm), lambda i: (i // tps, 0, i % tps))
        va_out = jax.ShapeDtypeStruct((n_seg, WIDTH, seg_len), BF16)
        vbb_spec = pl.BlockSpec((1, 1, WIDTH, tm), lambda i: (i // tps, i % tps, 0, 0))
        vbb_out = jax.ShapeDtypeStruct((n_seg, tps, WIDTH, tm), BF16)
    else:
        assert keep == seg_len
        keep_spec = kb_spec = vb_spec = pl.BlockSpec((tm, WIDTH), lambda i: (layer * n_tiles + i, 0))
        keep_out = kb_out = vb_out = jax.ShapeDtypeStruct((n_layers * m, WIDTH), F32)
        va_spec = vbb_spec = blk
        va_out = vbb_out = bf_out
    carried = [] if stacked is None else list(stacked)
    n_in = 5
    return pl.pallas_call(
        functools.partial(_attn_in_kernel, native=native),
        grid=(n_tiles,),
        in_specs=[pl.BlockSpec((tm, d), row), _resident((1, d)), _resident(w.shape),
                  pl.BlockSpec((tm, LANES), tab), pl.BlockSpec((tm, LANES), tab)]
                 + [pl.BlockSpec(memory_space=pl.ANY)] * len(carried),
        out_specs=[blk, blk, va_spec, blk, blk, vbb_spec, keep_spec, keep_spec, kb_spec, vb_spec],
        out_shape=[bf_out, bf_out, va_out, bf_out, bf_out, vbb_out, keep_out, keep_out, kb_out, vb_out],
        input_output_aliases={n_in + k: 6 + k for k in range(len(carried))},
        compiler_params=_params("arbitrary"),
        name="attn_in_proj",
    )(x, g, w, cos_t, sin_t, *carried)


def _rope_tables(pos):
    half = HEAD_DIM // 2
    inv = ROPE_THETA ** (-jnp.arange(half, dtype=F32) / half)
    ang = pos.astype(F32)[:, None] * inv[None, :]
    c, s = jnp.cos(ang), jnp.sin(ang)
    reps = LANES // HEAD_DIM
    return jnp.tile(c, (1, 2 * reps)), jnp.tile(jnp.concatenate([-s, s], axis=1), (1, reps))


def _band_bias_kernel(tab_ref, o_ref):
    r = pl.program_id(0)
    e_shape = (BAND_WIN + LANES, LANES)
    y = lax.broadcasted_iota(jnp.int32, e_shape, 0) - LANES
    lane = lax.broadcasted_iota(jnp.int32, e_shape, 1)
    idx = jnp.clip(BAND_PAST + lane - y, -MAX_REL, MAX_REL) + MAX_REL
    cols = []
    for hh in range(2):
        def body(d, acc, hh=hh):
            return jnp.where(idx == d, tab_ref[2 * r + hh, d], acc)

        e = lax.fori_loop(MAX_REL - CHUNK + 1, 2 * MAX_REL + 1, body, jnp.zeros(e_shape, F32))
        for c in range(BAND_Q // LANES):
            cols.append(e[LANES - LANES * c:LANES - LANES * c + BAND_WIN])
    bias = jnp.concatenate(cols, axis=1)

    kj = lax.broadcasted_iota(jnp.int32, bias.shape, 0)
    qi = lax.broadcasted_iota(jnp.int32, bias.shape, 1) % BAND_Q
    first_key = (qi // CHUNK) * CHUNK
    in_band = jnp.logical_and(kj >= first_key, kj < first_key + BAND_KEYS)
    o_ref[0] = jnp.where(in_band, bias * LOG2E, NEG)


def _band_bias(table):
    n = table.shape[0] // 2
    return pl.pallas_call(
        _band_bias_kernel,
        grid=(n,),
        in_specs=[pl.BlockSpec(memory_space=pltpu.SMEM)],
        out_specs=pl.BlockSpec((1, BAND_WIN, 2 * BAND_Q), lambda r: (r, 0, 0)),
        out_shape=jax.ShapeDtypeStruct((n, BAND_WIN, 2 * BAND_Q), F32),
        compiler_params=_params("arbitrary"),
        name="band_bias",
    )(table)


def _sample_bias_kernel(tab_ref, o_ref, *, t):
    r = pl.program_id(0)
    shape = (N_HEADS_A * t, BIAS_LANES)
    row = lax.broadcasted_iota(jnp.int32, shape, 0)
    kj = lax.broadcasted_iota(jnp.int32, shape, 1)
    idx = jnp.clip(BAND_PAST + row % t - kj, -MAX_REL, MAX_REL) + MAX_REL
    head = lax.broadcasted_iota(jnp.int32, (N_HEADS_A * t, 1), 0) // t

    def body(d, acc):
        col = jnp.zeros(head.shape, F32)
        for h in range(N_HEADS_A):
            col = jnp.where(head == h, tab_ref[N_HEADS_A * r + h, d], col)
        return jnp.where(idx == d, col, acc)

    o_ref[0] = lax.fori_loop(0, 2 * MAX_REL + 1, body, jnp.zeros(shape, F32)) * LOG2E


def _sample_bias(table, t):
    n = table.shape[0] // N_HEADS_A
    return pl.pallas_call(
        functools.partial(_sample_bias_kernel, t=t),
        grid=(n,),
        in_specs=[pl.BlockSpec(memory_space=pltpu.SMEM)],
        out_specs=pl.BlockSpec((1, N_HEADS_A * t, BIAS_LANES), lambda r: (r, 0, 0)),
        out_shape=jax.ShapeDtypeStruct((n, N_HEADS_A * t, BIAS_LANES), F32),
        compiler_params=_params("arbitrary"),
        name="sample_bias",
    )(table)


def _band_kernel(q_ref, kp_ref, kc_ref, vp_ref, vc_ref, bias_ref, o_ref, *bufs):
    qblk = pl.program_id(2)
    n_sub = q_ref.shape[1] // BAND_Q

    def window(prev_ref, cur_ref, sub, first, axis):
        lo, hi = sub * BAND_Q, (sub + 1) * BAND_Q
        take = lambda ref, a, b: ref[0, a:b] if axis == 0 else ref[0, :, a:b]
        parts = []
        if lo < BAND_PAST and not first:
            parts.append(take(prev_ref, lo, BAND_PAST))
        parts.append(take(cur_ref, max(lo - BAND_PAST, 0), hi))
        return parts[0] if len(parts) == 1 else jnp.concatenate(parts, axis=axis)

    def scores(sub, first):
        keys = window(kp_ref, kc_ref, sub, first, 0)
        nk = keys.shape[0]
        q = q_ref[0, sub * BAND_Q:(sub + 1) * BAND_Q, :]
        s = lax.dot_general(keys, _split_maps(q), _NT, preferred_element_type=F32)
        s = s + bias_ref[0, BAND_WIN - nk:, :]
        bufs[2 * sub][:nk] = s
        bufs[2 * sub + 1][...] = jnp.max(s, axis=0, keepdims=True)
        return nk

    def finish(nk, sub, first):
        p = jnp.exp2(bufs[2 * sub][:nk] - bufs[2 * sub + 1][...]).astype(BF16)
        o_t = _values_and_sum(window(vp_ref, vc_ref, sub, first, 1), p)
        o_t = o_t[:LANES] / o_t[LANES:LANES + 1]
        o = jnp.concatenate([o_t[:HEAD_DIM, :BAND_Q], o_t[HEAD_DIM:, BAND_Q:]], axis=0).T
        o_ref[0, sub * BAND_Q:(sub + 1) * BAND_Q, :] = o.astype(BF16)

    def run(first):
        sizes = [scores(sub, first) for sub in range(n_sub)]
        for sub in range(n_sub):
            finish(sizes[sub], sub, first)

    pl.when(qblk == 0)(functools.partial(run, True))
    pl.when(qblk > 0)(functools.partial(run, False))


def _band_attn(q, k, v_t, bias, layer):
    b, s, _ = q.shape
    bq = 2 * BAND_PAST if s % (2 * BAND_PAST) == 0 else BAND_PAST
    n_pairs = N_HEADS_A // 2
    prev_of = lambda qi: jnp.maximum(qi * (bq // BAND_PAST) - 1, 0)
    cur = lambda hp, bi, qi: (bi, qi, hp)
    prev = lambda hp, bi, qi: (bi, prev_of(qi), hp)
    cur_t = lambda hp, bi, qi: (bi, hp, qi)
    prev_t = lambda hp, bi, qi: (bi, hp, prev_of(qi))
    blk = (1, bq, LANES)
    return pl.pallas_call(
        _band_kernel,
        grid=(n_pairs, b, s // bq),
        in_specs=[pl.BlockSpec(blk, cur), pl.BlockSpec((1, BAND_PAST, LANES), prev), pl.BlockSpec(blk, cur),
                  pl.BlockSpec((1, LANES, BAND_PAST), prev_t), pl.BlockSpec((1, LANES, bq), cur_t),
                  pl.BlockSpec((1, BAND_WIN, 2 * BAND_Q), lambda hp, bi, qi: (layer * n_pairs + hp, 0, 0))],
        out_specs=pl.BlockSpec(blk, cur),
        out_shape=jax.ShapeDtypeStruct((b, s, WIDTH), BF16),
        scratch_shapes=[pltpu.VMEM((BAND_WIN, 2 * BAND_Q), F32), pltpu.VMEM((1, 2 * BAND_Q), F32)]
                       * (bq // BAND_Q),
        compiler_params=_params("arbitrary", "arbitrary", "arbitrary"),
        name="band_attn",
    )(q, k, k, v_t, v_t, bias)


def _lambda(lam_ref, lam_init):
    lp = lam_ref[...]
    a1 = jnp.sum(lp[0:1] * lp[1:2], axis=-1, keepdims=True)
    a2 = jnp.sum(lp[2:3] * lp[3:4], axis=-1, keepdims=True)
    return jnp.exp(a1) - jnp.exp(a2) + lam_init


def _diff_kernel(lam_ref, g_ref, q_ref, k_ref, v_ref, o_ref, m_ref, acc_ref, *bufs, lam_init):
    qi = pl.program_id(2)
    q = q_ref[0]
    bq = q.shape[0]
    bk = bq // 2
    q2 = _split_maps(q)
    queries = (q2[:bq], q2[bq:])

    m_ref[...] = jnp.full(m_ref.shape, NEG, F32)
    acc_ref[...] = jnp.zeros(acc_ref.shape, F32)

    def block(ref, kj):
        return ref[0, pl.ds(pl.multiple_of(kj * bk, bk), bk), :]

    def scores(kj, stream, buf, diagonal=None):
        s_ref, mx_ref = bufs[4 * stream + 2 * buf], bufs[4 * stream + 2 * buf + 1]
        qs = queries[stream][bk:] if diagonal == 1 else queries[stream]
        s = lax.dot_general(block(k_ref, kj), qs, _NT, preferred_element_type=F32)
        if diagonal is not None:
            k_chunk = lax.broadcasted_iota(jnp.int32, s.shape, 0) // CHUNK
            q_chunk = lax.broadcasted_iota(jnp.int32, s.shape, 1) // CHUNK
            s = jnp.where(k_chunk <= q_chunk, s, NEG)
        n = s.shape[1]
        s_ref[:, :n] = s
        mx_ref[:, :n] = jnp.max(s, axis=0, keepdims=True)

    def absorb(kj, stream, buf, second_half=False):
        s_ref, mx_ref = bufs[4 * stream + 2 * buf], bufs[4 * stream + 2 * buf + 1]
        n = bk if second_half else bq
        cols = slice((stream + 1) * bq - n, (stream + 1) * bq)
        m_old = m_ref[:, cols]
        m_new = jnp.maximum(m_old, mx_ref[:, :n])
        alpha = jnp.exp2(m_old - m_new)
        p = jnp.exp2(s_ref[:, :n] - m_new).astype(BF16)
        acc_ref[:, cols] = alpha * acc_ref[:, cols] + _values_and_sum(v_ref[0, kj], p)
        m_ref[:, cols] = m_new

    def stage(kj, buf, look=True, look_diagonal=None, second_half=False):
        if look:
            scores(kj + 1, 0, 1 - buf, look_diagonal)
        absorb(kj, 1, buf, second_half)
        if look:
            scores(kj + 1, 1, 1 - buf, look_diagonal)
        absorb(kj, 0, buf, second_half)

    @pl.when(qi == 0)
    def _():
        scores(0, 0, 0, diagonal=0)
        scores(0, 1, 0, diagonal=0)
        stage(0, 0, look_diagonal=1)
        stage(1, 1, look=False, second_half=True)

    @pl.when(qi > 0)
    def _():
        scores(0, 0, 0)
        scores(0, 1, 0)

    def two_blocks(pair, carry):
        stage(2 * pair, 0)
        stage(2 * pair + 1, 1)
        return carry

    lax.fori_loop(0, jnp.maximum(qi - 1, 0), two_blocks, 0)

    @pl.when(qi > 0)
    def _():
        last = 2 * qi
        stage(last - 2, 0)
        stage(last - 1, 1, look_diagonal=0)
        stage(last, 0, look_diagonal=1)
        stage(last + 1, 1, look=False, second_half=True)

    o_t = acc_ref[:LANES] / acc_ref[LANES:LANES + 1]
    lam = _lambda(lam_ref, lam_init)
    d_t = o_t[:, :bq] - lam * o_t[:, bq:]
    inv = lax.rsqrt(jnp.mean(d_t * d_t, axis=0, keepdims=True) + EPS)
    o_ref[0] = ((d_t * inv).T * (g_ref[...] * (1.0 - lam_init))).astype(BF16)


def _diff_attn(q, k, v_t, lam_p, g, lam_init):
    b, s, _ = q.shape
    bq = 2 * DIFF_KEYS
    assert s % bq == 0 and v_t.shape[3] == DIFF_KEYS
    seq = pl.BlockSpec((1, s, LANES), lambda bi, h, qi: (bi, 0, h))
    seq_t = pl.BlockSpec((1, s // DIFF_KEYS, LANES, DIFF_KEYS), lambda bi, h, qi: (bi, 0, h, 0))
    blk = pl.BlockSpec((1, bq, LANES), lambda bi, h, qi: (bi, qi, h))
    return pl.pallas_call(
        functools.partial(_diff_kernel, lam_init=lam_init),
        grid=(b, N_HEADS_B, s // bq),
        in_specs=[_resident(lam_p.shape), _resident(g.shape), blk, seq, seq_t],
        out_specs=blk,
        out_shape=jax.ShapeDtypeStruct((b, s, WIDTH), BF16),
        scratch_shapes=[pltpu.VMEM((1, 2 * bq), F32), pltpu.VMEM((LANES + SUM_ROWS, 2 * bq), F32),
                        ] + [pltpu.VMEM((DIFF_KEYS, bq), F32), pltpu.VMEM((1, bq), F32)] * 4,
        compiler_params=_params("arbitrary", "arbitrary", "arbitrary"),
        name="diff_attn",
    )(lam_p, g, q, k, v_t)


def _sample_kernel(lam_ref, g_ref, qa_ref, kan_ref, van_ref, qb_ref, kbn_ref, vbn_ref,
                   cak_ref, cav_ref, cbk_ref, cbv_ref, bias_c_ref, bias_n_ref,
                   oa_ref, ob_ref, *, lam_init):
    t = qa_ref.shape[0]
    n_grp = WIDTH // HEAD_DIM
    rows = n_grp * t
    grp_of_row = lax.broadcasted_iota(jnp.int32, (rows, WIDTH), 0) // t
    grp_of_lane = lax.broadcasted_iota(jnp.int32, (rows, WIDTH), 1) // HEAD_DIM
    own = grp_of_row == grp_of_lane
    lane_t = lax.broadcasted_iota(jnp.int32, (t, WIDTH), 1)

    def expand(q):
        qe = jnp.concatenate([q] * n_grp, axis=0)
        return jnp.where(own, qe, jnp.zeros_like(qe))

    def attend(qe, kt_cache, k_new, v_cache, v_new, bias_c, bias_n, v_dims):
        s_c = jnp.dot(qe, kt_cache, preferred_element_type=F32)
        s_n = lax.dot_general(qe, k_new, _NT, preferred_element_type=F32)
        if bias_c is not None:
            s_c = s_c + bias_c
            s_n = s_n + bias_n
        m = jnp.maximum(jnp.max(s_c, axis=-1, keepdims=True), jnp.max(s_n, axis=-1, keepdims=True))
        p_c = jnp.exp2(s_c - m)
        p_n = jnp.exp2(s_n - m)
        l = jnp.sum(p_c, axis=-1, keepdims=True) + jnp.sum(p_n, axis=-1, keepdims=True)
        o = (lax.dot_general(p_c.astype(BF16), v_cache, v_dims, preferred_element_type=F32)
             + jnp.dot(p_n.astype(BF16), v_new, preferred_element_type=F32))
        return o / l

    oa_all = attend(expand(qa_ref[...]), cak_ref[0, 0].astype(BF16), kan_ref[...],
                    cav_ref[0, 0].astype(BF16), van_ref[...], bias_c_ref[...], bias_n_ref[...], _NT)
    oa = jnp.zeros((t, WIDTH), F32)
    for h in range(N_HEADS_A):
        oa = jnp.where(lane_t // HEAD_DIM == h, oa_all[h * t:(h + 1) * t], oa)
    oa_ref[...] = oa.astype(BF16)

    past = cbk_ref.shape[3]
    v_cache = jnp.concatenate([cbv_ref[pl.ds(h, past, stride=N_HEADS_B), :] for h in range(N_HEADS_B)],
                              axis=1).astype(BF16)
    ob_all = attend(expand(qb_ref[...]), cbk_ref[0, 0].astype(BF16), kbn_ref[...],
                    v_cache, vbn_ref[...], None, None, _NN)
    lam = _lambda(lam_ref, lam_init)
    g = g_ref[...]
    outs = []
    for h in range(N_HEADS_B):
        d = ob_all[2 * h * t:(2 * h + 1) * t] - lam * ob_all[(2 * h + 1) * t:(2 * h + 2) * t]
        outs.append(_rms(d[:, h * LANES:(h + 1) * LANES], g) * (1.0 - lam_init))
    ob_ref[...] = jnp.concatenate(outs, axis=1).astype(BF16)


def _sample_attn(qa, ka, va, qb, kb, vb, cak, cav, cbk, cbv, bias_c, bias_n, lam_p, g, lam_init, t, layer):
    m = qa.shape[0]
    bs = m // t
    past = cbk.shape[3]
    new = pl.BlockSpec((t, WIDTH), lambda i: (i, 0))
    cache = lambda c: pl.BlockSpec((1, 1) + c.shape[2:], lambda i: (layer, i, 0, 0))
    cbv_spec = pl.BlockSpec((past * N_HEADS_B, LANES), lambda i: (layer * bs + i, 0))
    out = jax.ShapeDtypeStruct((m, WIDTH), BF16)
    return pl.pallas_call(
        functools.partial(_sample_kernel, lam_init=lam_init),
        grid=(bs,),
        in_specs=[_resident(lam_p.shape), _resident(g.shape)] + [new] * 6
                 + [cache(cak), cache(cav), cache(cbk), cbv_spec,
                    _resident(bias_c.shape), _resident(bias_n.shape)],
        out_specs=[new, new],
        out_shape=[out, out],
        compiler_params=_params("arbitrary"),
        name="sample_attn",
    )(lam_p, g, qa, ka, va, qb, kb, vb, cak, cav, cbk, cbv, bias_c, bias_n)


def _conv_kernel(x_ref, g_ref, w_ref, cw_ref, st_ref, y_ref, tail_ref, carry_ref, *, tps, seg_len):
    tm, d = x_ref.shape
    h = _rms(x_ref[...], g_ref[...]).astype(BF16)
    carried = seg_len >= tm
    if carried:
        @pl.when(pl.program_id(0) % tps == 0)
        def _():
            carry_ref[...] = st_ref[0]

    cw = cw_ref[...]
    cc = 2 * FF_CHUNK
    for c in range(d // cc):
        cols = slice(c * cc, (c + 1) * cc)
        proj = lambda part: jnp.dot(h, w_ref[:, part * d + c * cc:part * d + (c + 1) * cc],
                                    preferred_element_type=F32)
        xin = proj(1) * proj(2)
        r = lax.broadcasted_iota(jnp.int32, xin.shape, 0)
        x1 = pltpu.roll(xin, 1, 0)
        x2 = pltpu.roll(xin, 2, 0)
        if carried:
            prev = carry_ref[:, cols]
            x1 = jnp.where(r == 0, prev[7:8], x1)
            x2 = jnp.where(r == 0, prev[6:7], jnp.where(r == 1, prev[7:8], x2))
            carry_ref[:, cols] = xin[tm - 8:]
            tail_ref[0, :, cols] = xin[tm - 8:]
        else:
            t = r % seg_len
            x1 = jnp.where(t >= 1, x1, st_ref[0, :, cols])
            x2 = jnp.where(t >= 2, x2, st_ref[1, :, cols])
            tail_ref[:, cols] = xin
        conv = cw[0:1, cols] * x2 + cw[1:2, cols] * x1 + cw[2:3, cols] * xin
        y_ref[:, cols] = (proj(0) * conv).astype(BF16)


def _conv_in(x, g, w, cw, state, *, seg_len):
    m, d = x.shape
    tm = min(ROW_TILE, m)
    row = lambda i: (i, 0)
    if seg_len >= tm:
        tps = seg_len // tm
        st_spec = pl.BlockSpec((1, 8, d), lambda i: (i // tps, 0, 0))
        tail_spec = pl.BlockSpec((1, 8, d), lambda i: (i // tps, 0, 0))
        tail_shape = jax.ShapeDtypeStruct((m // seg_len, 8, d), F32)
    else:
        tps = 1
        st_spec = pl.BlockSpec((2, tm, d), lambda i: (0, i, 0))
        tail_spec = pl.BlockSpec((tm, d), row)
        tail_shape = jax.ShapeDtypeStruct((m, d), F32)
    return pl.pallas_call(
        functools.partial(_conv_kernel, tps=tps, seg_len=seg_len),
        grid=(m // tm,),
        in_specs=[pl.BlockSpec((tm, d), row), _resident((1, d)), _resident(w.shape),
                  _resident(cw.shape), st_spec],
        out_specs=[pl.BlockSpec((tm, d), row), tail_spec],
        out_shape=[jax.ShapeDtypeStruct((m, d), BF16), tail_shape],
        scratch_shapes=[pltpu.VMEM((8, d), F32)],
        compiler_params=_params("arbitrary"),
        name="conv_in",
    )(x, g, w, cw, state)


def _ffn_kernel(*refs, n_parts, final):
    x_ref = refs[0]
    a_refs = refs[1:1 + n_parts]
    wo_ref, g_ref, wg_ref, wu_ref, wd_ref = refs[1 + n_parts:6 + n_parts]
    gf_ref = refs[6 + n_parts] if final else None
    o_ref, act_ref = refs[-2], refs[-1]

    tm = x_ref.shape[0]
    halves = [slice(0, tm // 2), slice(tm // 2, tm)] if tm % 32 == 0 else [slice(0, tm)]
    x1s = []
    for rows in halves:
        a = jnp.concatenate([r[rows, :] for r in a_refs], axis=1) if n_parts > 1 else a_refs[0][rows, :]
        x1s.append(x_ref[rows, :] + jnp.dot(a, wo_ref[...], preferred_element_type=F32))
    d_ff = wg_ref.shape[1]
    for rows, x1 in zip(halves, x1s):
        h = _rms(x1, g_ref[...]).astype(BF16)
        for c in range(d_ff // FF_CHUNK):
            cols = slice(c * FF_CHUNK, (c + 1) * FF_CHUNK)
            gate = jnp.dot(h, wg_ref[:, cols], preferred_element_type=F32)
            up = jnp.dot(h, wu_ref[:, cols], preferred_element_type=F32)
            act_ref[rows, cols] = (gate * (1.0 / (1.0 + jnp.exp(-gate))) * up).astype(BF16)
        y = x1 + jnp.dot(act_ref[rows, :], wd_ref[...], preferred_element_type=F32)
        o_ref[rows, :] = _rms(y, gf_ref[...]) if final else y


def _mix_out_ffn(x, parts, wo, g, wg, wu, wd, g_final=None):
    m, d = x.shape
    tm = min(ROW_TILE, m)
    row = lambda i: (i, 0)
    final = g_final is not None
    extra = [g_final] if final else []
    return pl.pallas_call(
        functools.partial(_ffn_kernel, n_parts=len(parts), final=final),
        grid=(m // tm,),
        in_specs=[pl.BlockSpec((tm, d), row)] + [pl.BlockSpec((tm, p.shape[1]), row) for p in parts]
                 + [_resident(wo.shape), _resident((1, d)), _resident(wg.shape), _resident(wu.shape),
                    _resident(wd.shape)] + [_resident((1, d))] * len(extra),
        out_specs=pl.BlockSpec((tm, d), row),
        out_shape=jax.ShapeDtypeStruct((m, d), F32),
        scratch_shapes=[pltpu.VMEM((tm, wg.shape[1]), BF16)],
        compiler_params=_params("arbitrary"),
        name="mix_out_ffn",
    )(x, *parts, wo, g, wg, wu, wd, *extra)


def kernel(x_prompt, x_sample, cache_a_k, cache_a_v, cache_b_k, cache_b_v, state_conv, norm_mix, norm_ffn, norm_final, w_attn_in, w_attn_out, rel_bias, lambda_q1, lambda_k1, lambda_q2, lambda_k2, subln_g, w_conv_in, conv_w, w_conv_out, w_ffn_gate, w_ffn_up, w_ffn_down):
    b, s, d = x_prompt.shape
    bs, t, _ = x_sample.shape
    depth = norm_mix.shape[0]
    n_attn = w_attn_in.shape[0]
    n_conv = w_conv_in.shape[0]
    past = cache_b_k.shape[2]
    a_past = cache_a_k.shape[2]
    keep = min(BAND_PAST, s)
    assert s % BAND_PAST == 0 and a_past == BAND_PAST and d % (2 * FF_CHUNK) == 0
    assert w_ffn_gate.shape[2] % FF_CHUNK == 0 and 2 <= t and a_past + t <= BIAS_LANES

    xp = x_prompt.reshape(b * s, d)
    xs = x_sample.reshape(bs * t, d)
    row = lambda v: v.reshape(1, -1)

    cos_p, sin_p = _rope_tables(jnp.arange(s))
    cos_s, sin_s = (jnp.tile(tab, (bs, 1)) for tab in _rope_tables(past + jnp.arange(t)))

    table = rel_bias.reshape(n_attn * N_HEADS_A, 2 * MAX_REL + 1)
    band_bias = _band_bias(table)
    sample_bias = _sample_bias(table, t)

    cak_t = cache_a_k.transpose(0, 1, 3, 4, 2).reshape(n_attn, bs, WIDTH, a_past)
    cav_t = cache_a_v.transpose(0, 1, 3, 4, 2).reshape(n_attn, bs, WIDTH, a_past)
    cbk_t = cache_b_k.transpose(0, 1, 3, 4, 5, 2).reshape(n_attn, bs, WIDTH, past)
    cbv_rows = cache_b_v.reshape(n_attn * bs * past * N_HEADS_B, LANES)

    stacked_p = stacked_s = None
    pcs, scs = [], []
    for i in range(depth):
        j = i // 2
        g_mix = row(norm_mix[i])
        if i % 2 == 0:
            lam_init = 0.8 - 0.6 * math.exp(-0.3 * i)
            w_in = w_attn_in[j].astype(BF16)
            wo = w_attn_out[j].astype(BF16)
            lam_p = jnp.stack([lambda_q1[j], lambda_k1[j], lambda_q2[j], lambda_k2[j]])
            g_sub = row(subln_g[j])

            outs = _attn_in_proj(xp, g_mix, w_in, cos_p, sin_p, stacked_p,
                                 seg_len=s, keep=keep, layer=j, n_layers=n_attn, native=True)
            qa, ka, va_t, qb, kb, vb_t = outs[:6]
            qa, ka, qb, kb = (o.reshape(b, s, WIDTH) for o in (qa, ka, qb, kb))
            stacked_p = outs[6:]
            oa = _band_attn(qa, ka, va_t, band_bias, j)
            ob = _diff_attn(qb, kb, vb_t, lam_p, g_sub, lam_init)
            parts_p = [oa.reshape(b * s, WIDTH), ob.reshape(b * s, WIDTH)]

            outs = _attn_in_proj(xs, g_mix, w_in, cos_s, sin_s, stacked_s,
                                 seg_len=t, keep=t, layer=j, n_layers=n_attn, native=False)
            stacked_s = outs[6:]
            oa, ob = _sample_attn(
                *outs[:6], cak_t, cav_t, cbk_t, cbv_rows,
                sample_bias[j, :, :a_past], sample_bias[j, :, a_past:a_past + t],
                lam_p, g_sub, lam_init, t, j)
            parts_s = [oa, ob]
        else:
            w_in = w_conv_in[j].astype(BF16)
            wo = w_conv_out[j].astype(BF16)
            yp, tail = _conv_in(xp, g_mix, w_in, conv_w[j], jnp.zeros((b, 8, d), F32), seg_len=s)
            pcs.append(tail[:, 6:])
            st = state_conv[j]
            stand_in = jnp.stack([jnp.pad(st[:, 1:2], ((0, 0), (0, t - 1), (0, 0))),
                                  jnp.pad(st, ((0, 0), (0, t - 2), (0, 0)))]).reshape(2, bs * t, d)
            ys, xin = _conv_in(xs, g_mix, w_in, conv_w[j], stand_in, seg_len=t)
            scs.append(xin.reshape(bs, t, d)[:, t - 2:])
            parts_p, parts_s = [yp], [ys]

        ffn = (wo, row(norm_ffn[i]), w_ffn_gate[i].astype(BF16), w_ffn_up[i].astype(BF16),
               w_ffn_down[i].astype(BF16))
        g_final = row(norm_final) if i == depth - 1 else None
        xp = _mix_out_ffn(xp, parts_p, *ffn, g_final=g_final)
        xs = _mix_out_ffn(xs, parts_s, *ffn, g_final=g_final)

    pak, pav, pbk, pbv = stacked_p
    sak, sav, sbk, sbv = stacked_s
    heads_a = lambda z: z.reshape(n_attn, b, N_HEADS_A, HEAD_DIM, keep).transpose(0, 1, 4, 2, 3)
    return (xp.reshape(b, s, d), xs.reshape(bs, t, d),
            heads_a(pak), heads_a(pav),
            pbk.reshape(n_attn, b, N_HEADS_B, 2, HEAD_DIM, s).transpose(0, 1, 5, 2, 3, 4),
            pbv.reshape(n_attn, b, s, N_HEADS_B, 2 * HEAD_DIM),
            jnp.stack(pcs),
            sak.reshape(n_attn, bs, t, N_HEADS_A, HEAD_DIM), sav.reshape(n_attn, bs, t, N_HEADS_A, HEAD_DIM),
            sbk.reshape(n_attn, bs, t, N_HEADS_B, 2, HEAD_DIM), sbv.reshape(n_attn, bs, t, N_HEADS_B, 2 * HEAD_DIM),
            jnp.stack(scs))
```

```python
import functools
import math

import jax
import jax.numpy as jnp
from jax import lax
from jax.experimental import pallas as pl
from jax.experimental.pallas import tpu as pltpu

F32 = jnp.float32
BF16 = jnp.bfloat16

EPS = 1e-6
NEG = -1e30
LOG2E = 1.4426950408889634
ROPE_THETA = 10000.0
CHUNK = 64
HEAD_DIM = 64
N_HEADS_A = 8
N_HEADS_B = 4
BAND_PAST = 512
MAX_REL = 128
BAND_KEYS = BAND_PAST + CHUNK
BAND_Q = 256
BAND_WIN = BAND_Q + BAND_PAST
DIFF_KEYS = 512
BIAS_LANES = 640
WIDTH = N_HEADS_A * HEAD_DIM
LANES = 128
ROW_TILE = 512
FF_CHUNK = 256
VMEM_LIMIT = 56 * 1024 * 1024

_NT = (((1,), (1,)), ((), ()))
_TN = (((0,), (0,)), ((), ()))
_NN = (((1,), (0,)), ((), ()))


def _resident(shape):
    return pl.BlockSpec(shape, lambda *_: (0,) * len(shape), pipeline_mode=pl.Buffered(1))


def _layer_of(stack, layer):
    return pl.BlockSpec((None,) + stack.shape[1:], lambda *_: (layer, 0, 0), pipeline_mode=pl.Buffered(1))


def _rms(x, g):
    return x * lax.rsqrt(jnp.mean(x * x, axis=-1, keepdims=True) + EPS) * g


def _params(*sem):
    return pltpu.CompilerParams(dimension_semantics=sem, vmem_limit_bytes=VMEM_LIMIT)


SUM_ROWS = 16


def _values_and_sum(v_t, p):
    ones = jnp.ones((SUM_ROWS, v_t.shape[1]), v_t.dtype)
    return jnp.dot(jnp.concatenate([v_t, ones], axis=0), p, preferred_element_type=F32)


TABLE_LANES = 384


def _table_lookup(rows, idx):
    lane = idx & (LANES - 1)
    group = idx >> 7
    out = None
    for g in range(TABLE_LANES // LANES):
        picked = jnp.take_along_axis(rows[:, g * LANES:(g + 1) * LANES], lane, axis=1)
        out = picked if out is None else jnp.where(group == g, picked, out)
    return out


def _split_maps(q):
    low = lax.broadcasted_iota(jnp.int32, q.shape, 1) < HEAD_DIM
    zero = jnp.zeros_like(q)
    return jnp.concatenate([jnp.where(low, q, zero), jnp.where(low, zero, q)], axis=0)


def _attn_in_kernel(*refs, native):
    x_ref, g_ref, w_ref, cos_ref, sin_ref = refs[:5]
    (qa_ref, ka_ref, va_ref, qb_ref, kb_ref, vb_ref,
     kaf_ref, vaf_ref, kbf_ref, vbf_ref) = refs[-10:]
    q_scale = LOG2E / math.sqrt(HEAD_DIM)
    tm = x_ref.shape[0]
    halves = [(0, tm // 2), (tm // 2, tm)] if tm % 32 == 0 else [(0, tm)]
    normed = [_rms(x_ref[lo:hi, :], g_ref[...]).astype(BF16) for lo, hi in halves]

    for (lo, hi), h in zip(halves, normed):
        def proj(c):
            return jnp.dot(h, w_ref[:, c * WIDTH:(c + 1) * WIDTH], preferred_element_type=F32)

        cos = cos_ref[lo:hi, :]
        sin = sin_ref[lo:hi, :]
        low_half = (lax.broadcasted_iota(jnp.int32, cos.shape, 1) % HEAD_DIM) < (HEAD_DIM // 2)

        def rotary(z):
            parts = []
            for j in range(WIDTH // LANES):
                zj = z[:, j * LANES:(j + 1) * LANES]
                swapped = jnp.where(low_half, pltpu.roll(zj, LANES - HEAD_DIM // 2, 1),
                                    pltpu.roll(zj, HEAD_DIM // 2, 1))
                parts.append(zj * cos + swapped * sin)
            return jnp.concatenate(parts, axis=1)

        vb = proj(5)
        if native:
            vb_ref[0, 0, :, lo:hi] = vb.T.astype(BF16)
            for head in range(N_HEADS_B):
                vbf_ref[pl.ds(lo * N_HEADS_B + head, hi - lo, stride=N_HEADS_B), :] = (
                    vb[:, head * LANES:(head + 1) * LANES])
        else:
            vb_ref[lo:hi, :] = vb.astype(BF16)
            vbf_ref[lo:hi, :] = vb
        kb = rotary(proj(4))
        kb_ref[lo:hi, :] = kb.astype(BF16)
        va = proj(2)
        ka = proj(1)
        ka_ref[lo:hi, :] = ka.astype(BF16)
        if native:
            kbf_ref[0, 0, :, lo:hi] = kb.T
            va_t = va.T
            vaf_ref[0, 0, :, lo:hi] = va_t
            va_ref[0, :, lo:hi] = va_t.astype(BF16)
            kaf_ref[0, 0, :, lo:hi] = ka.T
        else:
            kbf_ref[lo:hi, :] = kb
            vaf_ref[lo:hi, :] = va
            va_ref[lo:hi, :] = va.astype(BF16)
            kaf_ref[lo:hi, :] = ka
        qb_ref[lo:hi, :] = (rotary(proj(3)) * q_scale).astype(BF16)
        qa_ref[lo:hi, :] = (proj(0) * q_scale).astype(BF16)


def _attn_in_proj(x, g, w, cos_t, sin_t, stacked, *, seg_len, keep, layer, n_layers, native):
    m, d = x.shape
    tm = min(ROW_TILE, m)
    n_tiles = m // tm
    n_tab = cos_t.shape[0] // tm
    n_seg = m // seg_len
    row = lambda i: (i, 0)
    tab = lambda i: (i % n_tab, 0)
    blk = pl.BlockSpec((tm, WIDTH), row)
    bf_out = jax.ShapeDtypeStruct((m, WIDTH), BF16)
    if native:
        tps, kpt = seg_len // tm, keep // tm
        t_blk = (1, 1, WIDTH, tm)
        keep_spec = pl.BlockSpec(
            t_blk, lambda i: (layer, i // tps, 0, jnp.clip(i % tps - (tps - kpt), 0, kpt - 1)))
        keep_out = jax.ShapeDtypeStruct((n_layers, n_seg, WIDTH, keep), F32)
        kb_spec = pl.BlockSpec(t_blk, lambda i: (layer, i // tps, 0, i % tps))
        kb_out = jax.ShapeDtypeStruct((n_layers, n_seg, WIDTH, seg_len), F32)
        vb_spec = pl.BlockSpec((tm * N_HEADS_B, LANES), lambda i: (layer * n_tiles + i, 0))
        vb_out = jax.ShapeDtypeStruct((n_layers * m * N_HEADS_B, LANES), F32)
        va_spec = pl.BlockSpec((1, WIDTH, tm), lambda i: (i // tps, 0, i % tps))
        va_out = jax.ShapeDtypeStruct((n_seg, WIDTH, seg_len), BF16)
        vbb_spec = pl.BlockSpec((1, 1, WIDTH, tm), lambda i: (i // tps, i % tps, 0, 0))
        vbb_out = jax.ShapeDtypeStruct((n_seg, tps, WIDTH, tm), BF16)
    else:
        assert keep == seg_len
        keep_spec = kb_spec = vb_spec = pl.BlockSpec((tm, WIDTH), lambda i: (layer * n_tiles + i, 0))
        keep_out = kb_out = vb_out = jax.ShapeDtypeStruct((n_layers * m, WIDTH), F32)
        va_spec = vbb_spec = blk
        va_out = vbb_out = bf_out
    carried = [] if stacked is None else list(stacked)
    n_in = 5
    return pl.pallas_call(
        functools.partial(_attn_in_kernel, native=native),
        grid=(n_tiles,),
        in_specs=[pl.BlockSpec((tm, d), row), _resident((1, d)), _layer_of(*w),
                  pl.BlockSpec((tm, LANES), tab), pl.BlockSpec((tm, LANES), tab)]
                 + [pl.BlockSpec(memory_space=pl.ANY)] * len(carried),
        out_specs=[blk, blk, va_spec, blk, blk, vbb_spec, keep_spec, keep_spec, kb_spec, vb_spec],
        out_shape=[bf_out, bf_out, va_out, bf_out, bf_out, vbb_out, keep_out, keep_out, kb_out, vb_out],
        input_output_aliases={n_in + k: 6 + k for k in range(len(carried))},
        compiler_params=_params("arbitrary"),
        name="attn_in_proj",
    )(x, g, w[0], cos_t, sin_t, *carried)


def _rope_tables(pos):
    half = HEAD_DIM // 2
    inv = ROPE_THETA ** (-jnp.arange(half, dtype=F32) / half)
    ang = pos.astype(F32)[:, None] * inv[None, :]
    c, s = jnp.cos(ang), jnp.sin(ang)
    reps = LANES // HEAD_DIM
    return jnp.tile(c, (1, 2 * reps)), jnp.tile(jnp.concatenate([-s, s], axis=1), (1, reps))


def _band_bias_kernel(tab_ref, o_ref):
    r = pl.program_id(0)
    e_shape = (BAND_WIN + LANES, LANES)
    y = lax.broadcasted_iota(jnp.int32, e_shape, 0) - LANES
    lane = lax.broadcasted_iota(jnp.int32, e_shape, 1)
    idx = jnp.clip(BAND_PAST + lane - y, -MAX_REL, MAX_REL) + MAX_REL
    cols = []
    for hh in range(2):
        e = _table_lookup(jnp.broadcast_to(tab_ref[pl.ds(2 * r + hh, 1), :], (e_shape[0], TABLE_LANES)), idx)
        for c in range(BAND_Q // LANES):
            cols.append(e[LANES - LANES * c:LANES - LANES * c + BAND_WIN])
    bias = jnp.concatenate(cols, axis=1)

    kj = lax.broadcasted_iota(jnp.int32, bias.shape, 0)
    qi = lax.broadcasted_iota(jnp.int32, bias.shape, 1) % BAND_Q
    first_key = (qi // CHUNK) * CHUNK
    in_band = jnp.logical_and(kj >= first_key, kj < first_key + BAND_KEYS)
    o_ref[0] = jnp.where(in_band, bias * LOG2E, NEG)


def _band_bias(table):
    n = table.shape[0] // 2
    return pl.pallas_call(
        _band_bias_kernel,
        grid=(n,),
        in_specs=[_resident(table.shape)],
        out_specs=pl.BlockSpec((1, BAND_WIN, 2 * BAND_Q), lambda r: (r, 0, 0)),
        out_shape=jax.ShapeDtypeStruct((n, BAND_WIN, 2 * BAND_Q), F32),
        compiler_params=_params("arbitrary"),
        name="band_bias",
    )(table)


def _sample_bias_kernel(tab_ref, o_ref, *, t):
    rows = tab_ref[0]
    blocks = []
    for c in range(BIAS_LANES // LANES):
        shape = (rows.shape[0], LANES)
        frame = lax.broadcasted_iota(jnp.int32, shape, 0) % t
        kj = lax.broadcasted_iota(jnp.int32, shape, 1) + c * LANES
        idx = jnp.clip(BAND_PAST + frame - kj, -MAX_REL, MAX_REL) + MAX_REL
        blocks.append(_table_lookup(rows, idx))
    o_ref[0] = jnp.concatenate(blocks, axis=1) * LOG2E


def _sample_bias(table, t):
    n = table.shape[0] // N_HEADS_A
    per_row = jnp.repeat(table.reshape(n, N_HEADS_A, TABLE_LANES), t, axis=1)
    return pl.pallas_call(
        functools.partial(_sample_bias_kernel, t=t),
        grid=(n,),
        in_specs=[pl.BlockSpec((1, N_HEADS_A * t, TABLE_LANES), lambda r: (r, 0, 0))],
        out_specs=pl.BlockSpec((1, N_HEADS_A * t, BIAS_LANES), lambda r: (r, 0, 0)),
        out_shape=jax.ShapeDtypeStruct((n, N_HEADS_A * t, BIAS_LANES), F32),
        compiler_params=_params("arbitrary"),
        name="sample_bias",
    )(per_row)


def _band_kernel(q_ref, kp_ref, kc_ref, vp_ref, vc_ref, bias_ref, o_ref, *bufs):
    qblk = pl.program_id(2)
    n_sub = q_ref.shape[1] // BAND_Q

    def window(prev_ref, cur_ref, sub, first, axis):
        lo, hi = sub * BAND_Q, (sub + 1) * BAND_Q
        take = lambda ref, a, b: ref[0, a:b] if axis == 0 else ref[0, :, a:b]
        parts = []
        if lo < BAND_PAST and not first:
            parts.append(take(prev_ref, lo, BAND_PAST))
        parts.append(take(cur_ref, max(lo - BAND_PAST, 0), hi))
        return parts[0] if len(parts) == 1 else jnp.concatenate(parts, axis=axis)

    def scores(sub, first):
        keys = window(kp_ref, kc_ref, sub, first, 0)
        nk = keys.shape[0]
        q = q_ref[0, sub * BAND_Q:(sub + 1) * BAND_Q, :]
        s = lax.dot_general(keys, _split_maps(q), _NT, preferred_element_type=F32)
        s = s + bias_ref[0, BAND_WIN - nk:, :]
        bufs[2 * sub][:nk] = s
        bufs[2 * sub + 1][...] = jnp.max(s, axis=0, keepdims=True)
        return nk

    def finish(nk, sub, first):
        p = jnp.exp2(bufs[2 * sub][:nk] - bufs[2 * sub + 1][...]).astype(BF16)
        o_t = _values_and_sum(window(vp_ref, vc_ref, sub, first, 1), p)
        o_t = o_t[:LANES] / o_t[LANES:LANES + 1]
        o = jnp.concatenate([o_t[:HEAD_DIM, :BAND_Q], o_t[HEAD_DIM:, BAND_Q:]], axis=0).T
        o_ref[0, sub * BAND_Q:(sub + 1) * BAND_Q, :] = o.astype(BF16)

    def run(first):
        sizes = [scores(sub, first) for sub in range(n_sub)]
        for sub in range(n_sub):
            finish(sizes[sub], sub, first)

    pl.when(qblk == 0)(functools.partial(run, True))
    pl.when(qblk > 0)(functools.partial(run, False))


def _band_attn(q, k, v_t, bias, layer):
    b, s, _ = q.shape
    bq = 2 * BAND_PAST if s % (2 * BAND_PAST) == 0 else BAND_PAST
    n_pairs = N_HEADS_A // 2
    prev_of = lambda qi: jnp.maximum(qi * (bq // BAND_PAST) - 1, 0)
    cur = lambda hp, bi, qi: (bi, qi, hp)
    prev = lambda hp, bi, qi: (bi, prev_of(qi), hp)
    cur_t = lambda hp, bi, qi: (bi, hp, qi)
    prev_t = lambda hp, bi, qi: (bi, hp, prev_of(qi))
    blk = (1, bq, LANES)
    return pl.pallas_call(
        _band_kernel,
        grid=(n_pairs, b, s // bq),
        in_specs=[pl.BlockSpec(blk, cur), pl.BlockSpec((1, BAND_PAST, LANES), prev), pl.BlockSpec(blk, cur),
                  pl.BlockSpec((1, LANES, BAND_PAST), prev_t), pl.BlockSpec((1, LANES, bq), cur_t),
                  pl.BlockSpec((1, BAND_WIN, 2 * BAND_Q), lambda hp, bi, qi: (layer * n_pairs + hp, 0, 0))],
        out_specs=pl.BlockSpec(blk, cur),
        out_shape=jax.ShapeDtypeStruct((b, s, WIDTH), BF16),
        scratch_shapes=[pltpu.VMEM((BAND_WIN, 2 * BAND_Q), F32), pltpu.VMEM((1, 2 * BAND_Q), F32)]
                       * (bq // BAND_Q),
        compiler_params=_params("arbitrary", "arbitrary", "arbitrary"),
        name="band_attn",
    )(q, k, k, v_t, v_t, bias)


def _lambda(lam_ref, lam_init):
    lp = lam_ref[...]
    a1 = jnp.sum(lp[0:1] * lp[1:2], axis=-1, keepdims=True)
    a2 = jnp.sum(lp[2:3] * lp[3:4], axis=-1, keepdims=True)
    return jnp.exp(a1) - jnp.exp(a2) + lam_init


def _diff_kernel(lam_ref, g_ref, q_ref, k_ref, v_ref, o_ref, m_ref, acc_ref, *bufs, lam_init):
    qi = pl.program_id(2)
    q = q_ref[0]
    bq = q.shape[0]
    bk = bq // 2
    q2 = _split_maps(q)
    queries = (q2[:bq], q2[bq:])

    m_ref[...] = jnp.full(m_ref.shape, NEG, F32)
    acc_ref[...] = jnp.zeros(acc_ref.shape, F32)

    def block(ref, kj):
        return ref[0, pl.ds(pl.multiple_of(kj * bk, bk), bk), :]

    def scores(kj, stream, buf, diagonal=None):
        s_ref, mx_ref = bufs[4 * stream + 2 * buf], bufs[4 * stream + 2 * buf + 1]
        qs = queries[stream][bk:] if diagonal == 1 else queries[stream]
        s = lax.dot_general(block(k_ref, kj), qs, _NT, preferred_element_type=F32)
        if diagonal is not None:
            k_chunk = lax.broadcasted_iota(jnp.int32, s.shape, 0) // CHUNK
            q_chunk = lax.broadcasted_iota(jnp.int32, s.shape, 1) // CHUNK
            s = jnp.where(k_chunk <= q_chunk, s, NEG)
        n = s.shape[1]
        s_ref[:, :n] = s
        mx_ref[:, :n] = jnp.max(s, axis=0, keepdims=True)

    def absorb(kj, stream, buf, second_half=False):
        s_ref, mx_ref = bufs[4 * stream + 2 * buf], bufs[4 * stream + 2 * buf + 1]
        n = bk if second_half else bq
        cols = slice((stream + 1) * bq - n, (stream + 1) * bq)
        m_old = m_ref[:, cols]
        m_new = jnp.maximum(m_old, mx_ref[:, :n])
        alpha = jnp.exp2(m_old - m_new)
        p = jnp.exp2(s_ref[:, :n] - m_new).astype(BF16)
        acc_ref[:, cols] = alpha * acc_ref[:, cols] + _values_and_sum(v_ref[0, kj], p)
        m_ref[:, cols] = m_new

    def stage(kj, buf, look=True, look_diagonal=None, second_half=False):
        if look:
            scores(kj + 1, 0, 1 - buf, look_diagonal)
        absorb(kj, 1, buf, second_half)
        if look:
            scores(kj + 1, 1, 1 - buf, look_diagonal)
        absorb(kj, 0, buf, second_half)

    @pl.when(qi == 0)
    def _():
        scores(0, 0, 0, diagonal=0)
        scores(0, 1, 0, diagonal=0)
        stage(0, 0, look_diagonal=1)
        stage(1, 1, look=False, second_half=True)

    @pl.when(qi > 0)
    def _():
        scores(0, 0, 0)
        scores(0, 1, 0)

    def two_blocks(pair, carry):
        stage(2 * pair, 0)
        stage(2 * pair + 1, 1)
        return carry

    lax.fori_loop(0, jnp.maximum(qi - 1, 0), two_blocks, 0)

    @pl.when(qi > 0)
    def _():
        last = 2 * qi
        stage(last - 2, 0)
        stage(last - 1, 1, look_diagonal=0)
        stage(last, 0, look_diagonal=1)
        stage(last + 1, 1, look=False, second_half=True)

    o_t = acc_ref[:LANES] / acc_ref[LANES:LANES + 1]
    lam = _lambda(lam_ref, lam_init)
    d_t = o_t[:, :bq] - lam * o_t[:, bq:]
    inv = lax.rsqrt(jnp.mean(d_t * d_t, axis=0, keepdims=True) + EPS)
    o_ref[0] = ((d_t * inv).T * (g_ref[...] * (1.0 - lam_init))).astype(BF16)


def _diff_attn(q, k, v_t, lam_p, g, lam_init):
    b, s, _ = q.shape
    bq = 2 * DIFF_KEYS
    assert s % bq == 0 and v_t.shape[3] == DIFF_KEYS
    seq = pl.BlockSpec((1, s, LANES), lambda bi, h, qi: (bi, 0, h))
    seq_t = pl.BlockSpec((1, s // DIFF_KEYS, LANES, DIFF_KEYS), lambda bi, h, qi: (bi, 0, h, 0))
    blk = pl.BlockSpec((1, bq, LANES), lambda bi, h, qi: (bi, qi, h))
    return pl.pallas_call(
        functools.partial(_diff_kernel, lam_init=lam_init),
        grid=(b, N_HEADS_B, s // bq),
        in_specs=[_resident(lam_p.shape), _resident(g.shape), blk, seq, seq_t],
        out_specs=blk,
        out_shape=jax.ShapeDtypeStruct((b, s, WIDTH), BF16),
        scratch_shapes=[pltpu.VMEM((1, 2 * bq), F32), pltpu.VMEM((LANES + SUM_ROWS, 2 * bq), F32),
                        ] + [pltpu.VMEM((DIFF_KEYS, bq), F32), pltpu.VMEM((1, bq), F32)] * 4,
        compiler_params=_params("arbitrary", "arbitrary", "arbitrary"),
        name="diff_attn",
    )(lam_p, g, q, k, v_t)


def _sample_kernel(lam_ref, g_ref, qa_ref, kan_ref, van_ref, qb_ref, kbn_ref, vbn_ref,
                   cak_ref, cav_ref, cbk_ref, cbv_ref, bias_c_ref, bias_n_ref,
                   oa_ref, ob_ref, *, lam_init):
    t = qa_ref.shape[0]
    n_grp = WIDTH // HEAD_DIM
    rows = n_grp * t
    grp_of_row = lax.broadcasted_iota(jnp.int32, (rows, WIDTH), 0) // t
    grp_of_lane = lax.broadcasted_iota(jnp.int32, (rows, WIDTH), 1) // HEAD_DIM
    own = grp_of_row == grp_of_lane
    lane_t = lax.broadcasted_iota(jnp.int32, (t, WIDTH), 1)

    def expand(q):
        qe = jnp.concatenate([q] * n_grp, axis=0)
        return jnp.where(own, qe, jnp.zeros_like(qe))

    def attend(qe, kt_cache, k_new, v_cache, v_new, bias_c, bias_n, v_dims):
        s_c = jnp.dot(qe, kt_cache, preferred_element_type=F32)
        s_n = lax.dot_general(qe, k_new, _NT, preferred_element_type=F32)
        if bias_c is not None:
            s_c = s_c + bias_c
            s_n = s_n + bias_n
        m = jnp.maximum(jnp.max(s_c, axis=-1, keepdims=True), jnp.max(s_n, axis=-1, keepdims=True))
        p_c = jnp.exp2(s_c - m)
        p_n = jnp.exp2(s_n - m)
        l = jnp.sum(p_c, axis=-1, keepdims=True) + jnp.sum(p_n, axis=-1, keepdims=True)
        o = (lax.dot_general(p_c.astype(BF16), v_cache, v_dims, preferred_element_type=F32)
             + jnp.dot(p_n.astype(BF16), v_new, preferred_element_type=F32))
        return o / l

    oa_all = attend(expand(qa_ref[...]), cak_ref[0, 0].astype(BF16), kan_ref[...],
                    cav_ref[0, 0].astype(BF16), van_ref[...], bias_c_ref[...], bias_n_ref[...], _NT)
    oa = jnp.zeros((t, WIDTH), F32)
    for h in range(N_HEADS_A):
        oa = jnp.where(lane_t // HEAD_DIM == h, oa_all[h * t:(h + 1) * t], oa)
    oa_ref[...] = oa.astype(BF16)

    past = cbk_ref.shape[3]
    v_cache = jnp.concatenate([cbv_ref[pl.ds(h, past, stride=N_HEADS_B), :] for h in range(N_HEADS_B)],
                              axis=1).astype(BF16)
    ob_all = attend(expand(qb_ref[...]), cbk_ref[0, 0].astype(BF16), kbn_ref[...],
                    v_cache, vbn_ref[...], None, None, _NN)
    lam = _lambda(lam_ref, lam_init)
    g = g_ref[...]
    outs = []
    for h in range(N_HEADS_B):
        d = ob_all[2 * h * t:(2 * h + 1) * t] - lam * ob_all[(2 * h + 1) * t:(2 * h + 2) * t]
        outs.append(_rms(d[:, h * LANES:(h + 1) * LANES], g) * (1.0 - lam_init))
    ob_ref[...] = jnp.concatenate(outs, axis=1).astype(BF16)


def _sample_attn(qa, ka, va, qb, kb, vb, cak, cav, cbk, cbv, bias_c, bias_n, lam_p, g, lam_init, t, layer):
    m = qa.shape[0]
    bs = m // t
    past = cbk.shape[3]
    new = pl.BlockSpec((t, WIDTH), lambda i: (i, 0))
    cache = lambda c: pl.BlockSpec((1, 1) + c.shape[2:], lambda i: (layer, i, 0, 0))
    cbv_spec = pl.BlockSpec((past * N_HEADS_B, LANES), lambda i: (layer * bs + i, 0))
    out = jax.ShapeDtypeStruct((m, WIDTH), BF16)
    return pl.pallas_call(
        functools.partial(_sample_kernel, lam_init=lam_init),
        grid=(bs,),
        in_specs=[_resident(lam_p.shape), _resident(g.shape)] + [new] * 6
                 + [cache(cak), cache(cav), cache(cbk), cbv_spec,
                    _resident(bias_c.shape), _resident(bias_n.shape)],
        out_specs=[new, new],
        out_shape=[out, out],
        compiler_params=_params("arbitrary"),
        name="sample_attn",
    )(lam_p, g, qa, ka, va, qb, kb, vb, cak, cav, cbk, cbv, bias_c, bias_n)


def _conv_kernel(x_ref, g_ref, w_ref, cw_ref, st_ref, y_ref, tail_ref, carry_ref, *, tps, seg_len):
    tm, d = x_ref.shape
    h = _rms(x_ref[...], g_ref[...]).astype(BF16)
    carried = seg_len >= tm
    if carried:
        @pl.when(pl.program_id(0) % tps == 0)
        def _():
            carry_ref[...] = st_ref[0]

    cw = cw_ref[...]
    cc = 2 * FF_CHUNK
    for c in range(d // cc):
        cols = slice(c * cc, (c + 1) * cc)
        proj = lambda part: jnp.dot(h, w_ref[:, part * d + c * cc:part * d + (c + 1) * cc],
                                    preferred_element_type=F32)
        xin = proj(1) * proj(2)
        r = lax.broadcasted_iota(jnp.int32, xin.shape, 0)
        x1 = pltpu.roll(xin, 1, 0)
        x2 = pltpu.roll(xin, 2, 0)
        if carried:
            prev = carry_ref[:, cols]
            x1 = jnp.where(r == 0, prev[7:8], x1)
            x2 = jnp.where(r == 0, prev[6:7], jnp.where(r == 1, prev[7:8], x2))
            carry_ref[:, cols] = xin[tm - 8:]
            tail_ref[0, :, cols] = xin[tm - 8:]
        else:
            t = r % seg_len
            x1 = jnp.where(t >= 1, x1, st_ref[0, :, cols])
            x2 = jnp.where(t >= 2, x2, st_ref[1, :, cols])
            tail_ref[:, cols] = xin
        conv = cw[0:1, cols] * x2 + cw[1:2, cols] * x1 + cw[2:3, cols] * xin
        y_ref[:, cols] = (proj(0) * conv).astype(BF16)


def _conv_in(x, g, w, cw, state, *, seg_len):
    m, d = x.shape
    tm = min(ROW_TILE, m)
    row = lambda i: (i, 0)
    if seg_len >= tm:
        tps = seg_len // tm
        st_spec = pl.BlockSpec((1, 8, d), lambda i: (i // tps, 0, 0))
        tail_spec = pl.BlockSpec((1, 8, d), lambda i: (i // tps, 0, 0))
        tail_shape = jax.ShapeDtypeStruct((m // seg_len, 8, d), F32)
    else:
        tps = 1
        st_spec = pl.BlockSpec((2, tm, d), lambda i: (0, i, 0))
        tail_spec = pl.BlockSpec((tm, d), row)
        tail_shape = jax.ShapeDtypeStruct((m, d), F32)
    return pl.pallas_call(
        functools.partial(_conv_kernel, tps=tps, seg_len=seg_len),
        grid=(m // tm,),
        in_specs=[pl.BlockSpec((tm, d), row), _resident((1, d)), _layer_of(*w),
                  _resident(cw.shape), st_spec],
        out_specs=[pl.BlockSpec((tm, d), row), tail_spec],
        out_shape=[jax.ShapeDtypeStruct((m, d), BF16), tail_shape],
        scratch_shapes=[pltpu.VMEM((8, d), F32)],
        compiler_params=_params("arbitrary"),
        name="conv_in",
    )(x, g, w[0], cw, state)


def _ffn_kernel(*refs, n_parts, final):
    x_ref = refs[0]
    a_refs = refs[1:1 + n_parts]
    wo_ref, g_ref, wg_ref, wu_ref, wd_ref = refs[1 + n_parts:6 + n_parts]
    gf_ref = refs[6 + n_parts] if final else None
    o_ref, act_ref = refs[-2], refs[-1]

    tm = x_ref.shape[0]
    halves = [slice(0, tm // 2), slice(tm // 2, tm)] if tm % 32 == 0 else [slice(0, tm)]
    x1s = []
    for rows in halves:
        a = jnp.concatenate([r[rows, :] for r in a_refs], axis=1) if n_parts > 1 else a_refs[0][rows, :]
        x1s.append(x_ref[rows, :] + jnp.dot(a, wo_ref[...], preferred_element_type=F32))
    d_ff = wg_ref.shape[1]
    for rows, x1 in zip(halves, x1s):
        h = _rms(x1, g_ref[...]).astype(BF16)
        for c in range(d_ff // FF_CHUNK):
            cols = slice(c * FF_CHUNK, (c + 1) * FF_CHUNK)
            gate = jnp.dot(h, wg_ref[:, cols], preferred_element_type=F32)
            up = jnp.dot(h, wu_ref[:, cols], preferred_element_type=F32)
            act_ref[rows, cols] = (gate * (1.0 / (1.0 + jnp.exp(-gate))) * up).astype(BF16)
        y = x1 + jnp.dot(act_ref[rows, :], wd_ref[...], preferred_element_type=F32)
        o_ref[rows, :] = _rms(y, gf_ref[...]) if final else y


def _mix_out_ffn(x, parts, wo, g, wg, wu, wd, g_final=None):
    m, d = x.shape
    tm = min(ROW_TILE, m)
    row = lambda i: (i, 0)
    final = g_final is not None
    extra = [g_final] if final else []
    return pl.pallas_call(
        functools.partial(_ffn_kernel, n_parts=len(parts), final=final),
        grid=(m // tm,),
        in_specs=[pl.BlockSpec((tm, d), row)] + [pl.BlockSpec((tm, p.shape[1]), row) for p in parts]
                 + [_layer_of(*wo), _resident((1, d)), _layer_of(*wg), _layer_of(*wu), _layer_of(*wd)]
                 + [_resident((1, d))] * len(extra),
        out_specs=pl.BlockSpec((tm, d), row),
        out_shape=jax.ShapeDtypeStruct((m, d), F32),
        scratch_shapes=[pltpu.VMEM((tm, wg[0].shape[2]), BF16)],
        compiler_params=_params("arbitrary"),
        name="mix_out_ffn",
    )(x, *parts, wo[0], g, wg[0], wu[0], wd[0], *extra)


def kernel(x_prompt, x_sample, cache_a_k, cache_a_v, cache_b_k, cache_b_v, state_conv, norm_mix, norm_ffn, norm_final, w_attn_in, w_attn_out, rel_bias, lambda_q1, lambda_k1, lambda_q2, lambda_k2, subln_g, w_conv_in, conv_w, w_conv_out, w_ffn_gate, w_ffn_up, w_ffn_down):
    b, s, d = x_prompt.shape
    bs, t, _ = x_sample.shape
    depth = norm_mix.shape[0]
    n_attn = w_attn_in.shape[0]
    n_conv = w_conv_in.shape[0]
    past = cache_b_k.shape[2]
    a_past = cache_a_k.shape[2]
    keep = min(BAND_PAST, s)
    assert s % BAND_PAST == 0 and a_past == BAND_PAST and d % (2 * FF_CHUNK) == 0
    assert w_ffn_gate.shape[2] % FF_CHUNK == 0 and 2 <= t and a_past + t <= BIAS_LANES

    xp = x_prompt.reshape(b * s, d)
    xs = x_sample.reshape(bs * t, d)
    row = lambda v: v.reshape(1, -1)

    cos_p, sin_p = _rope_tables(jnp.arange(s))
    cos_s, sin_s = (jnp.tile(tab, (bs, 1)) for tab in _rope_tables(past + jnp.arange(t)))

    table = jnp.pad(rel_bias.reshape(n_attn * N_HEADS_A, 2 * MAX_REL + 1),
                    ((0, 0), (0, TABLE_LANES - 2 * MAX_REL - 1)))
    band_bias = _band_bias(table)
    sample_bias = _sample_bias(table, t)

    cak_t = cache_a_k.transpose(0, 1, 3, 4, 2).reshape(n_attn, bs, WIDTH, a_past)
    cav_t = cache_a_v.transpose(0, 1, 3, 4, 2).reshape(n_attn, bs, WIDTH, a_past)
    cbk_t = cache_b_k.transpose(0, 1, 3, 4, 5, 2).reshape(n_attn, bs, WIDTH, past)
    cbv_rows = cache_b_v.reshape(n_attn * bs * past * N_HEADS_B, LANES)

    attn_in_w, attn_out_w = w_attn_in.astype(BF16), w_attn_out.astype(BF16)
    conv_in_w, conv_out_w = w_conv_in.astype(BF16), w_conv_out.astype(BF16)
    gate_w, up_w, down_w = w_ffn_gate.astype(BF16), w_ffn_up.astype(BF16), w_ffn_down.astype(BF16)

    stacked_p = stacked_s = None
    pcs, scs = [], []
    for i in range(depth):
        j = i // 2
        g_mix = row(norm_mix[i])
        if i % 2 == 0:
            lam_init = 0.8 - 0.6 * math.exp(-0.3 * i)
            w_in = (attn_in_w, j)
            wo = (attn_out_w, j)
            lam_p = jnp.stack([lambda_q1[j], lambda_k1[j], lambda_q2[j], lambda_k2[j]])
            g_sub = row(subln_g[j])

            outs = _attn_in_proj(xp, g_mix, w_in, cos_p, sin_p, stacked_p,
                                 seg_len=s, keep=keep, layer=j, n_layers=n_attn, native=True)
            qa, ka, va_t, qb, kb, vb_t = outs[:6]
            qa, ka, qb, kb = (o.reshape(b, s, WIDTH) for o in (qa, ka, qb, kb))
            stacked_p = outs[6:]
            oa = _band_attn(qa, ka, va_t, band_bias, j)
            ob = _diff_attn(qb, kb, vb_t, lam_p, g_sub, lam_init)
            parts_p = [oa.reshape(b * s, WIDTH), ob.reshape(b * s, WIDTH)]

            outs = _attn_in_proj(xs, g_mix, w_in, cos_s, sin_s, stacked_s,
                                 seg_len=t, keep=t, layer=j, n_layers=n_attn, native=False)
            stacked_s = outs[6:]
            oa, ob = _sample_attn(
                *outs[:6], cak_t, cav_t, cbk_t, cbv_rows,
                sample_bias[j, :, :a_past], sample_bias[j, :, a_past:a_past + t],
                lam_p, g_sub, lam_init, t, j)
            parts_s = [oa, ob]
        else:
            w_in = (conv_in_w, j)
            wo = (conv_out_w, j)
            yp, tail = _conv_in(xp, g_mix, w_in, conv_w[j], jnp.zeros((b, 8, d), F32), seg_len=s)
            pcs.append(tail[:, 6:])
            st = state_conv[j]
            stand_in = jnp.stack([jnp.pad(st[:, 1:2], ((0, 0), (0, t - 1), (0, 0))),
                                  jnp.pad(st, ((0, 0), (0, t - 2), (0, 0)))]).reshape(2, bs * t, d)
            ys, xin = _conv_in(xs, g_mix, w_in, conv_w[j], stand_in, seg_len=t)
            scs.append(xin.reshape(bs, t, d)[:, t - 2:])
            parts_p, parts_s = [yp], [ys]

        ffn = (wo, row(norm_ffn[i]), (gate_w, i), (up_w, i), (down_w, i))
        g_final = row(norm_final) if i == depth - 1 else None
        xp = _mix_out_ffn(xp, parts_p, *ffn, g_final=g_final)
        xs = _mix_out_ffn(xs, parts_s, *ffn, g_final=g_final)

    pak, pav, pbk, pbv = stacked_p
    sak, sav, sbk, sbv = stacked_s
    heads_a = lambda z: z.reshape(n_attn, b, N_HEADS_A, HEAD_DIM, keep).transpose(0, 1, 4, 2, 3)
    return (xp.reshape(b, s, d), xs.reshape(bs, t, d),
            heads_a(pak), heads_a(pav),
            pbk.reshape(n_attn, b, N_HEADS_B, 2, HEAD_DIM, s).transpose(0, 1, 5, 2, 3, 4),
            pbv.reshape(n_attn, b, s, N_HEADS_B, 2 * HEAD_DIM),
            jnp.stack(pcs),
            sak.reshape(n_attn, bs, t, N_HEADS_A, HEAD_DIM), sav.reshape(n_attn, bs, t, N_HEADS_A, HEAD_DIM),
            sbk.reshape(n_attn, bs, t, N_HEADS_B, 2, HEAD_DIM), sbv.reshape(n_attn, bs, t, N_HEADS_B, 2 * HEAD_DIM),
            jnp.stack(scs))
```

```python
import functools
import math

import jax
import jax.numpy as jnp
from jax import lax
from jax.experimental import pallas as pl
from jax.experimental.pallas import tpu as pltpu

F32 = jnp.float32
BF16 = jnp.bfloat16

EPS = 1e-6
NEG = -1e30
LOG2E = 1.4426950408889634
ROPE_THETA = 10000.0
CHUNK = 64
HEAD_DIM = 64
N_HEADS_A = 8
N_HEADS_B = 4
BAND_PAST = 512
MAX_REL = 128
BAND_KEYS = BAND_PAST + CHUNK
BAND_Q = 256
BAND_WIN = BAND_Q + BAND_PAST
DIFF_KEYS = 512
BIAS_LANES = 640
WIDTH = N_HEADS_A * HEAD_DIM
LANES = 128
ROW_TILE = 512
FF_CHUNK = 256
VMEM_LIMIT = 56 * 1024 * 1024

_NT = (((1,), (1,)), ((), ()))
_TN = (((0,), (0,)), ((), ()))
_NN = (((1,), (0,)), ((), ()))


def _resident(shape):
    return pl.BlockSpec(shape, lambda *_: (0,) * len(shape), pipeline_mode=pl.Buffered(1))


def _layer_of(stack, layer):
    return pl.BlockSpec((None,) + stack.shape[1:], lambda *_: (layer, 0, 0), pipeline_mode=pl.Buffered(1))


def _rms(x, g):
    return x * lax.rsqrt(jnp.mean(x * x, axis=-1, keepdims=True) + EPS) * g


def _params(*sem):
    return pltpu.CompilerParams(dimension_semantics=sem, vmem_limit_bytes=VMEM_LIMIT)


SUM_ROWS = 16


def _values_and_sum(v_t, p):
    ones = jnp.ones((SUM_ROWS, v_t.shape[1]), v_t.dtype)
    return jnp.dot(jnp.concatenate([v_t, ones], axis=0), p, preferred_element_type=F32)


TABLE_LANES = 384


def _table_lookup(rows, idx):
    lane = idx & (LANES - 1)
    group = idx >> 7
    out = None
    for g in range(TABLE_LANES // LANES):
        picked = jnp.take_along_axis(rows[:, g * LANES:(g + 1) * LANES], lane, axis=1)
        out = picked if out is None else jnp.where(group == g, picked, out)
    return out


def _split_maps(q):
    low = lax.broadcasted_iota(jnp.int32, q.shape, 1) < HEAD_DIM
    zero = jnp.zeros_like(q)
    return jnp.concatenate([jnp.where(low, q, zero), jnp.where(low, zero, q)], axis=0)


def _attn_in_kernel(*refs, native):
    x_ref, g_ref, w_ref, cos_ref, sin_ref = refs[:5]
    (qa_ref, ka_ref, va_ref, qb_ref, kb_ref, vb_ref,
     kaf_ref, vaf_ref, kbf_ref, vbf_ref) = refs[-10:]
    q_scale = LOG2E / math.sqrt(HEAD_DIM)
    tm = x_ref.shape[0]
    halves = [(0, tm // 2), (tm // 2, tm)] if tm % 32 == 0 else [(0, tm)]
    normed = [_rms(x_ref[lo:hi, :], g_ref[...]).astype(BF16) for lo, hi in halves]

    for (lo, hi), h in zip(halves, normed):
        def proj(c):
            return jnp.dot(h, w_ref[:, c * WIDTH:(c + 1) * WIDTH], preferred_element_type=F32)

        cos = cos_ref[lo:hi, :]
        sin = sin_ref[lo:hi, :]
        low_half = (lax.broadcasted_iota(jnp.int32, cos.shape, 1) % HEAD_DIM) < (HEAD_DIM // 2)

        def rotary(z):
            parts = []
            for j in range(WIDTH // LANES):
                zj = z[:, j * LANES:(j + 1) * LANES]
                swapped = jnp.where(low_half, pltpu.roll(zj, LANES - HEAD_DIM // 2, 1),
                                    pltpu.roll(zj, HEAD_DIM // 2, 1))
                parts.append(zj * cos + swapped * sin)
            return jnp.concatenate(parts, axis=1)

        vb = proj(5)
        if native:
            vb_ref[0, 0, :, lo:hi] = vb.T.astype(BF16)
            for head in range(N_HEADS_B):
                vbf_ref[pl.ds(lo * N_HEADS_B + head, hi - lo, stride=N_HEADS_B), :] = (
                    vb[:, head * LANES:(head + 1) * LANES])
        else:
            vb_ref[lo:hi, :] = vb.astype(BF16)
            vbf_ref[lo:hi, :] = vb
        kb = rotary(proj(4))
        kb_ref[lo:hi, :] = kb.astype(BF16)
        va = proj(2)
        ka = proj(1)
        ka_ref[lo:hi, :] = ka.astype(BF16)
        if native:
            kbf_ref[0, 0, :, lo:hi] = kb.T
            va_t = va.T
            vaf_ref[0, 0, :, lo:hi] = va_t
            va_ref[0, :, lo:hi] = va_t.astype(BF16)
            kaf_ref[0, 0, :, lo:hi] = ka.T
        else:
            kbf_ref[lo:hi, :] = kb
            vaf_ref[lo:hi, :] = va
            va_ref[lo:hi, :] = va.astype(BF16)
            kaf_ref[lo:hi, :] = ka
        qb_ref[lo:hi, :] = (rotary(proj(3)) * q_scale).astype(BF16)
        qa_ref[lo:hi, :] = (proj(0) * q_scale).astype(BF16)


def _attn_in_proj(x, g, w, cos_t, sin_t, stacked, *, seg_len, keep, layer, n_layers, native):
    m, d = x.shape
    tm = min(ROW_TILE, m)
    n_tiles = m // tm
    n_tab = cos_t.shape[0] // tm
    n_seg = m // seg_len
    row = lambda i: (i, 0)
    tab = lambda i: (i % n_tab, 0)
    blk = pl.BlockSpec((tm, WIDTH), row)
    bf_out = jax.ShapeDtypeStruct((m, WIDTH), BF16)
    if native:
        tps, kpt = seg_len // tm, keep // tm
        t_blk = (1, 1, WIDTH, tm)
        keep_spec = pl.BlockSpec(
            t_blk, lambda i: (layer, i // tps, 0, jnp.clip(i % tps - (tps - kpt), 0, kpt - 1)))
        keep_out = jax.ShapeDtypeStruct((n_layers, n_seg, WIDTH, keep), F32)
        kb_spec = pl.BlockSpec(t_blk, lambda i: (layer, i // tps, 0, i % tps))
        kb_out = jax.ShapeDtypeStruct((n_layers, n_seg, WIDTH, seg_len), F32)
        vb_spec = pl.BlockSpec((tm * N_HEADS_B, LANES), lambda i: (layer * n_tiles + i, 0))
        vb_out = jax.ShapeDtypeStruct((n_layers * m * N_HEADS_B, LANES), F32)
        va_spec = pl.BlockSpec((1, WIDTH, tm), lambda i: (i // tps, 0, i % tps))
        va_out = jax.ShapeDtypeStruct((n_seg, WIDTH, seg_len), BF16)
        vbb_spec = pl.BlockSpec((1, 1, WIDTH, tm), lambda i: (i // tps, i % tps, 0, 0))
        vbb_out = jax.ShapeDtypeStruct((n_seg, tps, WIDTH, tm), BF16)
    else:
        assert keep == seg_len
        keep_spec = kb_spec = vb_spec = pl.BlockSpec((tm, WIDTH), lambda i: (layer * n_tiles + i, 0))
        keep_out = kb_out = vb_out = jax.ShapeDtypeStruct((n_layers * m, WIDTH), F32)
        va_spec = vbb_spec = blk
        va_out = vbb_out = bf_out
    carried = [] if stacked is None else list(stacked)
    n_in = 5
    return pl.pallas_call(
        functools.partial(_attn_in_kernel, native=native),
        grid=(n_tiles,),
        in_specs=[pl.BlockSpec((tm, d), row), _resident((1, d)), _layer_of(*w),
                  pl.BlockSpec((tm, LANES), tab), pl.BlockSpec((tm, LANES), tab)]
                 + [pl.BlockSpec(memory_space=pl.ANY)] * len(carried),
        out_specs=[blk, blk, va_spec, blk, blk, vbb_spec, keep_spec, keep_spec, kb_spec, vb_spec],
        out_shape=[bf_out, bf_out, va_out, bf_out, bf_out, vbb_out, keep_out, keep_out, kb_out, vb_out],
        input_output_aliases={n_in + k: 6 + k for k in range(len(carried))},
        compiler_params=_params("arbitrary"),
        name="attn_in_proj",
    )(x, g, w[0], cos_t, sin_t, *carried)


def _rope_tables(pos):
    half = HEAD_DIM // 2
    inv = ROPE_THETA ** (-jnp.arange(half, dtype=F32) / half)
    ang = pos.astype(F32)[:, None] * inv[None, :]
    c, s = jnp.cos(ang), jnp.sin(ang)
    reps = LANES // HEAD_DIM
    return jnp.tile(c, (1, 2 * reps)), jnp.tile(jnp.concatenate([-s, s], axis=1), (1, reps))


def _band_bias_kernel(tab_ref, o_ref):
    r = pl.program_id(0)
    e_shape = (BAND_WIN + LANES, LANES)
    y = lax.broadcasted_iota(jnp.int32, e_shape, 0) - LANES
    lane = lax.broadcasted_iota(jnp.int32, e_shape, 1)
    idx = jnp.clip(BAND_PAST + lane - y, -MAX_REL, MAX_REL) + MAX_REL
    cols = []
    for hh in range(2):
        e = _table_lookup(jnp.broadcast_to(tab_ref[pl.ds(2 * r + hh, 1), :], (e_shape[0], TABLE_LANES)), idx)
        for c in range(BAND_Q // LANES):
            cols.append(e[LANES - LANES * c:LANES - LANES * c + BAND_WIN])
    bias = jnp.concatenate(cols, axis=1)

    kj = lax.broadcasted_iota(jnp.int32, bias.shape, 0)
    qi = lax.broadcasted_iota(jnp.int32, bias.shape, 1) % BAND_Q
    first_key = (qi // CHUNK) * CHUNK
    in_band = jnp.logical_and(kj >= first_key, kj < first_key + BAND_KEYS)
    o_ref[0] = jnp.where(in_band, bias * LOG2E, NEG)


def _band_bias(table):
    n = table.shape[0] // 2
    return pl.pallas_call(
        _band_bias_kernel,
        grid=(n,),
        in_specs=[_resident(table.shape)],
        out_specs=pl.BlockSpec((1, BAND_WIN, 2 * BAND_Q), lambda r: (r, 0, 0)),
        out_shape=jax.ShapeDtypeStruct((n, BAND_WIN, 2 * BAND_Q), F32),
        compiler_params=_params("arbitrary"),
        name="band_bias",
    )(table)


def _sample_bias_kernel(tab_ref, o_ref, *, t):
    rows = tab_ref[0]
    blocks = []
    for c in range(BIAS_LANES // LANES):
        shape = (rows.shape[0], LANES)
        frame = lax.broadcasted_iota(jnp.int32, shape, 0) % t
        kj = lax.broadcasted_iota(jnp.int32, shape, 1) + c * LANES
        idx = jnp.clip(BAND_PAST + frame - kj, -MAX_REL, MAX_REL) + MAX_REL
        blocks.append(_table_lookup(rows, idx))
    o_ref[0] = jnp.concatenate(blocks, axis=1) * LOG2E


def _sample_bias(table, t):
    n = table.shape[0] // N_HEADS_A
    per_row = jnp.repeat(table.reshape(n, N_HEADS_A, TABLE_LANES), t, axis=1)
    return pl.pallas_call(
        functools.partial(_sample_bias_kernel, t=t),
        grid=(n,),
        in_specs=[pl.BlockSpec((1, N_HEADS_A * t, TABLE_LANES), lambda r: (r, 0, 0))],
        out_specs=pl.BlockSpec((1, N_HEADS_A * t, BIAS_LANES), lambda r: (r, 0, 0)),
        out_shape=jax.ShapeDtypeStruct((n, N_HEADS_A * t, BIAS_LANES), F32),
        compiler_params=_params("arbitrary"),
        name="sample_bias",
    )(per_row)


def _band_kernel(q_ref, kp_ref, kc_ref, vp_ref, vc_ref, bias_ref, o_ref, *bufs):
    qblk = pl.program_id(2)
    n_sub = q_ref.shape[1] // BAND_Q

    def window(prev_ref, cur_ref, sub, first, axis):
        lo, hi = sub * BAND_Q, (sub + 1) * BAND_Q
        take = lambda ref, a, b: ref[0, a:b] if axis == 0 else ref[0, :, a:b]
        parts = []
        if lo < BAND_PAST and not first:
            parts.append(take(prev_ref, lo, BAND_PAST))
        parts.append(take(cur_ref, max(lo - BAND_PAST, 0), hi))
        return parts[0] if len(parts) == 1 else jnp.concatenate(parts, axis=axis)

    def scores(sub, first):
        keys = window(kp_ref, kc_ref, sub, first, 0)
        nk = keys.shape[0]
        q = q_ref[0, sub * BAND_Q:(sub + 1) * BAND_Q, :]
        s = lax.dot_general(keys, _split_maps(q), _NT, preferred_element_type=F32)
        s = s + bias_ref[0, BAND_WIN - nk:, :]
        bufs[2 * sub][:nk] = s
        bufs[2 * sub + 1][...] = jnp.max(s, axis=0, keepdims=True)
        return nk

    def finish(nk, sub, first):
        p = jnp.exp2(bufs[2 * sub][:nk] - bufs[2 * sub + 1][...]).astype(BF16)
        o_t = _values_and_sum(window(vp_ref, vc_ref, sub, first, 1), p)
        o_t = o_t[:LANES] / o_t[LANES:LANES + 1]
        o = jnp.concatenate([o_t[:HEAD_DIM, :BAND_Q], o_t[HEAD_DIM:, BAND_Q:]], axis=0).T
        o_ref[0, sub * BAND_Q:(sub + 1) * BAND_Q, :] = o.astype(BF16)

    def run(first):
        sizes = [scores(sub, first) for sub in range(n_sub)]
        for sub in range(n_sub):
            finish(sizes[sub], sub, first)

    pl.when(qblk == 0)(functools.partial(run, True))
    pl.when(qblk > 0)(functools.partial(run, False))


def _band_attn(q, k, v_t, bias, layer):
    b, s, _ = q.shape
    bq = 2 * BAND_PAST if s % (2 * BAND_PAST) == 0 else BAND_PAST
    n_pairs = N_HEADS_A // 2
    prev_of = lambda qi: jnp.maximum(qi * (bq // BAND_PAST) - 1, 0)
    cur = lambda hp, bi, qi: (bi, qi, hp)
    prev = lambda hp, bi, qi: (bi, prev_of(qi), hp)
    cur_t = lambda hp, bi, qi: (bi, hp, qi)
    prev_t = lambda hp, bi, qi: (bi, hp, prev_of(qi))
    blk = (1, bq, LANES)
    return pl.pallas_call(
        _band_kernel,
        grid=(n_pairs, b, s // bq),
        in_specs=[pl.BlockSpec(blk, cur), pl.BlockSpec((1, BAND_PAST, LANES), prev), pl.BlockSpec(blk, cur),
                  pl.BlockSpec((1, LANES, BAND_PAST), prev_t), pl.BlockSpec((1, LANES, bq), cur_t),
                  pl.BlockSpec((1, BAND_WIN, 2 * BAND_Q), lambda hp, bi, qi: (layer * n_pairs + hp, 0, 0))],
        out_specs=pl.BlockSpec(blk, cur),
        out_shape=jax.ShapeDtypeStruct((b, s, WIDTH), BF16),
        scratch_shapes=[pltpu.VMEM((BAND_WIN, 2 * BAND_Q), F32), pltpu.VMEM((1, 2 * BAND_Q), F32)]
                       * (bq // BAND_Q),
        compiler_params=_params("arbitrary", "arbitrary", "arbitrary"),
        name="band_attn",
    )(q, k, k, v_t, v_t, bias)


def _lambda(lam_ref, lam_init):
    lp = lam_ref[...]
    a1 = jnp.sum(lp[0:1] * lp[1:2], axis=-1, keepdims=True)
    a2 = jnp.sum(lp[2:3] * lp[3:4], axis=-1, keepdims=True)
    return jnp.exp(a1) - jnp.exp(a2) + lam_init


def _diff_kernel(lam_ref, g_ref, q_ref, qn_ref, k_ref, v_ref, o_ref, m_ref, acc_ref, *bufs, lam_init):
    qi = pl.program_id(2)
    q = q_ref[0]
    bq = q.shape[0]
    bk = bq // 2
    q2 = _split_maps(q)
    queries = (q2[:bq], q2[bq:])

    m_ref[...] = jnp.full(m_ref.shape, NEG, F32)
    acc_ref[...] = jnp.zeros(acc_ref.shape, F32)

    def block(ref, kj):
        return ref[0, pl.ds(pl.multiple_of(kj * bk, bk), bk), :]

    def scores(kj, stream, buf, diagonal=None, queries=queries):
        s_ref, mx_ref = bufs[4 * stream + 2 * buf], bufs[4 * stream + 2 * buf + 1]
        qs = queries[stream][bk:] if diagonal == 1 else queries[stream]
        s = lax.dot_general(block(k_ref, kj), qs, _NT, preferred_element_type=F32)
        if diagonal is not None:
            k_chunk = lax.broadcasted_iota(jnp.int32, s.shape, 0) // CHUNK
            q_chunk = lax.broadcasted_iota(jnp.int32, s.shape, 1) // CHUNK
            s = jnp.where(k_chunk <= q_chunk, s, NEG)
        n = s.shape[1]
        s_ref[:, :n] = s
        mx_ref[:, :n] = jnp.max(s, axis=0, keepdims=True)

    def absorb(kj, stream, buf, second_half=False):
        s_ref, mx_ref = bufs[4 * stream + 2 * buf], bufs[4 * stream + 2 * buf + 1]
        n = bk if second_half else bq
        cols = slice((stream + 1) * bq - n, (stream + 1) * bq)
        m_old = m_ref[:, cols]
        m_new = jnp.maximum(m_old, mx_ref[:, :n])
        alpha = jnp.exp2(m_old - m_new)
        p = jnp.exp2(s_ref[:, :n] - m_new).astype(BF16)
        acc_ref[:, cols] = alpha * acc_ref[:, cols] + _values_and_sum(v_ref[0, kj], p)
        m_ref[:, cols] = m_new

    def stage(kj, buf, look=True, look_diagonal=None, second_half=False):
        if look:
            scores(kj + 1, 0, 1 - buf, look_diagonal)
        absorb(kj, 1, buf, second_half)
        if look:
            scores(kj + 1, 1, 1 - buf, look_diagonal)
        absorb(kj, 0, buf, second_half)

    @pl.when(qi == 0)
    def _():
        scores(0, 0, 0, diagonal=0)
        scores(0, 1, 0, diagonal=0)
        stage(0, 0, look_diagonal=1)
        stage(1, 1, look=False, second_half=True)


    def two_blocks(pair, carry):
        stage(2 * pair, 0)
        stage(2 * pair + 1, 1)
        return carry

    lax.fori_loop(0, jnp.maximum(qi - 1, 0), two_blocks, 0)

    @pl.when(qi > 0)
    def _():
        last = 2 * qi
        stage(last - 2, 0)
        stage(last - 1, 1, look_diagonal=0)
        stage(last, 0, look_diagonal=1)
        stage(last + 1, 1, look=False, second_half=True)

    def finalize():
        o_t = acc_ref[:LANES] / acc_ref[LANES:LANES + 1]
        lam = _lambda(lam_ref, lam_init)
        d_t = o_t[:, :bq] - lam * o_t[:, bq:]
        inv = lax.rsqrt(jnp.mean(d_t * d_t, axis=0, keepdims=True) + EPS)
        o_ref[0] = ((d_t * inv).T * (g_ref[...] * (1.0 - lam_init))).astype(BF16)

    @pl.when(qi + 1 < pl.num_programs(2))
    def _():
        nxt = _split_maps(qn_ref[0])
        scores(0, 0, 0, queries=(nxt[:bq], nxt[bq:]))
        scores(0, 1, 0, queries=(nxt[:bq], nxt[bq:]))
        finalize()

    pl.when(qi + 1 == pl.num_programs(2))(finalize)


def _diff_attn(q, k, v_t, lam_p, g, lam_init):
    b, s, _ = q.shape
    bq = 2 * DIFF_KEYS
    assert s % bq == 0 and v_t.shape[3] == DIFF_KEYS
    seq = pl.BlockSpec((1, s, LANES), lambda bi, h, qi: (bi, 0, h))
    seq_t = pl.BlockSpec((1, s // DIFF_KEYS, LANES, DIFF_KEYS), lambda bi, h, qi: (bi, 0, h, 0))
    blk = pl.BlockSpec((1, bq, LANES), lambda bi, h, qi: (bi, qi, h))
    return pl.pallas_call(
        functools.partial(_diff_kernel, lam_init=lam_init),
        grid=(b, N_HEADS_B, s // bq),
        in_specs=[_resident(lam_p.shape), _resident(g.shape), blk,
                  pl.BlockSpec((1, bq, LANES), lambda bi, h, qi: (bi, jnp.minimum(qi + 1, s // bq - 1), h)),
                  seq, seq_t],
        out_specs=blk,
        out_shape=jax.ShapeDtypeStruct((b, s, WIDTH), BF16),
        scratch_shapes=[pltpu.VMEM((1, 2 * bq), F32), pltpu.VMEM((LANES + SUM_ROWS, 2 * bq), F32),
                        ] + [pltpu.VMEM((DIFF_KEYS, bq), F32), pltpu.VMEM((1, bq), F32)] * 4,
        compiler_params=_params("arbitrary", "arbitrary", "arbitrary"),
        name="diff_attn",
    )(lam_p, g, q, q, k, v_t)


def _sample_kernel(lam_ref, g_ref, qa_ref, kan_ref, van_ref, qb_ref, kbn_ref, vbn_ref,
                   cak_ref, cav_ref, cbk_ref, cbv_ref, bias_c_ref, bias_n_ref,
                   oa_ref, ob_ref, *, lam_init):
    t = qa_ref.shape[0]
    n_grp = WIDTH // HEAD_DIM
    rows = n_grp * t
    grp_of_row = lax.broadcasted_iota(jnp.int32, (rows, WIDTH), 0) // t
    grp_of_lane = lax.broadcasted_iota(jnp.int32, (rows, WIDTH), 1) // HEAD_DIM
    own = grp_of_row == grp_of_lane
    lane_t = lax.broadcasted_iota(jnp.int32, (t, WIDTH), 1)

    def expand(q):
        qe = jnp.concatenate([q] * n_grp, axis=0)
        return jnp.where(own, qe, jnp.zeros_like(qe))

    def attend(qe, kt_cache, k_new, v_cache, v_new, bias_c, bias_n, v_dims):
        s_c = jnp.dot(qe, kt_cache, preferred_element_type=F32)
        s_n = lax.dot_general(qe, k_new, _NT, preferred_element_type=F32)
        if bias_c is not None:
            s_c = s_c + bias_c
            s_n = s_n + bias_n
        m = jnp.maximum(jnp.max(s_c, axis=-1, keepdims=True), jnp.max(s_n, axis=-1, keepdims=True))
        p_c = jnp.exp2(s_c - m)
        p_n = jnp.exp2(s_n - m)
        l = jnp.sum(p_c, axis=-1, keepdims=True) + jnp.sum(p_n, axis=-1, keepdims=True)
        o = (lax.dot_general(p_c.astype(BF16), v_cache, v_dims, preferred_element_type=F32)
             + jnp.dot(p_n.astype(BF16), v_new, preferred_element_type=F32))
        return o / l

    oa_all = attend(expand(qa_ref[...]), cak_ref[0, 0].astype(BF16), kan_ref[...],
                    cav_ref[0, 0].astype(BF16), van_ref[...], bias_c_ref[...], bias_n_ref[...], _NT)
    oa = jnp.zeros((t, WIDTH), F32)
    for h in range(N_HEADS_A):
        oa = jnp.where(lane_t // HEAD_DIM == h, oa_all[h * t:(h + 1) * t], oa)
    oa_ref[...] = oa.astype(BF16)

    past = cbk_ref.shape[3]
    v_cache = jnp.concatenate([cbv_ref[pl.ds(h, past, stride=N_HEADS_B), :] for h in range(N_HEADS_B)],
                              axis=1).astype(BF16)
    ob_all = attend(expand(qb_ref[...]), cbk_ref[0, 0].astype(BF16), kbn_ref[...],
                    v_cache, vbn_ref[...], None, None, _NN)
    lam = _lambda(lam_ref, lam_init)
    g = g_ref[...]
    outs = []
    for h in range(N_HEADS_B):
        d = ob_all[2 * h * t:(2 * h + 1) * t] - lam * ob_all[(2 * h + 1) * t:(2 * h + 2) * t]
        outs.append(_rms(d[:, h * LANES:(h + 1) * LANES], g) * (1.0 - lam_init))
    ob_ref[...] = jnp.concatenate(outs, axis=1).astype(BF16)


def _sample_attn(qa, ka, va, qb, kb, vb, cak, cav, cbk, cbv, bias_c, bias_n, lam_p, g, lam_init, t, layer):
    m = qa.shape[0]
    bs = m // t
    past = cbk.shape[3]
    new = pl.BlockSpec((t, WIDTH), lambda i: (i, 0))
    cache = lambda c: pl.BlockSpec((1, 1) + c.shape[2:], lambda i: (layer, i, 0, 0))
    cbv_spec = pl.BlockSpec((past * N_HEADS_B, LANES), lambda i: (layer * bs + i, 0))
    out = jax.ShapeDtypeStruct((m, WIDTH), BF16)
    return pl.pallas_call(
        functools.partial(_sample_kernel, lam_init=lam_init),
        grid=(bs,),
        in_specs=[_resident(lam_p.shape), _resident(g.shape)] + [new] * 6
                 + [cache(cak), cache(cav), cache(cbk), cbv_spec,
                    _resident(bias_c.shape), _resident(bias_n.shape)],
        out_specs=[new, new],
        out_shape=[out, out],
        compiler_params=_params("arbitrary"),
        name="sample_attn",
    )(lam_p, g, qa, ka, va, qb, kb, vb, cak, cav, cbk, cbv, bias_c, bias_n)


def _conv_kernel(x_ref, g_ref, w_ref, cw_ref, st_ref, y_ref, tail_ref, carry_ref, *, tps, seg_len):
    tm, d = x_ref.shape
    h = _rms(x_ref[...], g_ref[...]).astype(BF16)
    carried = seg_len >= tm
    if carried:
        @pl.when(pl.program_id(0) % tps == 0)
        def _():
            carry_ref[...] = st_ref[0]

    cw = cw_ref[...]
    cc = 2 * FF_CHUNK
    for c in range(d // cc):
        cols = slice(c * cc, (c + 1) * cc)
        proj = lambda part: jnp.dot(h, w_ref[:, part * d + c * cc:part * d + (c + 1) * cc],
                                    preferred_element_type=F32)
        xin = proj(1) * proj(2)
        r = lax.broadcasted_iota(jnp.int32, xin.shape, 0)
        x1 = pltpu.roll(xin, 1, 0)
        x2 = pltpu.roll(xin, 2, 0)
        if carried:
            prev = carry_ref[:, cols]
            x1 = jnp.where(r == 0, prev[7:8], x1)
            x2 = jnp.where(r == 0, prev[6:7], jnp.where(r == 1, prev[7:8], x2))
            carry_ref[:, cols] = xin[tm - 8:]
            tail_ref[0, :, cols] = xin[tm - 8:]
        else:
            t = r % seg_len
            x1 = jnp.where(t >= 1, x1, st_ref[0, :, cols])
            x2 = jnp.where(t >= 2, x2, st_ref[1, :, cols])
            tail_ref[:, cols] = xin
        conv = cw[0:1, cols] * x2 + cw[1:2, cols] * x1 + cw[2:3, cols] * xin
        y_ref[:, cols] = (proj(0) * conv).astype(BF16)


def _conv_in(x, g, w, cw, state, *, seg_len):
    m, d = x.shape
    tm = min(ROW_TILE, m)
    row = lambda i: (i, 0)
    if seg_len >= tm:
        tps = seg_len // tm
        st_spec = pl.BlockSpec((1, 8, d), lambda i: (i // tps, 0, 0))
        tail_spec = pl.BlockSpec((1, 8, d), lambda i: (i // tps, 0, 0))
        tail_shape = jax.ShapeDtypeStruct((m // seg_len, 8, d), F32)
    else:
        tps = 1
        st_spec = pl.BlockSpec((2, tm, d), lambda i: (0, i, 0))
        tail_spec = pl.BlockSpec((tm, d), row)
        tail_shape = jax.ShapeDtypeStruct((m, d), F32)
    return pl.pallas_call(
        functools.partial(_conv_kernel, tps=tps, seg_len=seg_len),
        grid=(m // tm,),
        in_specs=[pl.BlockSpec((tm, d), row), _resident((1, d)), _layer_of(*w),
                  _resident(cw.shape), st_spec],
        out_specs=[pl.BlockSpec((tm, d), row), tail_spec],
        out_shape=[jax.ShapeDtypeStruct((m, d), BF16), tail_shape],
        scratch_shapes=[pltpu.VMEM((8, d), F32)],
        compiler_params=_params("arbitrary"),
        name="conv_in",
    )(x, g, w[0], cw, state)


def _ffn_kernel(*refs, n_parts, final):
    x_ref = refs[0]
    a_refs = refs[1:1 + n_parts]
    wo_ref, g_ref, wg_ref, wu_ref, wd_ref = refs[1 + n_parts:6 + n_parts]
    gf_ref = refs[6 + n_parts] if final else None
    o_ref, act_ref = refs[-2], refs[-1]

    tm = x_ref.shape[0]
    halves = [slice(0, tm // 2), slice(tm // 2, tm)] if tm % 32 == 0 else [slice(0, tm)]
    x1s = []
    for rows in halves:
        a = jnp.concatenate([r[rows, :] for r in a_refs], axis=1) if n_parts > 1 else a_refs[0][rows, :]
        x1s.append(x_ref[rows, :] + jnp.dot(a, wo_ref[...], preferred_element_type=F32))
    d_ff = wg_ref.shape[1]
    for rows, x1 in zip(halves, x1s):
        h = _rms(x1, g_ref[...]).astype(BF16)
        for c in range(d_ff // FF_CHUNK):
            cols = slice(c * FF_CHUNK, (c + 1) * FF_CHUNK)
            gate = jnp.dot(h, wg_ref[:, cols], preferred_element_type=F32)
            up = jnp.dot(h, wu_ref[:, cols], preferred_element_type=F32)
            act_ref[rows, cols] = (gate * (1.0 / (1.0 + jnp.exp(-gate))) * up).astype(BF16)
        y = x1 + jnp.dot(act_ref[rows, :], wd_ref[...], preferred_element_type=F32)
        o_ref[rows, :] = _rms(y, gf_ref[...]) if final else y


def _mix_out_ffn(x, parts, wo, g, wg, wu, wd, g_final=None):
    m, d = x.shape
    tm = min(ROW_TILE, m)
    row = lambda i: (i, 0)
    final = g_final is not None
    extra = [g_final] if final else []
    return pl.pallas_call(
        functools.partial(_ffn_kernel, n_parts=len(parts), final=final),
        grid=(m // tm,),
        in_specs=[pl.BlockSpec((tm, d), row)] + [pl.BlockSpec((tm, p.shape[1]), row) for p in parts]
                 + [_layer_of(*wo), _resident((1, d)), _layer_of(*wg), _layer_of(*wu), _layer_of(*wd)]
                 + [_resident((1, d))] * len(extra),
        out_specs=pl.BlockSpec((tm, d), row),
        out_shape=jax.ShapeDtypeStruct((m, d), F32),
        scratch_shapes=[pltpu.VMEM((tm, wg[0].shape[2]), BF16)],
        compiler_params=_params("arbitrary"),
        name="mix_out_ffn",
    )(x, *parts, wo[0], g, wg[0], wu[0], wd[0], *extra)


def kernel(x_prompt, x_sample, cache_a_k, cache_a_v, cache_b_k, cache_b_v, state_conv, norm_mix, norm_ffn, norm_final, w_attn_in, w_attn_out, rel_bias, lambda_q1, lambda_k1, lambda_q2, lambda_k2, subln_g, w_conv_in, conv_w, w_conv_out, w_ffn_gate, w_ffn_up, w_ffn_down):
    b, s, d = x_prompt.shape
    bs, t, _ = x_sample.shape
    depth = norm_mix.shape[0]
    n_attn = w_attn_in.shape[0]
    n_conv = w_conv_in.shape[0]
    past = cache_b_k.shape[2]
    a_past = cache_a_k.shape[2]
    keep = min(BAND_PAST, s)
    assert s % BAND_PAST == 0 and a_past == BAND_PAST and d % (2 * FF_CHUNK) == 0
    assert w_ffn_gate.shape[2] % FF_CHUNK == 0 and 2 <= t and a_past + t <= BIAS_LANES

    xp = x_prompt.reshape(b * s, d)
    xs = x_sample.reshape(bs * t, d)
    row = lambda v: v.reshape(1, -1)

    cos_p, sin_p = _rope_tables(jnp.arange(s))
    cos_s, sin_s = (jnp.tile(tab, (bs, 1)) for tab in _rope_tables(past + jnp.arange(t)))

    table = jnp.pad(rel_bias.reshape(n_attn * N_HEADS_A, 2 * MAX_REL + 1),
                    ((0, 0), (0, TABLE_LANES - 2 * MAX_REL - 1)))
    band_bias = _band_bias(table)
    sample_bias = _sample_bias(table, t)

    cak_t = cache_a_k.transpose(0, 1, 3, 4, 2).reshape(n_attn, bs, WIDTH, a_past)
    cav_t = cache_a_v.transpose(0, 1, 3, 4, 2).reshape(n_attn, bs, WIDTH, a_past)
    cbk_t = cache_b_k.transpose(0, 1, 3, 4, 5, 2).reshape(n_attn, bs, WIDTH, past)
    cbv_rows = cache_b_v.reshape(n_attn * bs * past * N_HEADS_B, LANES)

    attn_in_w, attn_out_w = w_attn_in.astype(BF16), w_attn_out.astype(BF16)
    conv_in_w, conv_out_w = w_conv_in.astype(BF16), w_conv_out.astype(BF16)
    gate_w, up_w, down_w = w_ffn_gate.astype(BF16), w_ffn_up.astype(BF16), w_ffn_down.astype(BF16)

    stacked_p = stacked_s = None
    pcs, scs = [], []
    for i in range(depth):
        j = i // 2
        g_mix = row(norm_mix[i])
        if i % 2 == 0:
            lam_init = 0.8 - 0.6 * math.exp(-0.3 * i)
            w_in = (attn_in_w, j)
            wo = (attn_out_w, j)
            lam_p = jnp.stack([lambda_q1[j], lambda_k1[j], lambda_q2[j], lambda_k2[j]])
            g_sub = row(subln_g[j])

            outs = _attn_in_proj(xp, g_mix, w_in, cos_p, sin_p, stacked_p,
                                 seg_len=s, keep=keep, layer=j, n_layers=n_attn, native=True)
            qa, ka, va_t, qb, kb, vb_t = outs[:6]
            qa, ka, qb, kb = (o.reshape(b, s, WIDTH) for o in (qa, ka, qb, kb))
            stacked_p = outs[6:]
            oa = _band_attn(qa, ka, va_t, band_bias, j)
            ob = _diff_attn(qb, kb, vb_t, lam_p, g_sub, lam_init)
            parts_p = [oa.reshape(b * s, WIDTH), ob.reshape(b * s, WIDTH)]

            outs = _attn_in_proj(xs, g_mix, w_in, cos_s, sin_s, stacked_s,
                                 seg_len=t, keep=t, layer=j, n_layers=n_attn, native=False)
            stacked_s = outs[6:]
            oa, ob = _sample_attn(
                *outs[:6], cak_t, cav_t, cbk_t, cbv_rows,
                sample_bias[j, :, :a_past], sample_bias[j, :, a_past:a_past + t],
                lam_p, g_sub, lam_init, t, j)
            parts_s = [oa, ob]
        else:
            w_in = (conv_in_w, j)
            wo = (conv_out_w, j)
            yp, tail = _conv_in(xp, g_mix, w_in, conv_w[j], jnp.zeros((b, 8, d), F32), seg_len=s)
            pcs.append(tail[:, 6:])
            st = state_conv[j]
            stand_in = jnp.stack([jnp.pad(st[:, 1:2], ((0, 0), (0, t - 1), (0, 0))),
                                  jnp.pad(st, ((0, 0), (0, t - 2), (0, 0)))]).reshape(2, bs * t, d)
            ys, xin = _conv_in(xs, g_mix, w_in, conv_w[j], stand_in, seg_len=t)
            scs.append(xin.reshape(bs, t, d)[:, t - 2:])
            parts_p, parts_s = [yp], [ys]

        ffn = (wo, row(norm_ffn[i]), (gate_w, i), (up_w, i), (down_w, i))
        g_final = row(norm_final) if i == depth - 1 else None
        xp = _mix_out_ffn(xp, parts_p, *ffn, g_final=g_final)
        xs = _mix_out_ffn(xs, parts_s, *ffn, g_final=g_final)

    pak, pav, pbk, pbv = stacked_p
    sak, sav, sbk, sbv = stacked_s
    heads_a = lambda z: z.reshape(n_attn, b, N_HEADS_A, HEAD_DIM, keep).transpose(0, 1, 4, 2, 3)
    return (xp.reshape(b, s, d), xs.reshape(bs, t, d),
            heads_a(pak), heads_a(pav),
            pbk.reshape(n_attn, b, N_HEADS_B, 2, HEAD_DIM, s).transpose(0, 1, 5, 2, 3, 4),
            pbv.reshape(n_attn, b, s, N_HEADS_B, 2 * HEAD_DIM),
            jnp.stack(pcs),
            sak.reshape(n_attn, bs, t, N_HEADS_A, HEAD_DIM), sav.reshape(n_attn, bs, t, N_HEADS_A, HEAD_DIM),
            sbk.reshape(n_attn, bs, t, N_HEADS_B, 2, HEAD_DIM), sbv.reshape(n_attn, bs, t, N_HEADS_B, 2 * HEAD_DIM),
            jnp.stack(scs))
```

```python
import functools
import math

import jax
import jax.numpy as jnp
from jax import lax
from jax.experimental import pallas as pl
from jax.experimental.pallas import tpu as pltpu

F32 = jnp.float32
BF16 = jnp.bfloat16

EPS = 1e-6
NEG = -1e30
LOG2E = 1.4426950408889634
ROPE_THETA = 10000.0
CHUNK = 64
HEAD_DIM = 64
N_HEADS_A = 8
N_HEADS_B = 4
BAND_PAST = 512
MAX_REL = 128
BAND_KEYS = BAND_PAST + CHUNK
BAND_Q = 256
BAND_WIN = BAND_Q + BAND_PAST
DIFF_KEYS = 512
BIAS_LANES = 640
WIDTH = N_HEADS_A * HEAD_DIM
LANES = 128
ROW_TILE = 512
FF_CHUNK = 256
V7X_VMEM_BYTES = 64 * 1024 * 1024
VMEM_LIMIT = V7X_VMEM_BYTES - 8 * 1024 * 1024

_NT = (((1,), (1,)), ((), ()))
_TN = (((0,), (0,)), ((), ()))
_NN = (((1,), (0,)), ((), ()))


def _resident(shape):
    return pl.BlockSpec(shape, lambda *_: (0,) * len(shape), pipeline_mode=pl.Buffered(1))


def _layer_of(stack, layer):
    return pl.BlockSpec((None,) + stack.shape[1:], lambda *_: (layer, 0, 0), pipeline_mode=pl.Buffered(1))


def _rms(x, g):
    return x * lax.rsqrt(jnp.mean(x * x, axis=-1, keepdims=True) + EPS) * g


def _params(*sem):
    return pltpu.CompilerParams(dimension_semantics=sem, vmem_limit_bytes=VMEM_LIMIT)


SUM_ROWS = 16


def _values_and_sum(v_t, p):
    ones = jnp.ones((SUM_ROWS, v_t.shape[1]), v_t.dtype)
    return jnp.dot(jnp.concatenate([v_t, ones], axis=0), p, preferred_element_type=F32)


TABLE_LANES = 384


def _table_lookup(rows, idx):
    lane = idx & (LANES - 1)
    group = idx >> (LANES.bit_length() - 1)
    out = None
    for g in range(TABLE_LANES // LANES):
        picked = jnp.take_along_axis(rows[:, g * LANES:(g + 1) * LANES], lane, axis=1)
        out = picked if out is None else jnp.where(group == g, picked, out)
    return out


def _split_maps(q):
    low = lax.broadcasted_iota(jnp.int32, q.shape, 1) < HEAD_DIM
    zero = jnp.zeros_like(q)
    return jnp.concatenate([jnp.where(low, q, zero), jnp.where(low, zero, q)], axis=0)


def _attn_in_kernel(*refs, native):
    x_ref, g_ref, w_ref, cos_ref, sin_ref = refs[:5]
    (qa_ref, ka_ref, va_ref, qb_ref, kb_ref, vb_ref,
     kaf_ref, vaf_ref, kbf_ref, vbf_ref) = refs[-10:]
    q_scale = LOG2E / math.sqrt(HEAD_DIM)
    tm = x_ref.shape[0]
    halves = [(0, tm // 2), (tm // 2, tm)] if tm % 32 == 0 else [(0, tm)]
    normed = [_rms(x_ref[lo:hi, :], g_ref[...]).astype(BF16) for lo, hi in halves]

    for (lo, hi), h in zip(halves, normed):
        def proj(c):
            return jnp.dot(h, w_ref[:, c * WIDTH:(c + 1) * WIDTH], preferred_element_type=F32)

        cos = cos_ref[lo:hi, :]
        sin = sin_ref[lo:hi, :]
        low_half = (lax.broadcasted_iota(jnp.int32, cos.shape, 1) % HEAD_DIM) < (HEAD_DIM // 2)

        def rotary(z):
            parts = []
            for j in range(WIDTH // LANES):
                zj = z[:, j * LANES:(j + 1) * LANES]
                swapped = jnp.where(low_half, pltpu.roll(zj, LANES - HEAD_DIM // 2, 1),
                                    pltpu.roll(zj, HEAD_DIM // 2, 1))
                parts.append(zj * cos + swapped * sin)
            return jnp.concatenate(parts, axis=1)

        vb = proj(5)
        if native:
            vb_ref[0, 0, :, lo:hi] = vb.T.astype(BF16)
            for head in range(N_HEADS_B):
                vbf_ref[pl.ds(lo * N_HEADS_B + head, hi - lo, stride=N_HEADS_B), :] = (
                    vb[:, head * LANES:(head + 1) * LANES])
        else:
            vb_ref[lo:hi, :] = vb.astype(BF16)
            vbf_ref[lo:hi, :] = vb
        kb = rotary(proj(4))
        kb_ref[lo:hi, :] = kb.astype(BF16)
        va = proj(2)
        ka = proj(1)
        ka_ref[lo:hi, :] = ka.astype(BF16)
        if native:
            kbf_ref[0, 0, :, lo:hi] = kb.T
            va_t = va.T
            vaf_ref[0, 0, :, lo:hi] = va_t
            va_ref[0, :, lo:hi] = va_t.astype(BF16)
            kaf_ref[0, 0, :, lo:hi] = ka.T
        else:
            kbf_ref[lo:hi, :] = kb
            vaf_ref[lo:hi, :] = va
            va_ref[lo:hi, :] = va.astype(BF16)
            kaf_ref[lo:hi, :] = ka
        qb_ref[lo:hi, :] = (rotary(proj(3)) * q_scale).astype(BF16)
        qa_ref[lo:hi, :] = (proj(0) * q_scale).astype(BF16)


def _attn_in_proj(x, g, w, cos_t, sin_t, stacked, *, seg_len, keep, layer, n_layers, native):
    m, d = x.shape
    tm = min(ROW_TILE, m)
    n_tiles = m // tm
    n_tab = cos_t.shape[0] // tm
    n_seg = m // seg_len
    row = lambda i: (i, 0)
    tab = lambda i: (i % n_tab, 0)
    blk = pl.BlockSpec((tm, WIDTH), row)
    bf_out = jax.ShapeDtypeStruct((m, WIDTH), BF16)
    if native:
        tps, kpt = seg_len // tm, keep // tm
        t_blk = (1, 1, WIDTH, tm)
        keep_spec = pl.BlockSpec(
            t_blk, lambda i: (layer, i // tps, 0, jnp.clip(i % tps - (tps - kpt), 0, kpt - 1)))
        keep_out = jax.ShapeDtypeStruct((n_layers, n_seg, WIDTH, keep), F32)
        kb_spec = pl.BlockSpec(t_blk, lambda i: (layer, i // tps, 0, i % tps))
        kb_out = jax.ShapeDtypeStruct((n_layers, n_seg, WIDTH, seg_len), F32)
        vb_spec = pl.BlockSpec((tm * N_HEADS_B, LANES), lambda i: (layer * n_tiles + i, 0))
        vb_out = jax.ShapeDtypeStruct((n_layers * m * N_HEADS_B, LANES), F32)
        va_spec = pl.BlockSpec((1, WIDTH, tm), lambda i: (i // tps, 0, i % tps))
        va_out = jax.ShapeDtypeStruct((n_seg, WIDTH, seg_len), BF16)
        vbb_spec = pl.BlockSpec((1, 1, WIDTH, tm), lambda i: (i // tps, i % tps, 0, 0))
        vbb_out = jax.ShapeDtypeStruct((n_seg, tps, WIDTH, tm), BF16)
    else:
        assert keep == seg_len
        keep_spec = kb_spec = vb_spec = pl.BlockSpec((tm, WIDTH), lambda i: (layer * n_tiles + i, 0))
        keep_out = kb_out = vb_out = jax.ShapeDtypeStruct((n_layers * m, WIDTH), F32)
        va_spec = vbb_spec = blk
        va_out = vbb_out = bf_out
    carried = [] if stacked is None else list(stacked)
    n_in = 5
    return pl.pallas_call(
        functools.partial(_attn_in_kernel, native=native),
        grid=(n_tiles,),
        in_specs=[pl.BlockSpec((tm, d), row), _resident((1, d)), _layer_of(*w),
                  pl.BlockSpec((tm, LANES), tab), pl.BlockSpec((tm, LANES), tab)]
                 + [pl.BlockSpec(memory_space=pl.ANY)] * len(carried),
        out_specs=[blk, blk, va_spec, blk, blk, vbb_spec, keep_spec, keep_spec, kb_spec, vb_spec],
        out_shape=[bf_out, bf_out, va_out, bf_out, bf_out, vbb_out, keep_out, keep_out, kb_out, vb_out],
        input_output_aliases={n_in + k: 6 + k for k in range(len(carried))},
        compiler_params=_params("arbitrary"),
        name="attn_in_proj",
    )(x, g, w[0], cos_t, sin_t, *carried)


def _rope_tables(pos):
    half = HEAD_DIM // 2
    inv = ROPE_THETA ** (-jnp.arange(half, dtype=F32) / half)
    ang = pos.astype(F32)[:, None] * inv[None, :]
    c, s = jnp.cos(ang), jnp.sin(ang)
    reps = LANES // HEAD_DIM
    return jnp.tile(c, (1, 2 * reps)), jnp.tile(jnp.concatenate([-s, s], axis=1), (1, reps))


def _band_bias_kernel(tab_ref, o_ref):
    r = pl.program_id(0)
    e_shape = (BAND_WIN + LANES, LANES)
    y = lax.broadcasted_iota(jnp.int32, e_shape, 0) - LANES
    lane = lax.broadcasted_iota(jnp.int32, e_shape, 1)
    idx = jnp.clip(BAND_PAST + lane - y, -MAX_REL, MAX_REL) + MAX_REL
    cols = []
    for hh in range(2):
        e = _table_lookup(jnp.broadcast_to(tab_ref[pl.ds(2 * r + hh, 1), :], (e_shape[0], TABLE_LANES)), idx)
        for c in range(BAND_Q // LANES):
            cols.append(e[LANES - LANES * c:LANES - LANES * c + BAND_WIN])
    bias = jnp.concatenate(cols, axis=1)

    kj = lax.broadcasted_iota(jnp.int32, bias.shape, 0)
    qi = lax.broadcasted_iota(jnp.int32, bias.shape, 1) % BAND_Q
    first_key = (qi // CHUNK) * CHUNK
    in_band = jnp.logical_and(kj >= first_key, kj < first_key + BAND_KEYS)
    o_ref[0] = jnp.where(in_band, bias * LOG2E, NEG)


def _band_bias(table):
    n = table.shape[0] // 2
    return pl.pallas_call(
        _band_bias_kernel,
        grid=(n,),
        in_specs=[_resident(table.shape)],
        out_specs=pl.BlockSpec((1, BAND_WIN, 2 * BAND_Q), lambda r: (r, 0, 0)),
        out_shape=jax.ShapeDtypeStruct((n, BAND_WIN, 2 * BAND_Q), F32),
        compiler_params=_params("arbitrary"),
        name="band_bias",
    )(table)


def _sample_bias_kernel(tab_ref, o_ref, *, t):
    rows = tab_ref[0]
    blocks = []
    for c in range(BIAS_LANES // LANES):
        shape = (rows.shape[0], LANES)
        frame = lax.broadcasted_iota(jnp.int32, shape, 0) % t
        kj = lax.broadcasted_iota(jnp.int32, shape, 1) + c * LANES
        idx = jnp.clip(BAND_PAST + frame - kj, -MAX_REL, MAX_REL) + MAX_REL
        blocks.append(_table_lookup(rows, idx))
    o_ref[0] = jnp.concatenate(blocks, axis=1) * LOG2E


def _sample_bias(table, t):
    n = table.shape[0] // N_HEADS_A
    per_row = jnp.repeat(table.reshape(n, N_HEADS_A, TABLE_LANES), t, axis=1)
    return pl.pallas_call(
        functools.partial(_sample_bias_kernel, t=t),
        grid=(n,),
        in_specs=[pl.BlockSpec((1, N_HEADS_A * t, TABLE_LANES), lambda r: (r, 0, 0))],
        out_specs=pl.BlockSpec((1, N_HEADS_A * t, BIAS_LANES), lambda r: (r, 0, 0)),
        out_shape=jax.ShapeDtypeStruct((n, N_HEADS_A * t, BIAS_LANES), F32),
        compiler_params=_params("arbitrary"),
        name="sample_bias",
    )(per_row)


def _band_kernel(q_ref, kp_ref, kc_ref, vp_ref, vc_ref, bias_ref, o_ref, *bufs):
    qblk = pl.program_id(2)
    n_sub = q_ref.shape[1] // BAND_Q

    def window(prev_ref, cur_ref, sub, first, axis):
        lo, hi = sub * BAND_Q, (sub + 1) * BAND_Q
        take = lambda ref, a, b: ref[0, a:b] if axis == 0 else ref[0, :, a:b]
        parts = []
        if lo < BAND_PAST and not first:
            parts.append(take(prev_ref, lo, BAND_PAST))
        parts.append(take(cur_ref, max(lo - BAND_PAST, 0), hi))
        return parts[0] if len(parts) == 1 else jnp.concatenate(parts, axis=axis)

    def scores(sub, first):
        keys = window(kp_ref, kc_ref, sub, first, 0)
        nk = keys.shape[0]
        q = q_ref[0, sub * BAND_Q:(sub + 1) * BAND_Q, :]
        s = lax.dot_general(keys, _split_maps(q), _NT, preferred_element_type=F32)
        s = s + bias_ref[0, BAND_WIN - nk:, :]
        bufs[2 * sub][:nk] = s
        bufs[2 * sub + 1][...] = jnp.max(s, axis=0, keepdims=True)
        return nk

    def finish(nk, sub, first):
        p = jnp.exp2(bufs[2 * sub][:nk] - bufs[2 * sub + 1][...]).astype(BF16)
        o_t = _values_and_sum(window(vp_ref, vc_ref, sub, first, 1), p)
        o_t = o_t[:LANES] / o_t[LANES:LANES + 1]
        o = jnp.concatenate([o_t[:HEAD_DIM, :BAND_Q], o_t[HEAD_DIM:, BAND_Q:]], axis=0).T
        o_ref[0, sub * BAND_Q:(sub + 1) * BAND_Q, :] = o.astype(BF16)

    def run(first):
        sizes = [scores(sub, first) for sub in range(n_sub)]
        for sub in range(n_sub):
            finish(sizes[sub], sub, first)

    pl.when(qblk == 0)(functools.partial(run, True))
    pl.when(qblk > 0)(functools.partial(run, False))


def _band_attn(q, k, v_t, bias, layer):
    b, s, _ = q.shape
    bq = 2 * BAND_PAST if s % (2 * BAND_PAST) == 0 else BAND_PAST
    n_pairs = N_HEADS_A // 2
    prev_of = lambda qi: jnp.maximum(qi * (bq // BAND_PAST) - 1, 0)
    cur = lambda hp, bi, qi: (bi, qi, hp)
    prev = lambda hp, bi, qi: (bi, prev_of(qi), hp)
    cur_t = lambda hp, bi, qi: (bi, hp, qi)
    prev_t = lambda hp, bi, qi: (bi, hp, prev_of(qi))
    blk = (1, bq, LANES)
    return pl.pallas_call(
        _band_kernel,
        grid=(n_pairs, b, s // bq),
        in_specs=[pl.BlockSpec(blk, cur), pl.BlockSpec((1, BAND_PAST, LANES), prev), pl.BlockSpec(blk, cur),
                  pl.BlockSpec((1, LANES, BAND_PAST), prev_t), pl.BlockSpec((1, LANES, bq), cur_t),
                  pl.BlockSpec((1, BAND_WIN, 2 * BAND_Q), lambda hp, bi, qi: (layer * n_pairs + hp, 0, 0))],
        out_specs=pl.BlockSpec(blk, cur),
        out_shape=jax.ShapeDtypeStruct((b, s, WIDTH), BF16),
        scratch_shapes=[pltpu.VMEM((BAND_WIN, 2 * BAND_Q), F32), pltpu.VMEM((1, 2 * BAND_Q), F32)]
                       * (bq // BAND_Q),
        compiler_params=_params("arbitrary", "arbitrary", "arbitrary"),
        name="band_attn",
    )(q, k, k, v_t, v_t, bias)


def _lambda(lam_ref, lam_init):
    lp = lam_ref[...]
    a1 = jnp.sum(lp[0:1] * lp[1:2], axis=-1, keepdims=True)
    a2 = jnp.sum(lp[2:3] * lp[3:4], axis=-1, keepdims=True)
    return jnp.exp(a1) - jnp.exp(a2) + lam_init


def _diff_kernel(lam_ref, g_ref, q_ref, qn_ref, k_ref, v_ref, o_ref, m_ref, acc_ref, *bufs, lam_init):
    qi = pl.program_id(2)
    q = q_ref[0]
    bq = q.shape[0]
    bk = bq // 2
    q2 = _split_maps(q)
    queries = (q2[:bq], q2[bq:])

    m_ref[...] = jnp.full(m_ref.shape, NEG, F32)
    acc_ref[...] = jnp.zeros(acc_ref.shape, F32)

    def block(ref, kj):
        return ref[0, pl.ds(pl.multiple_of(kj * bk, bk), bk), :]

    def scores(kj, stream, buf, diagonal=None, queries=queries):
        s_ref, mx_ref = bufs[4 * stream + 2 * buf], bufs[4 * stream + 2 * buf + 1]
        qs = queries[stream][bk:] if diagonal == 1 else queries[stream]
        s = lax.dot_general(block(k_ref, kj), qs, _NT, preferred_element_type=F32)
        if diagonal is not None:
            k_chunk = lax.broadcasted_iota(jnp.int32, s.shape, 0) // CHUNK
            q_chunk = lax.broadcasted_iota(jnp.int32, s.shape, 1) // CHUNK
            s = jnp.where(k_chunk <= q_chunk, s, NEG)
        n = s.shape[1]
        s_ref[:, :n] = s
        mx_ref[:, :n] = jnp.max(s, axis=0, keepdims=True)

    def absorb(kj, stream, buf, second_half=False):
        s_ref, mx_ref = bufs[4 * stream + 2 * buf], bufs[4 * stream + 2 * buf + 1]
        n = bk if second_half else bq
        cols = slice((stream + 1) * bq - n, (stream + 1) * bq)
        m_old = m_ref[:, cols]
        m_new = jnp.maximum(m_old, mx_ref[:, :n])
        alpha = jnp.exp2(m_old - m_new)
        p = jnp.exp2(s_ref[:, :n] - m_new).astype(BF16)
        acc_ref[:, cols] = alpha * acc_ref[:, cols] + _values_and_sum(v_ref[0, kj], p)
        m_ref[:, cols] = m_new

    def stage(kj, buf, look=True, look_diagonal=None, second_half=False):
        if look:
            scores(kj + 1, 0, 1 - buf, look_diagonal)
        absorb(kj, 1, buf, second_half)
        if look:
            scores(kj + 1, 1, 1 - buf, look_diagonal)
        absorb(kj, 0, buf, second_half)

    @pl.when(qi == 0)
    def _():
        scores(0, 0, 0, diagonal=0)
        scores(0, 1, 0, diagonal=0)
        stage(0, 0, look_diagonal=1)
        stage(1, 1, look=False, second_half=True)


    def two_blocks(pair, carry):
        stage(2 * pair, 0)
        stage(2 * pair + 1, 1)
        return carry

    lax.fori_loop(0, jnp.maximum(qi - 1, 0), two_blocks, 0)

    @pl.when(qi > 0)
    def _():
        last = 2 * qi
        stage(last - 2, 0)
        stage(last - 1, 1, look_diagonal=0)
        stage(last, 0, look_diagonal=1)
        stage(last + 1, 1, look=False, second_half=True)

    def finalize():
        o_t = acc_ref[:LANES] / acc_ref[LANES:LANES + 1]
        lam = _lambda(lam_ref, lam_init)
        d_t = o_t[:, :bq] - lam * o_t[:, bq:]
        inv = lax.rsqrt(jnp.mean(d_t * d_t, axis=0, keepdims=True) + EPS)
        o_ref[0] = ((d_t * inv).T * (g_ref[...] * (1.0 - lam_init))).astype(BF16)

    @pl.when(qi + 1 < pl.num_programs(2))
    def _():
        nxt = _split_maps(qn_ref[0])
        scores(0, 0, 0, queries=(nxt[:bq], nxt[bq:]))
        scores(0, 1, 0, queries=(nxt[:bq], nxt[bq:]))
        finalize()

    pl.when(qi + 1 == pl.num_programs(2))(finalize)


def _diff_attn(q, k, v_t, lam_p, g, lam_init):
    b, s, _ = q.shape
    bq = 2 * DIFF_KEYS
    assert s % bq == 0 and v_t.shape[3] == DIFF_KEYS
    seq = pl.BlockSpec((1, s, LANES), lambda bi, h, qi: (bi, 0, h))
    seq_t = pl.BlockSpec((1, s // DIFF_KEYS, LANES, DIFF_KEYS), lambda bi, h, qi: (bi, 0, h, 0))
    blk = pl.BlockSpec((1, bq, LANES), lambda bi, h, qi: (bi, qi, h))
    return pl.pallas_call(
        functools.partial(_diff_kernel, lam_init=lam_init),
        grid=(b, N_HEADS_B, s // bq),
        in_specs=[_resident(lam_p.shape), _resident(g.shape), blk,
                  pl.BlockSpec((1, bq, LANES), lambda bi, h, qi: (bi, jnp.minimum(qi + 1, s // bq - 1), h)),
                  seq, seq_t],
        out_specs=blk,
        out_shape=jax.ShapeDtypeStruct((b, s, WIDTH), BF16),
        scratch_shapes=[pltpu.VMEM((1, 2 * bq), F32), pltpu.VMEM((LANES + SUM_ROWS, 2 * bq), F32),
                        ] + [pltpu.VMEM((DIFF_KEYS, bq), F32), pltpu.VMEM((1, bq), F32)] * 4,
        compiler_params=_params("arbitrary", "arbitrary", "arbitrary"),
        name="diff_attn",
    )(lam_p, g, q, q, k, v_t)


def _sample_kernel(lam_ref, g_ref, qa_ref, kan_ref, van_ref, qb_ref, kbn_ref, vbn_ref,
                   cak_ref, cav_ref, cbk_ref, cbv_ref, bias_c_ref, bias_n_ref,
                   oa_ref, ob_ref, *, lam_init):
    t = qa_ref.shape[0]
    n_grp = WIDTH // HEAD_DIM
    rows = n_grp * t
    grp_of_row = lax.broadcasted_iota(jnp.int32, (rows, WIDTH), 0) // t
    grp_of_lane = lax.broadcasted_iota(jnp.int32, (rows, WIDTH), 1) // HEAD_DIM
    own = grp_of_row == grp_of_lane
    lane_t = lax.broadcasted_iota(jnp.int32, (t, WIDTH), 1)

    def expand(q):
        qe = jnp.concatenate([q] * n_grp, axis=0)
        return jnp.where(own, qe, jnp.zeros_like(qe))

    def attend(qe, kt_cache, k_new, v_cache, v_new, bias_c, bias_n, v_dims):
        s_c = jnp.dot(qe, kt_cache, preferred_element_type=F32)
        s_n = lax.dot_general(qe, k_new, _NT, preferred_element_type=F32)
        if bias_c is not None:
            s_c = s_c + bias_c
            s_n = s_n + bias_n
        m = jnp.maximum(jnp.max(s_c, axis=-1, keepdims=True), jnp.max(s_n, axis=-1, keepdims=True))
        p_c = jnp.exp2(s_c - m)
        p_n = jnp.exp2(s_n - m)
        l = jnp.sum(p_c, axis=-1, keepdims=True) + jnp.sum(p_n, axis=-1, keepdims=True)
        o = (lax.dot_general(p_c.astype(BF16), v_cache, v_dims, preferred_element_type=F32)
             + jnp.dot(p_n.astype(BF16), v_new, preferred_element_type=F32))
        return o / l

    oa_all = attend(expand(qa_ref[...]), cak_ref[0, 0].astype(BF16), kan_ref[...],
                    cav_ref[0, 0].astype(BF16), van_ref[...], bias_c_ref[...], bias_n_ref[...], _NT)
    oa = jnp.zeros((t, WIDTH), F32)
    for h in range(N_HEADS_A):
        oa = jnp.where(lane_t // HEAD_DIM == h, oa_all[h * t:(h + 1) * t], oa)
    oa_ref[...] = oa.astype(BF16)

    past = cbk_ref.shape[3]
    v_cache = jnp.concatenate([cbv_ref[pl.ds(h, past, stride=N_HEADS_B), :] for h in range(N_HEADS_B)],
                              axis=1).astype(BF16)
    ob_all = attend(expand(qb_ref[...]), cbk_ref[0, 0].astype(BF16), kbn_ref[...],
                    v_cache, vbn_ref[...], None, None, _NN)
    lam = _lambda(lam_ref, lam_init)
    g = g_ref[...]
    outs = []
    for h in range(N_HEADS_B):
        d = ob_all[2 * h * t:(2 * h + 1) * t] - lam * ob_all[(2 * h + 1) * t:(2 * h + 2) * t]
        outs.append(_rms(d[:, h * LANES:(h + 1) * LANES], g) * (1.0 - lam_init))
    ob_ref[...] = jnp.concatenate(outs, axis=1).astype(BF16)


def _sample_attn(qa, ka, va, qb, kb, vb, cak, cav, cbk, cbv, bias_c, bias_n, lam_p, g, lam_init, t, layer):
    m = qa.shape[0]
    bs = m // t
    past = cbk.shape[3]
    new = pl.BlockSpec((t, WIDTH), lambda i: (i, 0))
    cache = lambda c: pl.BlockSpec((1, 1) + c.shape[2:], lambda i: (layer, i, 0, 0))
    cbv_spec = pl.BlockSpec((past * N_HEADS_B, LANES), lambda i: (layer * bs + i, 0))
    out = jax.ShapeDtypeStruct((m, WIDTH), BF16)
    return pl.pallas_call(
        functools.partial(_sample_kernel, lam_init=lam_init),
        grid=(bs,),
        in_specs=[_resident(lam_p.shape), _resident(g.shape)] + [new] * 6
                 + [cache(cak), cache(cav), cache(cbk), cbv_spec,
                    _resident(bias_c.shape), _resident(bias_n.shape)],
        out_specs=[new, new],
        out_shape=[out, out],
        compiler_params=_params("arbitrary"),
        name="sample_attn",
    )(lam_p, g, qa, ka, va, qb, kb, vb, cak, cav, cbk, cbv, bias_c, bias_n)


def _conv_kernel(x_ref, g_ref, w_ref, cw_ref, st_ref, y_ref, tail_ref, carry_ref, *, tps, seg_len):
    tm, d = x_ref.shape
    h = _rms(x_ref[...], g_ref[...]).astype(BF16)
    carried = seg_len >= tm
    if carried:
        @pl.when(pl.program_id(0) % tps == 0)
        def _():
            carry_ref[...] = st_ref[0]

    cw = cw_ref[...]
    cc = 2 * FF_CHUNK
    for c in range(d // cc):
        cols = slice(c * cc, (c + 1) * cc)
        proj = lambda part: jnp.dot(h, w_ref[:, part * d + c * cc:part * d + (c + 1) * cc],
                                    preferred_element_type=F32)
        xin = proj(1) * proj(2)
        r = lax.broadcasted_iota(jnp.int32, xin.shape, 0)
        x1 = pltpu.roll(xin, 1, 0)
        x2 = pltpu.roll(xin, 2, 0)
        if carried:
            prev = carry_ref[:, cols]
            x1 = jnp.where(r == 0, prev[7:8], x1)
            x2 = jnp.where(r == 0, prev[6:7], jnp.where(r == 1, prev[7:8], x2))
            carry_ref[:, cols] = xin[tm - 8:]
            tail_ref[0, :, cols] = xin[tm - 8:]
        else:
            t = r % seg_len
            x1 = jnp.where(t >= 1, x1, st_ref[0, :, cols])
            x2 = jnp.where(t >= 2, x2, st_ref[1, :, cols])
            tail_ref[:, cols] = xin
        conv = cw[0:1, cols] * x2 + cw[1:2, cols] * x1 + cw[2:3, cols] * xin
        y_ref[:, cols] = (proj(0) * conv).astype(BF16)


def _conv_in(x, g, w, cw, state, *, seg_len):
    m, d = x.shape
    tm = min(ROW_TILE, m)
    row = lambda i: (i, 0)
    if seg_len >= tm:
        tps = seg_len // tm
        st_spec = pl.BlockSpec((1, 8, d), lambda i: (i // tps, 0, 0))
        tail_spec = pl.BlockSpec((1, 8, d), lambda i: (i // tps, 0, 0))
        tail_shape = jax.ShapeDtypeStruct((m // seg_len, 8, d), F32)
    else:
        tps = 1
        st_spec = pl.BlockSpec((2, tm, d), lambda i: (0, i, 0))
        tail_spec = pl.BlockSpec((tm, d), row)
        tail_shape = jax.ShapeDtypeStruct((m, d), F32)
    return pl.pallas_call(
        functools.partial(_conv_kernel, tps=tps, seg_len=seg_len),
        grid=(m // tm,),
        in_specs=[pl.BlockSpec((tm, d), row), _resident((1, d)), _layer_of(*w),
                  _resident(cw.shape), st_spec],
        out_specs=[pl.BlockSpec((tm, d), row), tail_spec],
        out_shape=[jax.ShapeDtypeStruct((m, d), BF16), tail_shape],
        scratch_shapes=[pltpu.VMEM((8, d), F32)],
        compiler_params=_params("arbitrary"),
        name="conv_in",
    )(x, g, w[0], cw, state)


def _ffn_kernel(*refs, n_parts, final):
    x_ref = refs[0]
    a_refs = refs[1:1 + n_parts]
    wo_ref, g_ref, wg_ref, wu_ref, wd_ref = refs[1 + n_parts:6 + n_parts]
    gf_ref = refs[6 + n_parts] if final else None
    o_ref, act_ref = refs[-2], refs[-1]

    tm = x_ref.shape[0]
    halves = [slice(0, tm // 2), slice(tm // 2, tm)] if tm % 32 == 0 else [slice(0, tm)]
    x1s = []
    for rows in halves:
        a = jnp.concatenate([r[rows, :] for r in a_refs], axis=1) if n_parts > 1 else a_refs[0][rows, :]
        x1s.append(x_ref[rows, :] + jnp.dot(a, wo_ref[...], preferred_element_type=F32))
    d_ff = wg_ref.shape[1]
    for rows, x1 in zip(halves, x1s):
        h = _rms(x1, g_ref[...]).astype(BF16)
        for c in range(d_ff // FF_CHUNK):
            cols = slice(c * FF_CHUNK, (c + 1) * FF_CHUNK)
            gate = jnp.dot(h, wg_ref[:, cols], preferred_element_type=F32)
            up = jnp.dot(h, wu_ref[:, cols], preferred_element_type=F32)
            act_ref[rows, cols] = (gate * (1.0 / (1.0 + jnp.exp(-gate))) * up).astype(BF16)
        y = x1 + jnp.dot(act_ref[rows, :], wd_ref[...], preferred_element_type=F32)
        o_ref[rows, :] = _rms(y, gf_ref[...]) if final else y


def _mix_out_ffn(x, parts, wo, g, wg, wu, wd, g_final=None):
    m, d = x.shape
    tm = min(ROW_TILE, m)
    row = lambda i: (i, 0)
    final = g_final is not None
    extra = [g_final] if final else []
    return pl.pallas_call(
        functools.partial(_ffn_kernel, n_parts=len(parts), final=final),
        grid=(m // tm,),
        in_specs=[pl.BlockSpec((tm, d), row)] + [pl.BlockSpec((tm, p.shape[1]), row) for p in parts]
                 + [_layer_of(*wo), _resident((1, d)), _layer_of(*wg), _layer_of(*wu), _layer_of(*wd)]
                 + [_resident((1, d))] * len(extra),
        out_specs=pl.BlockSpec((tm, d), row),
        out_shape=jax.ShapeDtypeStruct((m, d), F32),
        scratch_shapes=[pltpu.VMEM((tm, wg[0].shape[2]), BF16)],
        compiler_params=_params("arbitrary"),
        name="mix_out_ffn",
    )(x, *parts, wo[0], g, wg[0], wu[0], wd[0], *extra)


def kernel(x_prompt, x_sample, cache_a_k, cache_a_v, cache_b_k, cache_b_v, state_conv, norm_mix, norm_ffn, norm_final, w_attn_in, w_attn_out, rel_bias, lambda_q1, lambda_k1, lambda_q2, lambda_k2, subln_g, w_conv_in, conv_w, w_conv_out, w_ffn_gate, w_ffn_up, w_ffn_down):
    b, s, d = x_prompt.shape
    bs, t, _ = x_sample.shape
    depth = norm_mix.shape[0]
    n_attn = w_attn_in.shape[0]
    n_conv = w_conv_in.shape[0]
    past = cache_b_k.shape[2]
    a_past = cache_a_k.shape[2]
    keep = min(BAND_PAST, s)
    assert s % BAND_PAST == 0 and a_past == BAND_PAST and d % (2 * FF_CHUNK) == 0
    assert w_ffn_gate.shape[2] % FF_CHUNK == 0 and 2 <= t and a_past + t <= BIAS_LANES

    xp = x_prompt.reshape(b * s, d)
    xs = x_sample.reshape(bs * t, d)
    row = lambda v: v.reshape(1, -1)

    cos_p, sin_p = _rope_tables(jnp.arange(s))
    cos_s, sin_s = (jnp.tile(tab, (bs, 1)) for tab in _rope_tables(past + jnp.arange(t)))

    table = jnp.pad(rel_bias.reshape(n_attn * N_HEADS_A, 2 * MAX_REL + 1),
                    ((0, 0), (0, TABLE_LANES - 2 * MAX_REL - 1)))
    band_bias = _band_bias(table)
    sample_bias = _sample_bias(table, t)

    cak_t = cache_a_k.transpose(0, 1, 3, 4, 2).reshape(n_attn, bs, WIDTH, a_past)
    cav_t = cache_a_v.transpose(0, 1, 3, 4, 2).reshape(n_attn, bs, WIDTH, a_past)
    cbk_t = cache_b_k.transpose(0, 1, 3, 4, 5, 2).reshape(n_attn, bs, WIDTH, past)
    cbv_rows = cache_b_v.reshape(n_attn * bs * past * N_HEADS_B, LANES)

    attn_in_w, attn_out_w = w_attn_in.astype(BF16), w_attn_out.astype(BF16)
    conv_in_w, conv_out_w = w_conv_in.astype(BF16), w_conv_out.astype(BF16)
    gate_w, up_w, down_w = w_ffn_gate.astype(BF16), w_ffn_up.astype(BF16), w_ffn_down.astype(BF16)

    stacked_p = stacked_s = None
    pcs, scs = [], []
    for i in range(depth):
        j = i // 2
        g_mix = row(norm_mix[i])
        if i % 2 == 0:
            lam_init = 0.8 - 0.6 * math.exp(-0.3 * i)
            w_in = (attn_in_w, j)
            wo = (attn_out_w, j)
            lam_p = jnp.stack([lambda_q1[j], lambda_k1[j], lambda_q2[j], lambda_k2[j]])
            g_sub = row(subln_g[j])

            outs = _attn_in_proj(xp, g_mix, w_in, cos_p, sin_p, stacked_p,
                                 seg_len=s, keep=keep, layer=j, n_layers=n_attn, native=True)
            qa, ka, va_t, qb, kb, vb_t = outs[:6]
            qa, ka, qb, kb = (o.reshape(b, s, WIDTH) for o in (qa, ka, qb, kb))
            stacked_p = outs[6:]
            oa = _band_attn(qa, ka, va_t, band_bias, j)
            ob = _diff_attn(qb, kb, vb_t, lam_p, g_sub, lam_init)
            parts_p = [oa.reshape(b * s, WIDTH), ob.reshape(b * s, WIDTH)]

            outs = _attn_in_proj(xs, g_mix, w_in, cos_s, sin_s, stacked_s,
                                 seg_len=t, keep=t, layer=j, n_layers=n_attn, native=False)
            stacked_s = outs[6:]
            oa, ob = _sample_attn(
                *outs[:6], cak_t, cav_t, cbk_t, cbv_rows,
                sample_bias[j, :, :a_past], sample_bias[j, :, a_past:a_past + t],
                lam_p, g_sub, lam_init, t, j)
            parts_s = [oa, ob]
        else:
            w_in = (conv_in_w, j)
            wo = (conv_out_w, j)
            yp, tail = _conv_in(xp, g_mix, w_in, conv_w[j], jnp.zeros((b, 8, d), F32), seg_len=s)
            pcs.append(tail[:, 6:])
            st = state_conv[j]
            stand_in = jnp.stack([jnp.pad(st[:, 1:2], ((0, 0), (0, t - 1), (0, 0))),
                                  jnp.pad(st, ((0, 0), (0, t - 2), (0, 0)))]).reshape(2, bs * t, d)
            ys, xin = _conv_in(xs, g_mix, w_in, conv_w[j], stand_in, seg_len=t)
            scs.append(xin.reshape(bs, t, d)[:, t - 2:])
            parts_p, parts_s = [yp], [ys]

        ffn = (wo, row(norm_ffn[i]), (gate_w, i), (up_w, i), (down_w, i))
        g_final = row(norm_final) if i == depth - 1 else None
        xp = _mix_out_ffn(xp, parts_p, *ffn, g_final=g_final)
        xs = _mix_out_ffn(xs, parts_s, *ffn, g_final=g_final)

    pak, pav, pbk, pbv = stacked_p
    sak, sav, sbk, sbv = stacked_s
    heads_a = lambda z: z.reshape(n_attn, b, N_HEADS_A, HEAD_DIM, keep).transpose(0, 1, 4, 2, 3)
    return (xp.reshape(b, s, d), xs.reshape(bs, t, d),
            heads_a(pak), heads_a(pav),
            pbk.reshape(n_attn, b, N_HEADS_B, 2, HEAD_DIM, s).transpose(0, 1, 5, 2, 3, 4),
            pbv.reshape(n_attn, b, s, N_HEADS_B, 2 * HEAD_DIM),
            jnp.stack(pcs),
            sak.reshape(n_attn, bs, t, N_HEADS_A, HEAD_DIM), sav.reshape(n_attn, bs, t, N_HEADS_A, HEAD_DIM),
            sbk.reshape(n_attn, bs, t, N_HEADS_B, 2, HEAD_DIM), sbv.reshape(n_attn, bs, t, N_HEADS_B, 2 * HEAD_DIM),
            jnp.stack(scs))
```

```python
import functools
import math

import jax
import jax.numpy as jnp
from jax import lax
from jax.experimental import pallas as pl
from jax.experimental.pallas import tpu as pltpu

F32 = jnp.float32
BF16 = jnp.bfloat16

EPS = 1e-6
NEG = -1e30
LOG2E = 1.4426950408889634
ROPE_THETA = 10000.0
CHUNK = 64
HEAD_DIM = 64
N_HEADS_A = 8
N_HEADS_B = 4
BAND_PAST = 512
MAX_REL = 128
BAND_KEYS = BAND_PAST + CHUNK
BAND_Q = 256
BAND_WIN = BAND_Q + BAND_PAST
DIFF_KEYS = 512
BIAS_LANES = 640
WIDTH = N_HEADS_A * HEAD_DIM
LANES = 128
ROW_TILE = 512
FF_CHUNK = 256
V7X_VMEM_BYTES = 64 * 1024 * 1024
VMEM_LIMIT = V7X_VMEM_BYTES - 8 * 1024 * 1024

_NT = (((1,), (1,)), ((), ()))
_TN = (((0,), (0,)), ((), ()))
_NN = (((1,), (0,)), ((), ()))


def _resident(shape):
    return pl.BlockSpec(shape, lambda *_: (0,) * len(shape), pipeline_mode=pl.Buffered(1))


def _layer_of(stack, layer):
    return pl.BlockSpec((None,) + stack.shape[1:], lambda *_: (layer, 0, 0), pipeline_mode=pl.Buffered(1))


def _rms(x, g):
    return x * lax.rsqrt(jnp.mean(x * x, axis=-1, keepdims=True) + EPS) * g


def _params(*sem):
    return pltpu.CompilerParams(dimension_semantics=sem, vmem_limit_bytes=VMEM_LIMIT)


SUM_ROWS = 16


def _values_and_sum(v_t, p):
    ones = jnp.ones((SUM_ROWS, v_t.shape[1]), v_t.dtype)
    return jnp.dot(jnp.concatenate([v_t, ones], axis=0), p, preferred_element_type=F32)


TABLE_LANES = 384


def _table_lookup(rows, idx):
    lane = idx & (LANES - 1)
    group = idx >> (LANES.bit_length() - 1)
    out = None
    for g in range(TABLE_LANES // LANES):
        picked = jnp.take_along_axis(rows[:, g * LANES:(g + 1) * LANES], lane, axis=1)
        out = picked if out is None else jnp.where(group == g, picked, out)
    return out


def _split_maps(q):
    low = lax.broadcasted_iota(jnp.int32, q.shape, 1) < HEAD_DIM
    zero = jnp.zeros_like(q)
    return jnp.concatenate([jnp.where(low, q, zero), jnp.where(low, zero, q)], axis=0)


def _attn_in_kernel(*refs, native):
    x_ref, g_ref, w_ref, cos_ref, sin_ref = refs[:5]
    (qa_ref, ka_ref, va_ref, qb_ref, kb_ref, vb_ref,
     kaf_ref, vaf_ref, kbf_ref, vbf_ref) = refs[-10:]
    q_scale = LOG2E / math.sqrt(HEAD_DIM)
    tm = x_ref.shape[0]
    halves = [(0, tm // 2), (tm // 2, tm)] if tm % 32 == 0 else [(0, tm)]
    normed = [_rms(x_ref[lo:hi, :], g_ref[...]).astype(BF16) for lo, hi in halves]

    for (lo, hi), h in zip(halves, normed):
        def proj(c):
            return jnp.dot(h, w_ref[:, c * WIDTH:(c + 1) * WIDTH], preferred_element_type=F32)

        cos = cos_ref[lo:hi, :]
        sin = sin_ref[lo:hi, :]
        low_half = (lax.broadcasted_iota(jnp.int32, cos.shape, 1) % HEAD_DIM) < (HEAD_DIM // 2)

        def rotary(z):
            parts = []
            for j in range(WIDTH // LANES):
                zj = z[:, j * LANES:(j + 1) * LANES]
                swapped = jnp.where(low_half, pltpu.roll(zj, LANES - HEAD_DIM // 2, 1),
                                    pltpu.roll(zj, HEAD_DIM // 2, 1))
                parts.append(zj * cos + swapped * sin)
            return jnp.concatenate(parts, axis=1)

        vb = proj(5)
        if native:
            vb_ref[0, 0, :, lo:hi] = vb.T.astype(BF16)
            for head in range(N_HEADS_B):
                vbf_ref[pl.ds(lo * N_HEADS_B + head, hi - lo, stride=N_HEADS_B), :] = (
                    vb[:, head * LANES:(head + 1) * LANES])
        else:
            vb_ref[lo:hi, :] = vb.astype(BF16)
            vbf_ref[lo:hi, :] = vb
        kb = rotary(proj(4))
        kb_ref[lo:hi, :] = kb.astype(BF16)
        va = proj(2)
        ka = proj(1)
        ka_ref[lo:hi, :] = ka.astype(BF16)
        if native:
            kbf_ref[0, 0, :, lo:hi] = kb.T
            va_t = va.T
            vaf_ref[0, 0, :, lo:hi] = va_t
            va_ref[0, :, lo:hi] = va_t.astype(BF16)
            kaf_ref[0, 0, :, lo:hi] = ka.T
        else:
            kbf_ref[lo:hi, :] = kb
            vaf_ref[lo:hi, :] = va
            va_ref[lo:hi, :] = va.astype(BF16)
            kaf_ref[lo:hi, :] = ka
        qb_ref[lo:hi, :] = (rotary(proj(3)) * q_scale).astype(BF16)
        qa_ref[lo:hi, :] = (proj(0) * q_scale).astype(BF16)


def _attn_in_proj(x, g, w, cos_t, sin_t, stacked, *, seg_len, keep, layer, n_layers, native):
    m, d = x.shape
    tm = min(ROW_TILE, m)
    n_tiles = m // tm
    n_tab = cos_t.shape[0] // tm
    n_seg = m // seg_len
    row = lambda i: (i, 0)
    tab = lambda i: (i % n_tab, 0)
    blk = pl.BlockSpec((tm, WIDTH), row)
    bf_out = jax.ShapeDtypeStruct((m, WIDTH), BF16)
    if native:
        tps, kpt = seg_len // tm, keep // tm
        t_blk = (1, 1, WIDTH, tm)
        keep_spec = pl.BlockSpec(
            t_blk, lambda i: (layer, i // tps, 0, jnp.clip(i % tps - (tps - kpt), 0, kpt - 1)))
        keep_out = jax.ShapeDtypeStruct((n_layers, n_seg, WIDTH, keep), F32)
        kb_spec = pl.BlockSpec(t_blk, lambda i: (layer, i // tps, 0, i % tps))
        kb_out = jax.ShapeDtypeStruct((n_layers, n_seg, WIDTH, seg_len), F32)
        vb_spec = pl.BlockSpec((tm * N_HEADS_B, LANES), lambda i: (layer * n_tiles + i, 0))
        vb_out = jax.ShapeDtypeStruct((n_layers * m * N_HEADS_B, LANES), F32)
        va_spec = pl.BlockSpec((1, WIDTH, tm), lambda i: (i // tps, 0, i % tps))
        va_out = jax.ShapeDtypeStruct((n_seg, WIDTH, seg_len), BF16)
        vbb_spec = pl.BlockSpec((1, 1, WIDTH, tm), lambda i: (i // tps, i % tps, 0, 0))
        vbb_out = jax.ShapeDtypeStruct((n_seg, tps, WIDTH, tm), BF16)
    else:
        assert keep == seg_len
        keep_spec = kb_spec = vb_spec = pl.BlockSpec((tm, WIDTH), lambda i: (layer * n_tiles + i, 0))
        keep_out = kb_out = vb_out = jax.ShapeDtypeStruct((n_layers * m, WIDTH), F32)
        va_spec = vbb_spec = blk
        va_out = vbb_out = bf_out
    carried = [] if stacked is None else list(stacked)
    n_in = 5
    return pl.pallas_call(
        functools.partial(_attn_in_kernel, native=native),
        grid=(n_tiles,),
        in_specs=[pl.BlockSpec((tm, d), row), _resident((1, d)), _layer_of(*w),
                  pl.BlockSpec((tm, LANES), tab), pl.BlockSpec((tm, LANES), tab)]
                 + [pl.BlockSpec(memory_space=pl.ANY)] * len(carried),
        out_specs=[blk, blk, va_spec, blk, blk, vbb_spec, keep_spec, keep_spec, kb_spec, vb_spec],
        out_shape=[bf_out, bf_out, va_out, bf_out, bf_out, vbb_out, keep_out, keep_out, kb_out, vb_out],
        input_output_aliases={n_in + k: 6 + k for k in range(len(carried))},
        compiler_params=_params("arbitrary"),
        name="attn_in_proj",
    )(x, g, w[0], cos_t, sin_t, *carried)


def _rope_tables(pos):
    half = HEAD_DIM // 2
    inv = ROPE_THETA ** (-jnp.arange(half, dtype=F32) / half)
    ang = pos.astype(F32)[:, None] * inv[None, :]
    c, s = jnp.cos(ang), jnp.sin(ang)
    reps = LANES // HEAD_DIM
    return jnp.tile(c, (1, 2 * reps)), jnp.tile(jnp.concatenate([-s, s], axis=1), (1, reps))


def _band_bias_kernel(tab_ref, o_ref):
    r = pl.program_id(0)
    e_shape = (BAND_WIN + LANES, LANES)
    y = lax.broadcasted_iota(jnp.int32, e_shape, 0) - LANES
    lane = lax.broadcasted_iota(jnp.int32, e_shape, 1)
    idx = jnp.clip(BAND_PAST + lane - y, -MAX_REL, MAX_REL) + MAX_REL
    cols = []
    for hh in range(2):
        e = _table_lookup(jnp.broadcast_to(tab_ref[pl.ds(2 * r + hh, 1), :], (e_shape[0], TABLE_LANES)), idx)
        for c in range(BAND_Q // LANES):
            cols.append(e[LANES - LANES * c:LANES - LANES * c + BAND_WIN])
    bias = jnp.concatenate(cols, axis=1)

    kj = lax.broadcasted_iota(jnp.int32, bias.shape, 0)
    qi = lax.broadcasted_iota(jnp.int32, bias.shape, 1) % BAND_Q
    first_key = (qi // CHUNK) * CHUNK
    in_band = jnp.logical_and(kj >= first_key, kj < first_key + BAND_KEYS)
    o_ref[0] = jnp.where(in_band, bias * LOG2E, NEG)


def _band_bias(table):
    n = table.shape[0] // 2
    return pl.pallas_call(
        _band_bias_kernel,
        grid=(n,),
        in_specs=[_resident(table.shape)],
        out_specs=pl.BlockSpec((1, BAND_WIN, 2 * BAND_Q), lambda r: (r, 0, 0)),
        out_shape=jax.ShapeDtypeStruct((n, BAND_WIN, 2 * BAND_Q), F32),
        compiler_params=_params("arbitrary"),
        name="band_bias",
    )(table)


def _sample_bias_kernel(tab_ref, o_ref, *, t):
    rows = tab_ref[0]
    blocks = []
    for c in range(BIAS_LANES // LANES):
        shape = (rows.shape[0], LANES)
        frame = lax.broadcasted_iota(jnp.int32, shape, 0) % t
        kj = lax.broadcasted_iota(jnp.int32, shape, 1) + c * LANES
        idx = jnp.clip(BAND_PAST + frame - kj, -MAX_REL, MAX_REL) + MAX_REL
        blocks.append(_table_lookup(rows, idx))
    o_ref[0] = jnp.concatenate(blocks, axis=1) * LOG2E


def _sample_bias(table, t):
    n = table.shape[0] // N_HEADS_A
    per_row = jnp.repeat(table.reshape(n, N_HEADS_A, TABLE_LANES), t, axis=1)
    return pl.pallas_call(
        functools.partial(_sample_bias_kernel, t=t),
        grid=(n,),
        in_specs=[pl.BlockSpec((1, N_HEADS_A * t, TABLE_LANES), lambda r: (r, 0, 0))],
        out_specs=pl.BlockSpec((1, N_HEADS_A * t, BIAS_LANES), lambda r: (r, 0, 0)),
        out_shape=jax.ShapeDtypeStruct((n, N_HEADS_A * t, BIAS_LANES), F32),
        compiler_params=_params("arbitrary"),
        name="sample_bias",
    )(per_row)


def _band_kernel(q_ref, kp_ref, kc_ref, vp_ref, vc_ref, bias_ref, o_ref, *bufs):
    qblk = pl.program_id(2)
    n_sub = q_ref.shape[1] // BAND_Q

    def window(prev_ref, cur_ref, sub, first, axis):
        lo, hi = sub * BAND_Q, (sub + 1) * BAND_Q
        take = lambda ref, a, b: ref[0, a:b] if axis == 0 else ref[0, :, a:b]
        parts = []
        if lo < BAND_PAST and not first:
            parts.append(take(prev_ref, lo, BAND_PAST))
        parts.append(take(cur_ref, max(lo - BAND_PAST, 0), hi))
        return parts[0] if len(parts) == 1 else jnp.concatenate(parts, axis=axis)

    def scores(sub, first):
        keys = window(kp_ref, kc_ref, sub, first, 0)
        nk = keys.shape[0]
        q = q_ref[0, sub * BAND_Q:(sub + 1) * BAND_Q, :]
        s = lax.dot_general(keys, _split_maps(q), _NT, preferred_element_type=F32)
        s = s + bias_ref[0, BAND_WIN - nk:, :]
        bufs[2 * sub][:nk] = s
        bufs[2 * sub + 1][...] = jnp.max(s, axis=0, keepdims=True)
        return nk

    def finish(nk, sub, first):
        p = jnp.exp2(bufs[2 * sub][:nk] - bufs[2 * sub + 1][...]).astype(BF16)
        o_t = _values_and_sum(window(vp_ref, vc_ref, sub, first, 1), p)
        o_t = o_t[:LANES] / o_t[LANES:LANES + 1]
        o = jnp.concatenate([o_t[:HEAD_DIM, :BAND_Q], o_t[HEAD_DIM:, BAND_Q:]], axis=0).T
        o_ref[0, sub * BAND_Q:(sub + 1) * BAND_Q, :] = o.astype(BF16)

    def run(first):
        sizes = [scores(sub, first) for sub in range(n_sub)]
        for sub in range(n_sub):
            finish(sizes[sub], sub, first)

    pl.when(qblk == 0)(functools.partial(run, True))
    pl.when(qblk > 0)(functools.partial(run, False))


def _band_attn(q, k, v_t, bias, layer):
    b, s, _ = q.shape
    bq = 2 * BAND_PAST if s % (2 * BAND_PAST) == 0 else BAND_PAST
    n_pairs = N_HEADS_A // 2
    prev_of = lambda qi: jnp.maximum(qi * (bq // BAND_PAST) - 1, 0)
    cur = lambda hp, bi, qi: (bi, qi, hp)
    prev = lambda hp, bi, qi: (bi, prev_of(qi), hp)
    cur_t = lambda hp, bi, qi: (bi, hp, qi)
    prev_t = lambda hp, bi, qi: (bi, hp, prev_of(qi))
    blk = (1, bq, LANES)
    return pl.pallas_call(
        _band_kernel,
        grid=(n_pairs, b, s // bq),
        in_specs=[pl.BlockSpec(blk, cur), pl.BlockSpec((1, BAND_PAST, LANES), prev), pl.BlockSpec(blk, cur),
                  pl.BlockSpec((1, LANES, BAND_PAST), prev_t), pl.BlockSpec((1, LANES, bq), cur_t),
                  pl.BlockSpec((1, BAND_WIN, 2 * BAND_Q), lambda hp, bi, qi: (layer * n_pairs + hp, 0, 0))],
        out_specs=pl.BlockSpec(blk, cur),
        out_shape=jax.ShapeDtypeStruct((b, s, WIDTH), BF16),
        scratch_shapes=[pltpu.VMEM((BAND_WIN, 2 * BAND_Q), F32), pltpu.VMEM((1, 2 * BAND_Q), F32)]
                       * (bq // BAND_Q),
        compiler_params=_params("arbitrary", "arbitrary", "arbitrary"),
        name="band_attn",
    )(q, k, k, v_t, v_t, bias)


def _lambda(lam_ref, lam_init):
    lp = lam_ref[...]
    a1 = jnp.sum(lp[0:1] * lp[1:2], axis=-1, keepdims=True)
    a2 = jnp.sum(lp[2:3] * lp[3:4], axis=-1, keepdims=True)
    return jnp.exp(a1) - jnp.exp(a2) + lam_init


def _diff_kernel(lam_ref, g_ref, q_ref, qn_ref, k_ref, v_ref, o_ref, m_ref, acc_ref, *bufs, lam_init):
    qi = pl.program_id(2)
    q = q_ref[0]
    bq = q.shape[0]
    bk = bq // 2
    q2 = _split_maps(q)
    queries = (q2[:bq], q2[bq:])

    m_ref[...] = jnp.full(m_ref.shape, NEG, F32)
    acc_ref[...] = jnp.zeros(acc_ref.shape, F32)

    def block(ref, kj):
        return ref[0, pl.ds(pl.multiple_of(kj * bk, bk), bk), :]

    def scores(kj, stream, buf, diagonal=None, queries=queries):
        s_ref, mx_ref = bufs[4 * stream + 2 * buf], bufs[4 * stream + 2 * buf + 1]
        qs = queries[stream][bk:] if diagonal == 1 else queries[stream]
        s = lax.dot_general(block(k_ref, kj), qs, _NT, preferred_element_type=F32)
        if diagonal is not None:
            k_chunk = lax.broadcasted_iota(jnp.int32, s.shape, 0) // CHUNK
            q_chunk = lax.broadcasted_iota(jnp.int32, s.shape, 1) // CHUNK
            s = jnp.where(k_chunk <= q_chunk, s, NEG)
        n = s.shape[1]
        s_ref[:, :n] = s
        mx_ref[:, :n] = jnp.max(s, axis=0, keepdims=True)

    def absorb(kj, stream, buf, second_half=False):
        s_ref, mx_ref = bufs[4 * stream + 2 * buf], bufs[4 * stream + 2 * buf + 1]
        n = bk if second_half else bq
        cols = slice((stream + 1) * bq - n, (stream + 1) * bq)
        m_old = m_ref[:, cols]
        m_new = jnp.maximum(m_old, mx_ref[:, :n])
        alpha = jnp.exp2(m_old - m_new)
        p = jnp.exp2(s_ref[:, :n] - m_new).astype(BF16)
        acc_ref[:, cols] = alpha * acc_ref[:, cols] + _values_and_sum(v_ref[0, kj], p)
        m_ref[:, cols] = m_new

    def stage(kj, buf, look=True, look_diagonal=None, second_half=False):
        if look:
            scores(kj + 1, 0, 1 - buf, look_diagonal)
        absorb(kj, 1, buf, second_half)
        if look:
            scores(kj + 1, 1, 1 - buf, look_diagonal)
        absorb(kj, 0, buf, second_half)

    @pl.when(qi == 0)
    def _():
        scores(0, 0, 0, diagonal=0)
        scores(0, 1, 0, diagonal=0)
        stage(0, 0, look_diagonal=1)
        stage(1, 1, look=False, second_half=True)


    def two_blocks(pair, carry):
        stage(2 * pair, 0)
        stage(2 * pair + 1, 1)
        return carry

    lax.fori_loop(0, jnp.maximum(qi - 1, 0), two_blocks, 0)

    @pl.when(qi > 0)
    def _():
        last = 2 * qi
        stage(last - 2, 0)
        stage(last - 1, 1, look_diagonal=0)
        stage(last, 0, look_diagonal=1)
        stage(last + 1, 1, look=False, second_half=True)

    def finalize():
        o_t = acc_ref[:LANES] / acc_ref[LANES:LANES + 1]
        lam = _lambda(lam_ref, lam_init)
        d_t = o_t[:, :bq] - lam * o_t[:, bq:]
        inv = lax.rsqrt(jnp.mean(d_t * d_t, axis=0, keepdims=True) + EPS)
        o_ref[0] = ((d_t * inv).T * (g_ref[...] * (1.0 - lam_init))).astype(BF16)

    @pl.when(qi + 1 < pl.num_programs(2))
    def _():
        nxt = _split_maps(qn_ref[0])
        scores(0, 0, 0, queries=(nxt[:bq], nxt[bq:]))
        scores(0, 1, 0, queries=(nxt[:bq], nxt[bq:]))
        finalize()

    pl.when(qi + 1 == pl.num_programs(2))(finalize)


def _diff_attn(q, k, v_t, lam_p, g, lam_init):
    b, s, _ = q.shape
    bq = 2 * DIFF_KEYS
    assert s % bq == 0 and v_t.shape[3] == DIFF_KEYS
    seq = pl.BlockSpec((1, s, LANES), lambda bi, h, qi: (bi, 0, h))
    seq_t = pl.BlockSpec((1, s // DIFF_KEYS, LANES, DIFF_KEYS), lambda bi, h, qi: (bi, 0, h, 0))
    blk = pl.BlockSpec((1, bq, LANES), lambda bi, h, qi: (bi, qi, h))
    return pl.pallas_call(
        functools.partial(_diff_kernel, lam_init=lam_init),
        grid=(b, N_HEADS_B, s // bq),
        in_specs=[_resident(lam_p.shape), _resident(g.shape), blk,
                  pl.BlockSpec((1, bq, LANES), lambda bi, h, qi: (bi, jnp.minimum(qi + 1, s // bq - 1), h)),
                  seq, seq_t],
        out_specs=blk,
        out_shape=jax.ShapeDtypeStruct((b, s, WIDTH), BF16),
        scratch_shapes=[pltpu.VMEM((1, 2 * bq), F32), pltpu.VMEM((LANES + SUM_ROWS, 2 * bq), F32),
                        ] + [pltpu.VMEM((DIFF_KEYS, bq), F32), pltpu.VMEM((1, bq), F32)] * 4,
        compiler_params=_params("arbitrary", "arbitrary", "arbitrary"),
        name="diff_attn",
    )(lam_p, g, q, q, k, v_t)


def _sample_kernel(lam_ref, g_ref, qa_ref, kan_ref, van_ref, qb_ref, kbn_ref, vbn_ref,
                   cak_ref, cav_ref, cbk_ref, cbv_ref, bias_c_ref, bias_n_ref,
                   oa_ref, ob_ref, *, lam_init):
    t = qa_ref.shape[0]
    n_grp = WIDTH // HEAD_DIM
    rows = n_grp * t
    grp_of_row = lax.broadcasted_iota(jnp.int32, (rows, WIDTH), 0) // t
    grp_of_lane = lax.broadcasted_iota(jnp.int32, (rows, WIDTH), 1) // HEAD_DIM
    own = grp_of_row == grp_of_lane
    lane_t = lax.broadcasted_iota(jnp.int32, (t, WIDTH), 1)

    def expand(q):
        qe = jnp.concatenate([q] * n_grp, axis=0)
        return jnp.where(own, qe, jnp.zeros_like(qe))

    def attend(qe, kt_cache, k_new, v_cache, v_new, bias_c, bias_n, v_dims):
        s_c = jnp.dot(qe, kt_cache, preferred_element_type=F32)
        s_n = lax.dot_general(qe, k_new, _NT, preferred_element_type=F32)
        if bias_c is not None:
            s_c = s_c + bias_c
            s_n = s_n + bias_n
        m = jnp.maximum(jnp.max(s_c, axis=-1, keepdims=True), jnp.max(s_n, axis=-1, keepdims=True))
        p_c = jnp.exp2(s_c - m)
        p_n = jnp.exp2(s_n - m)
        l = jnp.sum(p_c, axis=-1, keepdims=True) + jnp.sum(p_n, axis=-1, keepdims=True)
        o = (lax.dot_general(p_c.astype(BF16), v_cache, v_dims, preferred_element_type=F32)
             + jnp.dot(p_n.astype(BF16), v_new, preferred_element_type=F32))
        return o / l

    oa_all = attend(expand(qa_ref[...]), cak_ref[0, 0].astype(BF16), kan_ref[...],
                    cav_ref[0, 0].astype(BF16), van_ref[...], bias_c_ref[...], bias_n_ref[...], _NT)
    oa = jnp.zeros((t, WIDTH), F32)
    for h in range(N_HEADS_A):
        oa = jnp.where(lane_t // HEAD_DIM == h, oa_all[h * t:(h + 1) * t], oa)
    oa_ref[...] = oa.astype(BF16)

    past = cbk_ref.shape[3]
    v_cache = jnp.concatenate([cbv_ref[pl.ds(h, past, stride=N_HEADS_B), :] for h in range(N_HEADS_B)],
                              axis=1).astype(BF16)
    ob_all = attend(expand(qb_ref[...]), cbk_ref[0, 0].astype(BF16), kbn_ref[...],
                    v_cache, vbn_ref[...], None, None, _NN)
    lam = _lambda(lam_ref, lam_init)
    g = g_ref[...]
    outs = []
    for h in range(N_HEADS_B):
        d = ob_all[2 * h * t:(2 * h + 1) * t] - lam * ob_all[(2 * h + 1) * t:(2 * h + 2) * t]
        outs.append(_rms(d[:, h * LANES:(h + 1) * LANES], g) * (1.0 - lam_init))
    ob_ref[...] = jnp.concatenate(outs, axis=1).astype(BF16)


def _sample_attn(qa, ka, va, qb, kb, vb, cak, cav, cbk, cbv, bias_c, bias_n, lam_p, g, lam_init, t, layer):
    m = qa.shape[0]
    bs = m // t
    past = cbk.shape[3]
    new = pl.BlockSpec((t, WIDTH), lambda i: (i, 0))
    cache = lambda c: pl.BlockSpec((1, 1) + c.shape[2:], lambda i: (layer, i, 0, 0))
    cbv_spec = pl.BlockSpec((past * N_HEADS_B, LANES), lambda i: (layer * bs + i, 0))
    out = jax.ShapeDtypeStruct((m, WIDTH), BF16)
    return pl.pallas_call(
        functools.partial(_sample_kernel, lam_init=lam_init),
        grid=(bs,),
        in_specs=[_resident(lam_p.shape), _resident(g.shape)] + [new] * 6
                 + [cache(cak), cache(cav), cache(cbk), cbv_spec,
                    _resident(bias_c.shape), _resident(bias_n.shape)],
        out_specs=[new, new],
        out_shape=[out, out],
        compiler_params=_params("arbitrary"),
        name="sample_attn",
    )(lam_p, g, qa, ka, va, qb, kb, vb, cak, cav, cbk, cbv, bias_c, bias_n)


def _conv_kernel(x_ref, xn_ref, g_ref, w_ref, cw_ref, st_ref, y_ref, tail_ref, carry_ref, h0_ref, h1_ref,
                 *, tps, seg_len):
    tm, d = x_ref.shape
    step = pl.program_id(0)
    carried = seg_len >= tm
    if carried:
        @pl.when(step % tps == 0)
        def _():
            carry_ref[...] = st_ref[0]

        @pl.when(step == 0)
        def _():
            h0_ref[...] = _rms(x_ref[...], g_ref[...]).astype(BF16)

    def run(h_ref, next_ref):
        if next_ref is not None:
            next_ref[...] = _rms(xn_ref[...], g_ref[...]).astype(BF16)
        h = h_ref[...] if h_ref is not None else _rms(x_ref[...], g_ref[...]).astype(BF16)
        cw = cw_ref[...]
        cc = 2 * FF_CHUNK
        for c in range(d // cc):
            cols = slice(c * cc, (c + 1) * cc)
            proj = lambda part: jnp.dot(h, w_ref[:, part * d + c * cc:part * d + (c + 1) * cc],
                                        preferred_element_type=F32)
            xin = proj(1) * proj(2)
            r = lax.broadcasted_iota(jnp.int32, xin.shape, 0)
            x1 = pltpu.roll(xin, 1, 0)
            x2 = pltpu.roll(xin, 2, 0)
            if carried:
                prev = carry_ref[:, cols]
                x1 = jnp.where(r == 0, prev[7:8], x1)
                x2 = jnp.where(r == 0, prev[6:7], jnp.where(r == 1, prev[7:8], x2))
                carry_ref[:, cols] = xin[tm - 8:]
                tail_ref[0, :, cols] = xin[tm - 8:]
            else:
                t = r % seg_len
                x1 = jnp.where(t >= 1, x1, st_ref[0, :, cols])
                x2 = jnp.where(t >= 2, x2, st_ref[1, :, cols])
                tail_ref[:, cols] = xin
            conv = cw[0:1, cols] * x2 + cw[1:2, cols] * x1 + cw[2:3, cols] * xin
            y_ref[:, cols] = (proj(0) * conv).astype(BF16)

    if carried:
        pl.when(step % 2 == 0)(functools.partial(run, h0_ref, h1_ref))
        pl.when(step % 2 == 1)(functools.partial(run, h1_ref, h0_ref))
    else:
        run(None, None)


def _conv_in(x, g, w, cw, state, *, seg_len):
    m, d = x.shape
    tm = min(ROW_TILE, m)
    row = lambda i: (i, 0)
    if seg_len >= tm:
        tps = seg_len // tm
        st_spec = pl.BlockSpec((1, 8, d), lambda i: (i // tps, 0, 0))
        tail_spec = pl.BlockSpec((1, 8, d), lambda i: (i // tps, 0, 0))
        tail_shape = jax.ShapeDtypeStruct((m // seg_len, 8, d), F32)
    else:
        tps = 1
        st_spec = pl.BlockSpec((2, tm, d), lambda i: (0, i, 0))
        tail_spec = pl.BlockSpec((tm, d), row)
        tail_shape = jax.ShapeDtypeStruct((m, d), F32)
    return pl.pallas_call(
        functools.partial(_conv_kernel, tps=tps, seg_len=seg_len),
        grid=(m // tm,),
        in_specs=[pl.BlockSpec((tm, d), row),
                  pl.BlockSpec((tm, d), lambda i: (jnp.minimum(i + 1, m // tm - 1), 0)),
                  _resident((1, d)), _layer_of(*w), _resident(cw.shape), st_spec],
        out_specs=[pl.BlockSpec((tm, d), row), tail_spec],
        out_shape=[jax.ShapeDtypeStruct((m, d), BF16), tail_shape],
        scratch_shapes=[pltpu.VMEM((8, d), F32), pltpu.VMEM((tm, d), BF16), pltpu.VMEM((tm, d), BF16)],
        compiler_params=_params("arbitrary"),
        name="conv_in",
    )(x, x, g, w[0], cw, state)


def _ffn_kernel(*refs, n_parts, final):
    x_ref = refs[0]
    a_refs = refs[1:1 + n_parts]
    wo_ref, g_ref, wg_ref, wu_ref, wd_ref = refs[1 + n_parts:6 + n_parts]
    gf_ref = refs[6 + n_parts] if final else None
    o_ref, act_ref = refs[-2], refs[-1]

    tm = x_ref.shape[0]
    halves = [slice(0, tm // 2), slice(tm // 2, tm)] if tm % 32 == 0 else [slice(0, tm)]
    x1s = []
    for rows in halves:
        a = jnp.concatenate([r[rows, :] for r in a_refs], axis=1) if n_parts > 1 else a_refs[0][rows, :]
        x1s.append(x_ref[rows, :] + jnp.dot(a, wo_ref[...], preferred_element_type=F32))
    d_ff = wg_ref.shape[1]
    for rows, x1 in zip(halves, x1s):
        h = _rms(x1, g_ref[...]).astype(BF16)
        for c in range(d_ff // FF_CHUNK):
            cols = slice(c * FF_CHUNK, (c + 1) * FF_CHUNK)
            gate = jnp.dot(h, wg_ref[:, cols], preferred_element_type=F32)
            up = jnp.dot(h, wu_ref[:, cols], preferred_element_type=F32)
            act_ref[rows, cols] = (gate * (1.0 / (1.0 + jnp.exp(-gate))) * up).astype(BF16)
        y = x1 + jnp.dot(act_ref[rows, :], wd_ref[...], preferred_element_type=F32)
        o_ref[rows, :] = _rms(y, gf_ref[...]) if final else y


def _mix_out_ffn(x, parts, wo, g, wg, wu, wd, g_final=None):
    m, d = x.shape
    tm = min(ROW_TILE, m)
    row = lambda i: (i, 0)
    final = g_final is not None
    extra = [g_final] if final else []
    return pl.pallas_call(
        functools.partial(_ffn_kernel, n_parts=len(parts), final=final),
        grid=(m // tm,),
        in_specs=[pl.BlockSpec((tm, d), row)] + [pl.BlockSpec((tm, p.shape[1]), row) for p in parts]
                 + [_layer_of(*wo), _resident((1, d)), _layer_of(*wg), _layer_of(*wu), _layer_of(*wd)]
                 + [_resident((1, d))] * len(extra),
        out_specs=pl.BlockSpec((tm, d), row),
        out_shape=jax.ShapeDtypeStruct((m, d), F32),
        scratch_shapes=[pltpu.VMEM((tm, wg[0].shape[2]), BF16)],
        compiler_params=_params("arbitrary"),
        name="mix_out_ffn",
    )(x, *parts, wo[0], g, wg[0], wu[0], wd[0], *extra)


def kernel(x_prompt, x_sample, cache_a_k, cache_a_v, cache_b_k, cache_b_v, state_conv, norm_mix, norm_ffn, norm_final, w_attn_in, w_attn_out, rel_bias, lambda_q1, lambda_k1, lambda_q2, lambda_k2, subln_g, w_conv_in, conv_w, w_conv_out, w_ffn_gate, w_ffn_up, w_ffn_down):
    b, s, d = x_prompt.shape
    bs, t, _ = x_sample.shape
    depth = norm_mix.shape[0]
    n_attn = w_attn_in.shape[0]
    n_conv = w_conv_in.shape[0]
    past = cache_b_k.shape[2]
    a_past = cache_a_k.shape[2]
    keep = min(BAND_PAST, s)
    assert s % BAND_PAST == 0 and a_past == BAND_PAST and d % (2 * FF_CHUNK) == 0
    assert w_ffn_gate.shape[2] % FF_CHUNK == 0 and 2 <= t and a_past + t <= BIAS_LANES

    xp = x_prompt.reshape(b * s, d)
    xs = x_sample.reshape(bs * t, d)
    row = lambda v: v.reshape(1, -1)

    cos_p, sin_p = _rope_tables(jnp.arange(s))
    cos_s, sin_s = (jnp.tile(tab, (bs, 1)) for tab in _rope_tables(past + jnp.arange(t)))

    table = jnp.pad(rel_bias.reshape(n_attn * N_HEADS_A, 2 * MAX_REL + 1),
                    ((0, 0), (0, TABLE_LANES - 2 * MAX_REL - 1)))
    band_bias = _band_bias(table)
    sample_bias = _sample_bias(table, t)

    cak_t = cache_a_k.transpose(0, 1, 3, 4, 2).reshape(n_attn, bs, WIDTH, a_past)
    cav_t = cache_a_v.transpose(0, 1, 3, 4, 2).reshape(n_attn, bs, WIDTH, a_past)
    cbk_t = cache_b_k.transpose(0, 1, 3, 4, 5, 2).reshape(n_attn, bs, WIDTH, past)
    cbv_rows = cache_b_v.reshape(n_attn * bs * past * N_HEADS_B, LANES)

    attn_in_w, attn_out_w = w_attn_in.astype(BF16), w_attn_out.astype(BF16)
    conv_in_w, conv_out_w = w_conv_in.astype(BF16), w_conv_out.astype(BF16)
    gate_w, up_w, down_w = w_ffn_gate.astype(BF16), w_ffn_up.astype(BF16), w_ffn_down.astype(BF16)

    stacked_p = stacked_s = None
    pcs, scs = [], []
    for i in range(depth):
        j = i // 2
        g_mix = row(norm_mix[i])
        if i % 2 == 0:
            lam_init = 0.8 - 0.6 * math.exp(-0.3 * i)
            w_in = (attn_in_w, j)
            wo = (attn_out_w, j)
            lam_p = jnp.stack([lambda_q1[j], lambda_k1[j], lambda_q2[j], lambda_k2[j]])
            g_sub = row(subln_g[j])

            outs = _attn_in_proj(xp, g_mix, w_in, cos_p, sin_p, stacked_p,
                                 seg_len=s, keep=keep, layer=j, n_layers=n_attn, native=True)
            qa, ka, va_t, qb, kb, vb_t = outs[:6]
            qa, ka, qb, kb = (o.reshape(b, s, WIDTH) for o in (qa, ka, qb, kb))
            stacked_p = outs[6:]
            oa = _band_attn(qa, ka, va_t, band_bias, j)
            ob = _diff_attn(qb, kb, vb_t, lam_p, g_sub, lam_init)
            parts_p = [oa.reshape(b * s, WIDTH), ob.reshape(b * s, WIDTH)]

            outs = _attn_in_proj(xs, g_mix, w_in, cos_s, sin_s, stacked_s,
                                 seg_len=t, keep=t, layer=j, n_layers=n_attn, native=False)
            stacked_s = outs[6:]
            oa, ob = _sample_attn(
                *outs[:6], cak_t, cav_t, cbk_t, cbv_rows,
                sample_bias[j, :, :a_past], sample_bias[j, :, a_past:a_past + t],
                lam_p, g_sub, lam_init, t, j)
            parts_s = [oa, ob]
        else:
            w_in = (conv_in_w, j)
            wo = (conv_out_w, j)
            yp, tail = _conv_in(xp, g_mix, w_in, conv_w[j], jnp.zeros((b, 8, d), F32), seg_len=s)
            pcs.append(tail[:, 6:])
            st = state_conv[j]
            stand_in = jnp.stack([jnp.pad(st[:, 1:2], ((0, 0), (0, t - 1), (0, 0))),
                                  jnp.pad(st, ((0, 0), (0, t - 2), (0, 0)))]).reshape(2, bs * t, d)
            ys, xin = _conv_in(xs, g_mix, w_in, conv_w[j], stand_in, seg_len=t)
            scs.append(xin.reshape(bs, t, d)[:, t - 2:])
            parts_p, parts_s = [yp], [ys]

        ffn = (wo, row(norm_ffn[i]), (gate_w, i), (up_w, i), (down_w, i))
        g_final = row(norm_final) if i == depth - 1 else None
        xp = _mix_out_ffn(xp, parts_p, *ffn, g_final=g_final)
        xs = _mix_out_ffn(xs, parts_s, *ffn, g_final=g_final)

    pak, pav, pbk, pbv = stacked_p
    sak, sav, sbk, sbv = stacked_s
    heads_a = lambda z: z.reshape(n_attn, b, N_HEADS_A, HEAD_DIM, keep).transpose(0, 1, 4, 2, 3)
    return (xp.reshape(b, s, d), xs.reshape(bs, t, d),
            heads_a(pak), heads_a(pav),
            pbk.reshape(n_attn, b, N_HEADS_B, 2, HEAD_DIM, s).transpose(0, 1, 5, 2, 3, 4),
            pbv.reshape(n_attn, b, s, N_HEADS_B, 2 * HEAD_DIM),
            jnp.stack(pcs),
            sak.reshape(n_attn, bs, t, N_HEADS_A, HEAD_DIM), sav.reshape(n_attn, bs, t, N_HEADS_A, HEAD_DIM),
            sbk.reshape(n_attn, bs, t, N_HEADS_B, 2, HEAD_DIM), sbv.reshape(n_attn, bs, t, N_HEADS_B, 2 * HEAD_DIM),
            jnp.stack(scs))
```

```python
import functools
import math

import jax
import jax.numpy as jnp
from jax import lax
from jax.experimental import pallas as pl
from jax.experimental.pallas import tpu as pltpu

F32 = jnp.float32
BF16 = jnp.bfloat16

EPS = 1e-6
NEG = -1e30
LOG2E = 1.4426950408889634
ROPE_THETA = 10000.0
CHUNK = 64
HEAD_DIM = 64
N_HEADS_A = 8
N_HEADS_B = 4
BAND_PAST = 512
MAX_REL = 128
BAND_KEYS = BAND_PAST + CHUNK
BAND_Q = 128
BAND_WIN = BAND_Q + BAND_PAST
DIFF_KEYS = 512
BIAS_LANES = 640
WIDTH = N_HEADS_A * HEAD_DIM
LANES = 128
ROW_TILE = 512
FF_CHUNK = 256
V7X_VMEM_BYTES = 64 * 1024 * 1024
VMEM_LIMIT = V7X_VMEM_BYTES - 8 * 1024 * 1024

_NT = (((1,), (1,)), ((), ()))
_TN = (((0,), (0,)), ((), ()))
_NN = (((1,), (0,)), ((), ()))


def _resident(shape):
    return pl.BlockSpec(shape, lambda *_: (0,) * len(shape), pipeline_mode=pl.Buffered(1))


def _layer_of(stack, layer):
    return pl.BlockSpec((None,) + stack.shape[1:], lambda *_: (layer, 0, 0), pipeline_mode=pl.Buffered(1))


def _rms(x, g):
    return x * lax.rsqrt(jnp.mean(x * x, axis=-1, keepdims=True) + EPS) * g


def _params(*sem):
    return pltpu.CompilerParams(dimension_semantics=sem, vmem_limit_bytes=VMEM_LIMIT)


SUM_ROWS = 16


def _values_and_sum(v_t, p):
    ones = jnp.ones((SUM_ROWS, v_t.shape[1]), v_t.dtype)
    return jnp.dot(jnp.concatenate([v_t, ones], axis=0), p, preferred_element_type=F32)


TABLE_LANES = 384


def _table_lookup(rows, idx):
    lane = idx & (LANES - 1)
    group = idx >> (LANES.bit_length() - 1)
    out = None
    for g in range(TABLE_LANES // LANES):
        picked = jnp.take_along_axis(rows[:, g * LANES:(g + 1) * LANES], lane, axis=1)
        out = picked if out is None else jnp.where(group == g, picked, out)
    return out


def _split_maps(q):
    low = lax.broadcasted_iota(jnp.int32, q.shape, 1) < HEAD_DIM
    zero = jnp.zeros_like(q)
    return jnp.concatenate([jnp.where(low, q, zero), jnp.where(low, zero, q)], axis=0)


def _attn_in_kernel(*refs, native):
    x_ref, g_ref, w_ref, cos_ref, sin_ref = refs[:5]
    (qa_ref, ka_ref, va_ref, qb_ref, kb_ref, vb_ref,
     kaf_ref, vaf_ref, kbf_ref, vbf_ref) = refs[-10:]
    q_scale = LOG2E / math.sqrt(HEAD_DIM)
    tm = x_ref.shape[0]
    halves = [(0, tm // 2), (tm // 2, tm)] if tm % 32 == 0 else [(0, tm)]
    normed = [_rms(x_ref[lo:hi, :], g_ref[...]).astype(BF16) for lo, hi in halves]

    for (lo, hi), h in zip(halves, normed):
        def proj(c):
            return jnp.dot(h, w_ref[:, c * WIDTH:(c + 1) * WIDTH], preferred_element_type=F32)

        cos = cos_ref[lo:hi, :]
        sin = sin_ref[lo:hi, :]
        low_half = (lax.broadcasted_iota(jnp.int32, cos.shape, 1) % HEAD_DIM) < (HEAD_DIM // 2)

        def rotary(z):
            parts = []
            for j in range(WIDTH // LANES):
                zj = z[:, j * LANES:(j + 1) * LANES]
                swapped = jnp.where(low_half, pltpu.roll(zj, LANES - HEAD_DIM // 2, 1),
                                    pltpu.roll(zj, HEAD_DIM // 2, 1))
                parts.append(zj * cos + swapped * sin)
            return jnp.concatenate(parts, axis=1)

        vb = proj(5)
        if native:
            vb_ref[0, 0, :, lo:hi] = vb.T.astype(BF16)
            for head in range(N_HEADS_B):
                vbf_ref[pl.ds(lo * N_HEADS_B + head, hi - lo, stride=N_HEADS_B), :] = (
                    vb[:, head * LANES:(head + 1) * LANES])
        else:
            vb_ref[lo:hi, :] = vb.astype(BF16)
            vbf_ref[lo:hi, :] = vb
        kb = rotary(proj(4))
        kb_ref[lo:hi, :] = kb.astype(BF16)
        va = proj(2)
        ka = proj(1)
        ka_ref[lo:hi, :] = ka.astype(BF16)
        if native:
            kbf_ref[0, 0, :, lo:hi] = kb.T
            va_t = va.T
            vaf_ref[0, 0, :, lo:hi] = va_t
            va_ref[0, :, lo:hi] = va_t.astype(BF16)
            kaf_ref[0, 0, :, lo:hi] = ka.T
        else:
            kbf_ref[lo:hi, :] = kb
            vaf_ref[lo:hi, :] = va
            va_ref[lo:hi, :] = va.astype(BF16)
            kaf_ref[lo:hi, :] = ka
        qb_ref[lo:hi, :] = (rotary(proj(3)) * q_scale).astype(BF16)
        qa_ref[lo:hi, :] = (proj(0) * q_scale).astype(BF16)


def _attn_in_proj(x, g, w, cos_t, sin_t, stacked, *, seg_len, keep, layer, n_layers, native):
    m, d = x.shape
    tm = min(ROW_TILE, m)
    n_tiles = m // tm
    n_tab = cos_t.shape[0] // tm
    n_seg = m // seg_len
    row = lambda i: (i, 0)
    tab = lambda i: (i % n_tab, 0)
    blk = pl.BlockSpec((tm, WIDTH), row)
    bf_out = jax.ShapeDtypeStruct((m, WIDTH), BF16)
    if native:
        tps, kpt = seg_len // tm, keep // tm
        t_blk = (1, 1, WIDTH, tm)
        keep_spec = pl.BlockSpec(
            t_blk, lambda i: (layer, i // tps, 0, jnp.clip(i % tps - (tps - kpt), 0, kpt - 1)))
        keep_out = jax.ShapeDtypeStruct((n_layers, n_seg, WIDTH, keep), F32)
        kb_spec = pl.BlockSpec(t_blk, lambda i: (layer, i // tps, 0, i % tps))
        kb_out = jax.ShapeDtypeStruct((n_layers, n_seg, WIDTH, seg_len), F32)
        vb_spec = pl.BlockSpec((tm * N_HEADS_B, LANES), lambda i: (layer * n_tiles + i, 0))
        vb_out = jax.ShapeDtypeStruct((n_layers * m * N_HEADS_B, LANES), F32)
        va_spec = pl.BlockSpec((1, WIDTH, tm), lambda i: (i // tps, 0, i % tps))
        va_out = jax.ShapeDtypeStruct((n_seg, WIDTH, seg_len), BF16)
        vbb_spec = pl.BlockSpec((1, 1, WIDTH, tm), lambda i: (i // tps, i % tps, 0, 0))
        vbb_out = jax.ShapeDtypeStruct((n_seg, tps, WIDTH, tm), BF16)
    else:
        assert keep == seg_len
        keep_spec = kb_spec = vb_spec = pl.BlockSpec((tm, WIDTH), lambda i: (layer * n_tiles + i, 0))
        keep_out = kb_out = vb_out = jax.ShapeDtypeStruct((n_layers * m, WIDTH), F32)
        va_spec = vbb_spec = blk
        va_out = vbb_out = bf_out
    carried = [] if stacked is None else list(stacked)
    n_in = 5
    return pl.pallas_call(
        functools.partial(_attn_in_kernel, native=native),
        grid=(n_tiles,),
        in_specs=[pl.BlockSpec((tm, d), row), _resident((1, d)), _layer_of(*w),
                  pl.BlockSpec((tm, LANES), tab), pl.BlockSpec((tm, LANES), tab)]
                 + [pl.BlockSpec(memory_space=pl.ANY)] * len(carried),
        out_specs=[blk, blk, va_spec, blk, blk, vbb_spec, keep_spec, keep_spec, kb_spec, vb_spec],
        out_shape=[bf_out, bf_out, va_out, bf_out, bf_out, vbb_out, keep_out, keep_out, kb_out, vb_out],
        input_output_aliases={n_in + k: 6 + k for k in range(len(carried))},
        compiler_params=_params("arbitrary"),
        name="attn_in_proj",
    )(x, g, w[0], cos_t, sin_t, *carried)


def _rope_tables(pos):
    half = HEAD_DIM // 2
    inv = ROPE_THETA ** (-jnp.arange(half, dtype=F32) / half)
    ang = pos.astype(F32)[:, None] * inv[None, :]
    c, s = jnp.cos(ang), jnp.sin(ang)
    reps = LANES // HEAD_DIM
    return jnp.tile(c, (1, 2 * reps)), jnp.tile(jnp.concatenate([-s, s], axis=1), (1, reps))


def _band_bias_kernel(tab_ref, o_ref):
    r = pl.program_id(0)
    e_shape = (BAND_WIN + LANES, LANES)
    y = lax.broadcasted_iota(jnp.int32, e_shape, 0) - LANES
    lane = lax.broadcasted_iota(jnp.int32, e_shape, 1)
    idx = jnp.clip(BAND_PAST + lane - y, -MAX_REL, MAX_REL) + MAX_REL
    cols = []
    for hh in range(2):
        e = _table_lookup(jnp.broadcast_to(tab_ref[pl.ds(2 * r + hh, 1), :], (e_shape[0], TABLE_LANES)), idx)
        for c in range(BAND_Q // LANES):
            cols.append(e[LANES - LANES * c:LANES - LANES * c + BAND_WIN])
    bias = jnp.concatenate(cols, axis=1)

    kj = lax.broadcasted_iota(jnp.int32, bias.shape, 0)
    qi = lax.broadcasted_iota(jnp.int32, bias.shape, 1) % BAND_Q
    first_key = (qi // CHUNK) * CHUNK
    in_band = jnp.logical_and(kj >= first_key, kj < first_key + BAND_KEYS)
    o_ref[0] = jnp.where(in_band, bias * LOG2E, NEG)


def _band_bias(table):
    n = table.shape[0] // 2
    return pl.pallas_call(
        _band_bias_kernel,
        grid=(n,),
        in_specs=[_resident(table.shape)],
        out_specs=pl.BlockSpec((1, BAND_WIN, 2 * BAND_Q), lambda r: (r, 0, 0)),
        out_shape=jax.ShapeDtypeStruct((n, BAND_WIN, 2 * BAND_Q), F32),
        compiler_params=_params("arbitrary"),
        name="band_bias",
    )(table)


def _sample_bias_kernel(tab_ref, o_ref, *, t):
    rows = tab_ref[0]
    blocks = []
    for c in range(BIAS_LANES // LANES):
        shape = (rows.shape[0], LANES)
        frame = lax.broadcasted_iota(jnp.int32, shape, 0) % t
        kj = lax.broadcasted_iota(jnp.int32, shape, 1) + c * LANES
        idx = jnp.clip(BAND_PAST + frame - kj, -MAX_REL, MAX_REL) + MAX_REL
        blocks.append(_table_lookup(rows, idx))
    o_ref[0] = jnp.concatenate(blocks, axis=1) * LOG2E


def _sample_bias(table, t):
    n = table.shape[0] // N_HEADS_A
    per_row = jnp.repeat(table.reshape(n, N_HEADS_A, TABLE_LANES), t, axis=1)
    return pl.pallas_call(
        functools.partial(_sample_bias_kernel, t=t),
        grid=(n,),
        in_specs=[pl.BlockSpec((1, N_HEADS_A * t, TABLE_LANES), lambda r: (r, 0, 0))],
        out_specs=pl.BlockSpec((1, N_HEADS_A * t, BIAS_LANES), lambda r: (r, 0, 0)),
        out_shape=jax.ShapeDtypeStruct((n, N_HEADS_A * t, BIAS_LANES), F32),
        compiler_params=_params("arbitrary"),
        name="sample_bias",
    )(per_row)


def _band_kernel(q_ref, kp_ref, kc_ref, vp_ref, vc_ref, bias_ref, o_ref, *bufs):
    qblk = pl.program_id(2)
    n_sub = q_ref.shape[1] // BAND_Q

    def window(prev_ref, cur_ref, sub, first, axis):
        lo, hi = sub * BAND_Q, (sub + 1) * BAND_Q
        take = lambda ref, a, b: ref[0, a:b] if axis == 0 else ref[0, :, a:b]
        parts = []
        if lo < BAND_PAST and not first:
            parts.append(take(prev_ref, lo, BAND_PAST))
        parts.append(take(cur_ref, max(lo - BAND_PAST, 0), hi))
        return parts[0] if len(parts) == 1 else jnp.concatenate(parts, axis=axis)

    def scores(sub, first):
        keys = window(kp_ref, kc_ref, sub, first, 0)
        nk = keys.shape[0]
        q = q_ref[0, sub * BAND_Q:(sub + 1) * BAND_Q, :]
        s = lax.dot_general(keys, _split_maps(q), _NT, preferred_element_type=F32)
        s = s + bias_ref[0, BAND_WIN - nk:, :]
        bufs[2 * sub][:nk] = s
        bufs[2 * sub + 1][...] = jnp.max(s, axis=0, keepdims=True)
        return nk

    def finish(nk, sub, first):
        p = jnp.exp2(bufs[2 * sub][:nk] - bufs[2 * sub + 1][...]).astype(BF16)
        o_t = _values_and_sum(window(vp_ref, vc_ref, sub, first, 1), p)
        o_t = o_t[:LANES] / o_t[LANES:LANES + 1]
        o = jnp.concatenate([o_t[:HEAD_DIM, :BAND_Q], o_t[HEAD_DIM:, BAND_Q:]], axis=0).T
        o_ref[0, sub * BAND_Q:(sub + 1) * BAND_Q, :] = o.astype(BF16)

    def run(first):
        sizes = [scores(sub, first) for sub in range(n_sub)]
        for sub in range(n_sub):
            finish(sizes[sub], sub, first)

    pl.when(qblk == 0)(functools.partial(run, True))
    pl.when(qblk > 0)(functools.partial(run, False))


def _band_attn(q, k, v_t, bias, layer):
    b, s, _ = q.shape
    bq = 2 * BAND_PAST if s % (2 * BAND_PAST) == 0 else BAND_PAST
    n_pairs = N_HEADS_A // 2
    prev_of = lambda qi: jnp.maximum(qi * (bq // BAND_PAST) - 1, 0)
    cur = lambda hp, bi, qi: (bi, qi, hp)
    prev = lambda hp, bi, qi: (bi, prev_of(qi), hp)
    cur_t = lambda hp, bi, qi: (bi, hp, qi)
    prev_t = lambda hp, bi, qi: (bi, hp, prev_of(qi))
    blk = (1, bq, LANES)
    return pl.pallas_call(
        _band_kernel,
        grid=(n_pairs, b, s // bq),
        in_specs=[pl.BlockSpec(blk, cur), pl.BlockSpec((1, BAND_PAST, LANES), prev), pl.BlockSpec(blk, cur),
                  pl.BlockSpec((1, LANES, BAND_PAST), prev_t), pl.BlockSpec((1, LANES, bq), cur_t),
                  pl.BlockSpec((1, BAND_WIN, 2 * BAND_Q), lambda hp, bi, qi: (layer * n_pairs + hp, 0, 0))],
        out_specs=pl.BlockSpec(blk, cur),
        out_shape=jax.ShapeDtypeStruct((b, s, WIDTH), BF16),
        scratch_shapes=[pltpu.VMEM((BAND_WIN, 2 * BAND_Q), F32), pltpu.VMEM((1, 2 * BAND_Q), F32)]
                       * (bq // BAND_Q),
        compiler_params=_params("arbitrary", "arbitrary", "arbitrary"),
        name="band_attn",
    )(q, k, k, v_t, v_t, bias)


def _lambda(lam_ref, lam_init):
    lp = lam_ref[...]
    a1 = jnp.sum(lp[0:1] * lp[1:2], axis=-1, keepdims=True)
    a2 = jnp.sum(lp[2:3] * lp[3:4], axis=-1, keepdims=True)
    return jnp.exp(a1) - jnp.exp(a2) + lam_init


def _diff_kernel(lam_ref, g_ref, q_ref, qn_ref, k_ref, v_ref, o_ref, m_ref, acc_ref, *bufs, lam_init):
    qi = pl.program_id(2)
    q = q_ref[0]
    bq = q.shape[0]
    bk = bq // 2
    q2 = _split_maps(q)
    queries = (q2[:bq], q2[bq:])

    m_ref[...] = jnp.full(m_ref.shape, NEG, F32)
    acc_ref[...] = jnp.zeros(acc_ref.shape, F32)

    def block(ref, kj):
        return ref[0, pl.ds(pl.multiple_of(kj * bk, bk), bk), :]

    def scores(kj, stream, buf, diagonal=None, queries=queries):
        s_ref, mx_ref = bufs[4 * stream + 2 * buf], bufs[4 * stream + 2 * buf + 1]
        qs = queries[stream][bk:] if diagonal == 1 else queries[stream]
        s = lax.dot_general(block(k_ref, kj), qs, _NT, preferred_element_type=F32)
        if diagonal is not None:
            k_chunk = lax.broadcasted_iota(jnp.int32, s.shape, 0) // CHUNK
            q_chunk = lax.broadcasted_iota(jnp.int32, s.shape, 1) // CHUNK
            s = jnp.where(k_chunk <= q_chunk, s, NEG)
        n = s.shape[1]
        s_ref[:, :n] = s
        mx_ref[:, :n] = jnp.max(s, axis=0, keepdims=True)

    def absorb(kj, stream, buf, second_half=False):
        s_ref, mx_ref = bufs[4 * stream + 2 * buf], bufs[4 * stream + 2 * buf + 1]
        n = bk if second_half else bq
        cols = slice((stream + 1) * bq - n, (stream + 1) * bq)
        m_old = m_ref[:, cols]
        m_new = jnp.maximum(m_old, mx_ref[:, :n])
        alpha = jnp.exp2(m_old - m_new)
        p = jnp.exp2(s_ref[:, :n] - m_new).astype(BF16)
        acc_ref[:, cols] = alpha * acc_ref[:, cols] + _values_and_sum(v_ref[0, kj], p)
        m_ref[:, cols] = m_new

    def stage(kj, buf, look=True, look_diagonal=None, second_half=False):
        if look:
            scores(kj + 1, 0, 1 - buf, look_diagonal)
        absorb(kj, 1, buf, second_half)
        if look:
            scores(kj + 1, 1, 1 - buf, look_diagonal)
        absorb(kj, 0, buf, second_half)

    @pl.when(qi == 0)
    def _():
        scores(0, 0, 0, diagonal=0)
        scores(0, 1, 0, diagonal=0)
        stage(0, 0, look_diagonal=1)
        stage(1, 1, look=False, second_half=True)


    def two_blocks(pair, carry):
        stage(2 * pair, 0)
        stage(2 * pair + 1, 1)
        return carry

    lax.fori_loop(0, jnp.maximum(qi - 1, 0), two_blocks, 0)

    @pl.when(qi > 0)
    def _():
        last = 2 * qi
        stage(last - 2, 0)
        stage(last - 1, 1, look_diagonal=0)
        stage(last, 0, look_diagonal=1)
        stage(last + 1, 1, look=False, second_half=True)

    def finalize():
        o_t = acc_ref[:LANES] / acc_ref[LANES:LANES + 1]
        lam = _lambda(lam_ref, lam_init)
        d_t = o_t[:, :bq] - lam * o_t[:, bq:]
        inv = lax.rsqrt(jnp.mean(d_t * d_t, axis=0, keepdims=True) + EPS)
        o_ref[0] = ((d_t * inv).T * (g_ref[...] * (1.0 - lam_init))).astype(BF16)

    @pl.when(qi + 1 < pl.num_programs(2))
    def _():
        nxt = _split_maps(qn_ref[0])
        scores(0, 0, 0, queries=(nxt[:bq], nxt[bq:]))
        scores(0, 1, 0, queries=(nxt[:bq], nxt[bq:]))
        finalize()

    pl.when(qi + 1 == pl.num_programs(2))(finalize)


def _diff_attn(q, k, v_t, lam_p, g, lam_init):
    b, s, _ = q.shape
    bq = 2 * DIFF_KEYS
    assert s % bq == 0 and v_t.shape[3] == DIFF_KEYS
    seq = pl.BlockSpec((1, s, LANES), lambda bi, h, qi: (bi, 0, h))
    seq_t = pl.BlockSpec((1, s // DIFF_KEYS, LANES, DIFF_KEYS), lambda bi, h, qi: (bi, 0, h, 0))
    blk = pl.BlockSpec((1, bq, LANES), lambda bi, h, qi: (bi, qi, h))
    return pl.pallas_call(
        functools.partial(_diff_kernel, lam_init=lam_init),
        grid=(b, N_HEADS_B, s // bq),
        in_specs=[_resident(lam_p.shape), _resident(g.shape), blk,
                  pl.BlockSpec((1, bq, LANES), lambda bi, h, qi: (bi, jnp.minimum(qi + 1, s // bq - 1), h)),
                  seq, seq_t],
        out_specs=blk,
        out_shape=jax.ShapeDtypeStruct((b, s, WIDTH), BF16),
        scratch_shapes=[pltpu.VMEM((1, 2 * bq), F32), pltpu.VMEM((LANES + SUM_ROWS, 2 * bq), F32),
                        ] + [pltpu.VMEM((DIFF_KEYS, bq), F32), pltpu.VMEM((1, bq), F32)] * 4,
        compiler_params=_params("arbitrary", "arbitrary", "arbitrary"),
        name="diff_attn",
    )(lam_p, g, q, q, k, v_t)


def _sample_kernel(lam_ref, g_ref, qa_ref, kan_ref, van_ref, qb_ref, kbn_ref, vbn_ref,
                   cak_ref, cav_ref, cbk_ref, cbv_ref, bias_c_ref, bias_n_ref,
                   oa_ref, ob_ref, *, lam_init):
    t = qa_ref.shape[0]
    n_grp = WIDTH // HEAD_DIM
    rows = n_grp * t
    grp_of_row = lax.broadcasted_iota(jnp.int32, (rows, WIDTH), 0) // t
    grp_of_lane = lax.broadcasted_iota(jnp.int32, (rows, WIDTH), 1) // HEAD_DIM
    own = grp_of_row == grp_of_lane
    lane_t = lax.broadcasted_iota(jnp.int32, (t, WIDTH), 1)

    def expand(q):
        qe = jnp.concatenate([q] * n_grp, axis=0)
        return jnp.where(own, qe, jnp.zeros_like(qe))

    def attend(qe, kt_cache, k_new, v_cache, v_new, bias_c, bias_n, v_dims):
        s_c = jnp.dot(qe, kt_cache, preferred_element_type=F32)
        s_n = lax.dot_general(qe, k_new, _NT, preferred_element_type=F32)
        if bias_c is not None:
            s_c = s_c + bias_c
            s_n = s_n + bias_n
        m = jnp.maximum(jnp.max(s_c, axis=-1, keepdims=True), jnp.max(s_n, axis=-1, keepdims=True))
        p_c = jnp.exp2(s_c - m)
        p_n = jnp.exp2(s_n - m)
        l = jnp.sum(p_c, axis=-1, keepdims=True) + jnp.sum(p_n, axis=-1, keepdims=True)
        o = (lax.dot_general(p_c.astype(BF16), v_cache, v_dims, preferred_element_type=F32)
             + jnp.dot(p_n.astype(BF16), v_new, preferred_element_type=F32))
        return o / l

    oa_all = attend(expand(qa_ref[...]), cak_ref[0, 0].astype(BF16), kan_ref[...],
                    cav_ref[0, 0].astype(BF16), van_ref[...], bias_c_ref[...], bias_n_ref[...], _NT)
    oa = jnp.zeros((t, WIDTH), F32)
    for h in range(N_HEADS_A):
        oa = jnp.where(lane_t // HEAD_DIM == h, oa_all[h * t:(h + 1) * t], oa)
    oa_ref[...] = oa.astype(BF16)

    past = cbk_ref.shape[3]
    v_cache = jnp.concatenate([cbv_ref[pl.ds(h, past, stride=N_HEADS_B), :] for h in range(N_HEADS_B)],
                              axis=1).astype(BF16)
    ob_all = attend(expand(qb_ref[...]), cbk_ref[0, 0].astype(BF16), kbn_ref[...],
                    v_cache, vbn_ref[...], None, None, _NN)
    lam = _lambda(lam_ref, lam_init)
    g = g_ref[...]
    outs = []
    for h in range(N_HEADS_B):
        d = ob_all[2 * h * t:(2 * h + 1) * t] - lam * ob_all[(2 * h + 1) * t:(2 * h + 2) * t]
        outs.append(_rms(d[:, h * LANES:(h + 1) * LANES], g) * (1.0 - lam_init))
    ob_ref[...] = jnp.concatenate(outs, axis=1).astype(BF16)


def _sample_attn(qa, ka, va, qb, kb, vb, cak, cav, cbk, cbv, bias_c, bias_n, lam_p, g, lam_init, t, layer):
    m = qa.shape[0]
    bs = m // t
    past = cbk.shape[3]
    new = pl.BlockSpec((t, WIDTH), lambda i: (i, 0))
    cache = lambda c: pl.BlockSpec((1, 1) + c.shape[2:], lambda i: (layer, i, 0, 0))
    cbv_spec = pl.BlockSpec((past * N_HEADS_B, LANES), lambda i: (layer * bs + i, 0))
    out = jax.ShapeDtypeStruct((m, WIDTH), BF16)
    return pl.pallas_call(
        functools.partial(_sample_kernel, lam_init=lam_init),
        grid=(bs,),
        in_specs=[_resident(lam_p.shape), _resident(g.shape)] + [new] * 6
                 + [cache(cak), cache(cav), cache(cbk), cbv_spec,
                    _resident(bias_c.shape), _resident(bias_n.shape)],
        out_specs=[new, new],
        out_shape=[out, out],
        compiler_params=_params("arbitrary"),
        name="sample_attn",
    )(lam_p, g, qa, ka, va, qb, kb, vb, cak, cav, cbk, cbv, bias_c, bias_n)


def _conv_kernel(x_ref, xn_ref, g_ref, w_ref, cw_ref, st_ref, y_ref, tail_ref, carry_ref, h0_ref, h1_ref,
                 *, tps, seg_len):
    tm, d = x_ref.shape
    step = pl.program_id(0)
    carried = seg_len >= tm
    if carried:
        @pl.when(step % tps == 0)
        def _():
            carry_ref[...] = st_ref[0]

        @pl.when(step == 0)
        def _():
            h0_ref[...] = _rms(x_ref[...], g_ref[...]).astype(BF16)

    def run(h_ref, next_ref):
        if next_ref is not None:
            next_ref[...] = _rms(xn_ref[...], g_ref[...]).astype(BF16)
        h = h_ref[...] if h_ref is not None else _rms(x_ref[...], g_ref[...]).astype(BF16)
        cw = cw_ref[...]
        cc = 2 * FF_CHUNK
        for c in range(d // cc):
            cols = slice(c * cc, (c + 1) * cc)
            proj = lambda part: jnp.dot(h, w_ref[:, part * d + c * cc:part * d + (c + 1) * cc],
                                        preferred_element_type=F32)
            xin = proj(1) * proj(2)
            r = lax.broadcasted_iota(jnp.int32, xin.shape, 0)
            x1 = pltpu.roll(xin, 1, 0)
            x2 = pltpu.roll(xin, 2, 0)
            if carried:
                prev = carry_ref[:, cols]
                x1 = jnp.where(r == 0, prev[7:8], x1)
                x2 = jnp.where(r == 0, prev[6:7], jnp.where(r == 1, prev[7:8], x2))
                carry_ref[:, cols] = xin[tm - 8:]
                tail_ref[0, :, cols] = xin[tm - 8:]
            else:
                t = r % seg_len
                x1 = jnp.where(t >= 1, x1, st_ref[0, :, cols])
                x2 = jnp.where(t >= 2, x2, st_ref[1, :, cols])
                tail_ref[:, cols] = xin
            conv = cw[0:1, cols] * x2 + cw[1:2, cols] * x1 + cw[2:3, cols] * xin
            y_ref[:, cols] = (proj(0) * conv).astype(BF16)

    if carried:
        pl.when(step % 2 == 0)(functools.partial(run, h0_ref, h1_ref))
        pl.when(step % 2 == 1)(functools.partial(run, h1_ref, h0_ref))
    else:
        run(None, None)


def _conv_in(x, g, w, cw, state, *, seg_len):
    m, d = x.shape
    tm = min(ROW_TILE, m)
    row = lambda i: (i, 0)
    if seg_len >= tm:
        tps = seg_len // tm
        st_spec = pl.BlockSpec((1, 8, d), lambda i: (i // tps, 0, 0))
        tail_spec = pl.BlockSpec((1, 8, d), lambda i: (i // tps, 0, 0))
        tail_shape = jax.ShapeDtypeStruct((m // seg_len, 8, d), F32)
    else:
        tps = 1
        st_spec = pl.BlockSpec((2, tm, d), lambda i: (0, i, 0))
        tail_spec = pl.BlockSpec((tm, d), row)
        tail_shape = jax.ShapeDtypeStruct((m, d), F32)
    return pl.pallas_call(
        functools.partial(_conv_kernel, tps=tps, seg_len=seg_len),
        grid=(m // tm,),
        in_specs=[pl.BlockSpec((tm, d), row),
                  pl.BlockSpec((tm, d), lambda i: (jnp.minimum(i + 1, m // tm - 1), 0)),
                  _resident((1, d)), _layer_of(*w), _resident(cw.shape), st_spec],
        out_specs=[pl.BlockSpec((tm, d), row), tail_spec],
        out_shape=[jax.ShapeDtypeStruct((m, d), BF16), tail_shape],
        scratch_shapes=[pltpu.VMEM((8, d), F32), pltpu.VMEM((tm, d), BF16), pltpu.VMEM((tm, d), BF16)],
        compiler_params=_params("arbitrary"),
        name="conv_in",
    )(x, x, g, w[0], cw, state)


def _ffn_kernel(*refs, n_parts, final):
    x_ref = refs[0]
    a_refs = refs[1:1 + n_parts]
    wo_ref, g_ref, wg_ref, wu_ref, wd_ref = refs[1 + n_parts:6 + n_parts]
    gf_ref = refs[6 + n_parts] if final else None
    o_ref, act_ref = refs[-2], refs[-1]

    tm = x_ref.shape[0]
    halves = [slice(0, tm // 2), slice(tm // 2, tm)] if tm % 32 == 0 else [slice(0, tm)]
    x1s = []
    for rows in halves:
        a = jnp.concatenate([r[rows, :] for r in a_refs], axis=1) if n_parts > 1 else a_refs[0][rows, :]
        x1s.append(x_ref[rows, :] + jnp.dot(a, wo_ref[...], preferred_element_type=F32))
    d_ff = wg_ref.shape[1]
    for rows, x1 in zip(halves, x1s):
        h = _rms(x1, g_ref[...]).astype(BF16)
        for c in range(d_ff // FF_CHUNK):
            cols = slice(c * FF_CHUNK, (c + 1) * FF_CHUNK)
            gate = jnp.dot(h, wg_ref[:, cols], preferred_element_type=F32)
            up = jnp.dot(h, wu_ref[:, cols], preferred_element_type=F32)
            act_ref[rows, cols] = (gate * (1.0 / (1.0 + jnp.exp(-gate))) * up).astype(BF16)
        y = x1 + jnp.dot(act_ref[rows, :], wd_ref[...], preferred_element_type=F32)
        o_ref[rows, :] = _rms(y, gf_ref[...]) if final else y


def _mix_out_ffn(x, parts, wo, g, wg, wu, wd, g_final=None):
    m, d = x.shape
    tm = min(ROW_TILE, m)
    row = lambda i: (i, 0)
    final = g_final is not None
    extra = [g_final] if final else []
    return pl.pallas_call(
        functools.partial(_ffn_kernel, n_parts=len(parts), final=final),
        grid=(m // tm,),
        in_specs=[pl.BlockSpec((tm, d), row)] + [pl.BlockSpec((tm, p.shape[1]), row) for p in parts]
                 + [_layer_of(*wo), _resident((1, d)), _layer_of(*wg), _layer_of(*wu), _layer_of(*wd)]
                 + [_resident((1, d))] * len(extra),
        out_specs=pl.BlockSpec((tm, d), row),
        out_shape=jax.ShapeDtypeStruct((m, d), F32),
        scratch_shapes=[pltpu.VMEM((tm, wg[0].shape[2]), BF16)],
        compiler_params=_params("arbitrary"),
        name="mix_out_ffn",
    )(x, *parts, wo[0], g, wg[0], wu[0], wd[0], *extra)


def kernel(x_prompt, x_sample, cache_a_k, cache_a_v, cache_b_k, cache_b_v, state_conv, norm_mix, norm_ffn, norm_final, w_attn_in, w_attn_out, rel_bias, lambda_q1, lambda_k1, lambda_q2, lambda_k2, subln_g, w_conv_in, conv_w, w_conv_out, w_ffn_gate, w_ffn_up, w_ffn_down):
    b, s, d = x_prompt.shape
    bs, t, _ = x_sample.shape
    depth = norm_mix.shape[0]
    n_attn = w_attn_in.shape[0]
    n_conv = w_conv_in.shape[0]
    past = cache_b_k.shape[2]
    a_past = cache_a_k.shape[2]
    keep = min(BAND_PAST, s)
    assert s % BAND_PAST == 0 and a_past == BAND_PAST and d % (2 * FF_CHUNK) == 0
    assert w_ffn_gate.shape[2] % FF_CHUNK == 0 and 2 <= t and a_past + t <= BIAS_LANES

    xp = x_prompt.reshape(b * s, d)
    xs = x_sample.reshape(bs * t, d)
    row = lambda v: v.reshape(1, -1)

    cos_p, sin_p = _rope_tables(jnp.arange(s))
    cos_s, sin_s = (jnp.tile(tab, (bs, 1)) for tab in _rope_tables(past + jnp.arange(t)))

    table = jnp.pad(rel_bias.reshape(n_attn * N_HEADS_A, 2 * MAX_REL + 1),
                    ((0, 0), (0, TABLE_LANES - 2 * MAX_REL - 1)))
    band_bias = _band_bias(table)
    sample_bias = _sample_bias(table, t)

    cak_t = cache_a_k.transpose(0, 1, 3, 4, 2).reshape(n_attn, bs, WIDTH, a_past)
    cav_t = cache_a_v.transpose(0, 1, 3, 4, 2).reshape(n_attn, bs, WIDTH, a_past)
    cbk_t = cache_b_k.transpose(0, 1, 3, 4, 5, 2).reshape(n_attn, bs, WIDTH, past)
    cbv_rows = cache_b_v.reshape(n_attn * bs * past * N_HEADS_B, LANES)

    attn_in_w, attn_out_w = w_attn_in.astype(BF16), w_attn_out.astype(BF16)
    conv_in_w, conv_out_w = w_conv_in.astype(BF16), w_conv_out.astype(BF16)
    gate_w, up_w, down_w = w_ffn_gate.astype(BF16), w_ffn_up.astype(BF16), w_ffn_down.astype(BF16)

    stacked_p = stacked_s = None
    pcs, scs = [], []
    for i in range(depth):
        j = i // 2
        g_mix = row(norm_mix[i])
        if i % 2 == 0:
            lam_init = 0.8 - 0.6 * math.exp(-0.3 * i)
            w_in = (attn_in_w, j)
            wo = (attn_out_w, j)
            lam_p = jnp.stack([lambda_q1[j], lambda_k1[j], lambda_q2[j], lambda_k2[j]])
            g_sub = row(subln_g[j])

            outs = _attn_in_proj(xp, g_mix, w_in, cos_p, sin_p, stacked_p,
                                 seg_len=s, keep=keep, layer=j, n_layers=n_attn, native=True)
            qa, ka, va_t, qb, kb, vb_t = outs[:6]
            qa, ka, qb, kb = (o.reshape(b, s, WIDTH) for o in (qa, ka, qb, kb))
            stacked_p = outs[6:]
            oa = _band_attn(qa, ka, va_t, band_bias, j)
            ob = _diff_attn(qb, kb, vb_t, lam_p, g_sub, lam_init)
            parts_p = [oa.reshape(b * s, WIDTH), ob.reshape(b * s, WIDTH)]

            outs = _attn_in_proj(xs, g_mix, w_in, cos_s, sin_s, stacked_s,
                                 seg_len=t, keep=t, layer=j, n_layers=n_attn, native=False)
            stacked_s = outs[6:]
            oa, ob = _sample_attn(
                *outs[:6], cak_t, cav_t, cbk_t, cbv_rows,
                sample_bias[j, :, :a_past], sample_bias[j, :, a_past:a_past + t],
                lam_p, g_sub, lam_init, t, j)
            parts_s = [oa, ob]
        else:
            w_in = (conv_in_w, j)
            wo = (conv_out_w, j)
            yp, tail = _conv_in(xp, g_mix, w_in, conv_w[j], jnp.zeros((b, 8, d), F32), seg_len=s)
            pcs.append(tail[:, 6:])
            st = state_conv[j]
            stand_in = jnp.stack([jnp.pad(st[:, 1:2], ((0, 0), (0, t - 1), (0, 0))),
                                  jnp.pad(st, ((0, 0), (0, t - 2), (0, 0)))]).reshape(2, bs * t, d)
            ys, xin = _conv_in(xs, g_mix, w_in, conv_w[j], stand_in, seg_len=t)
            scs.append(xin.reshape(bs, t, d)[:, t - 2:])
            parts_p, parts_s = [yp], [ys]

        ffn = (wo, row(norm_ffn[i]), (gate_w, i), (up_w, i), (down_w, i))
        g_final = row(norm_final) if i == depth - 1 else None
        xp = _mix_out_ffn(xp, parts_p, *ffn, g_final=g_final)
        xs = _mix_out_ffn(xs, parts_s, *ffn, g_final=g_final)

    pak, pav, pbk, pbv = stacked_p
    sak, sav, sbk, sbv = stacked_s
    heads_a = lambda z: z.reshape(n_attn, b, N_HEADS_A, HEAD_DIM, keep).transpose(0, 1, 4, 2, 3)
    return (xp.reshape(b, s, d), xs.reshape(bs, t, d),
            heads_a(pak), heads_a(pav),
            pbk.reshape(n_attn, b, N_HEADS_B, 2, HEAD_DIM, s).transpose(0, 1, 5, 2, 3, 4),
            pbv.reshape(n_attn, b, s, N_HEADS_B, 2 * HEAD_DIM),
            jnp.stack(pcs),
            sak.reshape(n_attn, bs, t, N_HEADS_A, HEAD_DIM), sav.reshape(n_attn, bs, t, N_HEADS_A, HEAD_DIM),
            sbk.reshape(n_attn, bs, t, N_HEADS_B, 2, HEAD_DIM), sbv.reshape(n_attn, bs, t, N_HEADS_B, 2 * HEAD_DIM),
            jnp.stack(scs))
```
